```python
import math
import jax, jax.numpy as jnp
from jax import lax
import numpy as np

D_MODEL = 1024
BATCH = 8
SEQ = 2048
DEPTH = 1

GRID_W = 64
CTX_LEN = 256
D_MIX = D_MODEL
F_WIDTH = D_MIX // 4
N_FGROUPS = 4
FGROUP_DIM = F_WIDTH // N_FGROUPS
SSD_WIDTH = D_MIX - F_WIDTH
SSD_HEAD_DIM = 64
SSD_HEADS = SSD_WIDTH // SSD_HEAD_DIM
SSD_GROUPS = 4
HEADS_PER_GROUP = SSD_HEADS // SSD_GROUPS
D_STATE = 128
BC_WIDTH = SSD_GROUPS * D_STATE
CONV_DIM = SSD_WIDTH + 2 * BC_WIDTH
D_CONV = 7
CHUNK = 128
XBC_OFF = F_WIDTH + SSD_WIDTH
IN_WIDTH = XBC_OFF + CONV_DIM + 2 * SSD_HEADS
D_FF = -(-8 * D_MODEL // (3 * 256)) * 256
N_MOD = 6
EPS = 1e-6

kernel_name = 'fourier_ssd_hybrid_prefix_block'


def rms_norm(x, g):
    xf = x.astype(jnp.float32)
    y = xf * lax.rsqrt(jnp.mean(xf * xf, axis=-1, keepdims=True) + EPS)
    return (y * g.astype(jnp.float32)).astype(x.dtype)


def modulate(h, shift, scale):
    return h * (1 + scale) + shift


def conv_rows(u, w, bias, rows, row_len):
    b, L, ch = u.shape
    half = D_CONV // 2
    up = jnp.pad(u.reshape(b, rows, row_len, ch), ((0, 0), (0, 0), (half, half), (0, 0)))
    out = bias
    for k in range(D_CONV):
        out = out + up[:, :, k:k + row_len, :] * w[:, k]
    return out.reshape(b, L, ch)


def fourier_mix(u):
    b, L, _ = u.shape
    ug = u.reshape(b, L, N_FGROUPS, FGROUP_DIM).astype(jnp.float32)
    yf = jnp.fft.fft2(ug, axes=(1, 3), norm='ortho').real
    return yf.reshape(b, L, F_WIDTH).astype(u.dtype)


def ssd_scan(xdt, dA, Bm, Cm, h0, with_output):
    b, L = xdt.shape[:2]
    nc = L // CHUNK
    xq = xdt.reshape(b, nc, CHUNK, SSD_GROUPS, HEADS_PER_GROUP, SSD_HEAD_DIM)
    aq = dA.reshape(b, nc, CHUNK, SSD_GROUPS, HEADS_PER_GROUP)
    Bq = Bm.reshape(b, nc, CHUNK, SSD_GROUPS, D_STATE)
    Cq = Cm.reshape(b, nc, CHUNK, SSD_GROUPS, D_STATE)
    a_cs = jnp.cumsum(aq, axis=2)
    a_end = a_cs[:, :, -1]
    st = jnp.einsum('bcsgn,bcsgr,bcsgrp->bcgrpn', Bq, jnp.exp(a_end[:, :, None] - a_cs), xq)

    def step(h, inp):
        s_c, a_c = inp
        return jnp.exp(a_c)[..., None, None] * h + s_c, h

    h_fin, h_in = lax.scan(step, h0, (jnp.moveaxis(st, 1, 0), jnp.moveaxis(a_end, 1, 0)))
    if not with_output:
        return None, h_fin
    h_in = jnp.moveaxis(h_in, 0, 1)
    seg = a_cs[:, :, :, None] - a_cs[:, :, None]
    lower = jnp.tril(jnp.ones((CHUNK, CHUNK), dtype=bool))[:, :, None, None]
    decay = jnp.exp(jnp.where(lower, seg, -jnp.inf))
    scores = jnp.einsum('bcqgn,bcsgn->bcqsg', Cq, Bq)
    y_diag = jnp.einsum('bcqsg,bcqsgr,bcsgrp->bcqgrp', scores, decay, xq)
    y_off = jnp.einsum('bcqgn,bcgrpn,bcqgr->bcqgrp', Cq, h_in, jnp.exp(a_cs))
    return (y_diag + y_off).reshape(b, L, SSD_HEADS, SSD_HEAD_DIM), h_fin


def ssd_direction(xs, Bm, Cm, dt_raw, dt_bias, a_log, h0, with_output):
    b, L, _ = xs.shape
    dt = jax.nn.softplus(dt_raw.astype(jnp.float32) + dt_bias.astype(jnp.float32))
    dA = dt * -jnp.exp(a_log.astype(jnp.float32))
    xdt = xs.reshape(b, L, SSD_HEADS, SSD_HEAD_DIM) * dt[..., None]
    return ssd_scan(xdt, dA, Bm.reshape(b, L, SSD_GROUPS, D_STATE),
                    Cm.reshape(b, L, SSD_GROUPS, D_STATE), h0, with_output)


def ssd_bidir(xs, Bm, Cm, dt_raw, dt_bias, a_log, h0_fwd, h0_bwd, with_output):
    flip = lambda t: jnp.flip(t, axis=1)
    y_f, h_f = ssd_direction(xs, Bm, Cm, dt_raw[..., :SSD_HEADS], dt_bias[0], a_log[0], h0_fwd, with_output)
    y_b, h_b = ssd_direction(flip(xs), flip(Bm), flip(Cm), flip(dt_raw[..., SSD_HEADS:]),
                             dt_bias[1], a_log[1], h0_bwd, with_output)
    y = y_f + flip(y_b) if with_output else None
    return y, h_f, h_b


def mixer_inputs(h, w_in, conv_w, conv_b, rows, row_len, full):
    p = h @ (w_in if full else w_in[:, XBC_OFF:])
    if full:
        u_f, z, rest = p[..., :F_WIDTH], p[..., F_WIDTH:XBC_OFF], p[..., XBC_OFF:]
    else:
        u_f, z, rest = None, None, p
    xbc = jax.nn.silu(conv_rows(rest[..., :CONV_DIM], conv_w, conv_b, rows, row_len))
    dt_raw = rest[..., CONV_DIM:]
    xs = xbc[..., :SSD_WIDTH]
    Bm = xbc[..., SSD_WIDTH:SSD_WIDTH + BC_WIDTH]
    Cm = xbc[..., SSD_WIDTH + BC_WIDTH:]
    return u_f, z, xs, Bm, Cm, dt_raw


def mixer_output(u_f, z, xs, y, d_skip, g_ssd, w_out):
    b, L, _ = xs.shape
    y = y + xs.reshape(b, L, SSD_HEADS, SSD_HEAD_DIM) * d_skip[:, None]
    y = y.reshape(b, L, SSD_WIDTH).astype(xs.dtype)
    y = rms_norm(y * jax.nn.silu(z), g_ssd)
    return jnp.concatenate([fourier_mix(u_f), y], axis=-1) @ w_out


def swiglu(h, w_gate, w_up, w_down):
    return (jax.nn.silu(h @ w_gate) * (h @ w_up)) @ w_down


def setup_inputs(seed: int = 0) -> dict:
    key = jax.random.key(seed)
    ks = jax.random.split(key, 24)
    nrm = lambda k, shape, fan_in: jax.random.normal(k, shape, jnp.float32) * fan_in ** -0.5
    gain = lambda k, n: 1.0 + 0.05 * jax.random.normal(k, (DEPTH, n), jnp.float32)
    dt0 = jnp.exp(jax.random.uniform(ks[10], (DEPTH, 2, SSD_HEADS), jnp.float32)
                  * (math.log(0.1) - math.log(0.001)) + math.log(0.001))
    return {
        'x': jax.random.normal(ks[0], (BATCH, SEQ, D_MODEL), jnp.float32),
        'c': jax.random.normal(ks[1], (BATCH, D_MODEL), jnp.float32),
        'ctx': jax.random.normal(ks[2], (BATCH, CTX_LEN, D_MODEL), jnp.float32),
        'c_ctx': jax.random.normal(ks[3], (D_MODEL,), jnp.float32),
        'w_ada': nrm(ks[4], (DEPTH, D_MODEL, N_MOD * D_MODEL), D_MODEL),
        'b_ada': 0.01 * jax.random.normal(ks[5], (DEPTH, N_MOD * D_MODEL), jnp.float32),
        'g_pre_mix': gain(ks[6], D_MODEL),
        'g_post_mix': gain(ks[7], D_MODEL),
        'g_pre_ffn': gain(ks[8], D_MODEL),
        'g_post_ffn': gain(ks[9], D_MODEL),
        'w_in': nrm(ks[11], (DEPTH, D_MODEL, IN_WIDTH), D_MODEL),
        'conv_w': nrm(ks[12], (DEPTH, CONV_DIM, D_CONV), D_CONV),
        'conv_b': 0.01 * jax.random.normal(ks[13], (DEPTH, CONV_DIM), jnp.float32),
        'dt_bias': dt0 + jnp.log(-jnp.expm1(-dt0)),
        'a_log': jnp.log(jax.random.uniform(ks[14], (DEPTH, 2, SSD_HEADS), jnp.float32, 1.0, 16.0)),
        'd_skip': 1.0 + 0.1 * jax.random.normal(ks[15], (DEPTH, SSD_HEADS), jnp.float32),
        'g_ssd': gain(ks[16], SSD_WIDTH),
        'w_out': nrm(ks[17], (DEPTH, D_MIX, D_MODEL), D_MIX),
        'w_gate': nrm(ks[18], (DEPTH, D_MODEL, D_FF), D_MODEL),
        'w_up': nrm(ks[19], (DEPTH, D_MODEL, D_FF), D_MODEL),
        'w_down': nrm(ks[20], (DEPTH, D_FF, D_MODEL), D_FF),
    }


def reference(x, c, ctx, c_ctx, w_ada, b_ada, g_pre_mix, g_post_mix, g_pre_ffn, g_post_ffn,
              w_in, conv_w, conv_b, dt_bias, a_log, d_skip, g_ssd, w_out, w_gate, w_up, w_down):
    b = x.shape[0]
    rows = x.shape[1] // GRID_W
    ctx_len = ctx.shape[1]
    h0 = jnp.zeros((b, SSD_GROUPS, HEADS_PER_GROUP, SSD_HEAD_DIM, D_STATE), jnp.float32)
    for l in range(DEPTH):
        keep_ctx = l + 1 < DEPTH
        mx = (jax.nn.silu(c) @ w_ada[l] + b_ada[l])[:, None, :]
        sm_x, cm_x, gm_x, sf_x, cf_x, gf_x = jnp.split(mx, N_MOD, axis=-1)
        mc = jax.nn.silu(c_ctx) @ w_ada[l] + b_ada[l]
        sm_c, cm_c, gm_c, sf_c, cf_c, gf_c = jnp.split(mc, N_MOD, axis=-1)

        hc = modulate(rms_norm(ctx, g_pre_mix[l]), sm_c, cm_c)
        uf_c, z_c, xs_c, B_c, C_c, dt_c = mixer_inputs(hc, w_in[l], conv_w[l], conv_b[l], 1, ctx_len, keep_ctx)
        y_c, hf_c, hb_c = ssd_bidir(xs_c, B_c, C_c, dt_c, dt_bias[l], a_log[l], h0, h0, keep_ctx)

        hx = modulate(rms_norm(x, g_pre_mix[l]), sm_x, cm_x)
        uf_x, z_x, xs_x, B_x, C_x, dt_x = mixer_inputs(hx, w_in[l], conv_w[l], conv_b[l], rows, GRID_W, True)
        y_x, _, _ = ssd_bidir(xs_x, B_x, C_x, dt_x, dt_bias[l], a_log[l], hf_c, hb_c, True)
        x = x + gm_x * rms_norm(mixer_output(uf_x, z_x, xs_x, y_x, d_skip[l], g_ssd[l], w_out[l]), g_post_mix[l])

        hx2 = modulate(rms_norm(x, g_pre_ffn[l]), sf_x, cf_x)
        x = x + gf_x * rms_norm(swiglu(hx2, w_gate[l], w_up[l], w_down[l]), g_post_ffn[l])

        if keep_ctx:
            out_c = mixer_output(uf_c, z_c, xs_c, y_c, d_skip[l], g_ssd[l], w_out[l])
            ctx = ctx + gm_c * rms_norm(out_c, g_post_mix[l])
            hc2 = modulate(rms_norm(ctx, g_pre_ffn[l]), sf_c, cf_c)
            ctx = ctx + gf_c * rms_norm(swiglu(hc2, w_gate[l], w_up[l], w_down[l]), g_post_ffn[l])
    return x
```

```python
import functools
import math

import jax
import jax.numpy as jnp
import numpy as np
from jax import lax
from jax.experimental import pallas as pl
from jax.experimental.pallas import tpu as pltpu

F32 = jnp.float32
BF16 = jnp.bfloat16

D_MODEL = 1024
GRID_W = 64
F_WIDTH = 256
N_FGROUPS = 4
FGROUP_DIM = F_WIDTH // N_FGROUPS
SSD_WIDTH = 768
HEAD_DIM = 64
SSD_HEADS = 12
SSD_GROUPS = 4
HEADS_PER_GROUP = 3
D_STATE = 128
BC_WIDTH = SSD_GROUPS * D_STATE
D_CONV = 7
HALF_CONV = D_CONV // 2
CHUNK = 128
XBC_OFF = F_WIDTH + SSD_WIDTH
CONV_DIM = SSD_WIDTH + 2 * BC_WIDTH
N_MOD = 6
EPS = 1e-6

GROUP_PAD = 256
SSD_PAD = SSD_GROUPS * GROUP_PAD
XBC_PAD = SSD_PAD + 2 * BC_WIDTH
W_COLS = F_WIDTH + SSD_PAD + XBC_PAD
CTX_COL0 = F_WIDTH + SSD_PAD
DT_ROWS = 32
DT_DIR_ROWS = 16
MOD_ROWS = 16
MOD_CTX_ROW = 8

SUBLANE = 8
LANE = 128
VMEM_LIMIT = 56 * 1024 * 1024


def _dot(a, b):
    return jnp.dot(a, b, preferred_element_type=F32)


def _silu(v):
    return v * jax.nn.sigmoid(v)


def _rms(v, g, n=None):
    n = v.shape[-1] if n is None else n
    ms = jnp.sum(v * v, axis=-1, keepdims=True) * (1.0 / n)
    return v * lax.rsqrt(ms + EPS) * g


def _ada_kernel(c_ref, w_ref, b_ref, o_ref):
    s = _silu(c_ref[...]).astype(BF16)
    o_ref[...] = _dot(s, w_ref[...].astype(BF16)) + b_ref[...]


def _ada_call(cc, w_ada, b_ada):
    n = w_ada.shape[1]
    tn = 768
    return pl.pallas_call(
        _ada_kernel,
        grid=(n // tn,),
        in_specs=[
            pl.BlockSpec((MOD_ROWS, D_MODEL), lambda j: (0, 0)),
            pl.BlockSpec((D_MODEL, tn), lambda j: (0, j)),
            pl.BlockSpec((1, tn), lambda j: (0, j)),
        ],
        out_specs=pl.BlockSpec((MOD_ROWS, tn), lambda j: (0, j)),
        out_shape=jax.ShapeDtypeStruct((MOD_ROWS, n), F32),
        compiler_params=pltpu.CompilerParams(dimension_semantics=("arbitrary",)),
        name="ada_mod",
    )(cc, w_ada, b_ada)


def _inproj_kernel(*refs, ctx, tm, row_len):
    if ctx:
        (x_ref, mod_ref, g_ref, w_ref, wdt_ref, cw_ref, cb_ref,
         xs_ref, bm_ref, cm_ref, dt_ref, pad_ref) = refs
    else:
        (x_ref, mod_ref, g_ref, w_ref, wdt_ref, cw_ref, cb_ref, wc_ref,
         uu_ref, z_ref, xs_ref, bm_ref, cm_ref, dt_ref, pad_ref) = refs

    xt = x_ref[0]
    if ctx:
        shift = mod_ref[MOD_CTX_ROW:MOD_CTX_ROW + 1, 0:D_MODEL]
        scale = mod_ref[MOD_CTX_ROW:MOD_CTX_ROW + 1, D_MODEL:2 * D_MODEL]
    else:
        b = pl.program_id(0)
        shift = mod_ref[pl.ds(b, 1), 0:D_MODEL]
        scale = mod_ref[pl.ds(b, 1), D_MODEL:2 * D_MODEL]
    h = _rms(xt, g_ref[...]) * (1.0 + scale) + shift
    hb = h.astype(BF16)

    if not ctx:
        uf = _dot(hb, w_ref[:, 0:F_WIDTH])
        uu_ref[0] = _dot(uf.astype(BF16), wc_ref[...].astype(BF16)).astype(BF16)
        z_ref[0] = _dot(hb, w_ref[:, F_WIDTH:CTX_COL0]).astype(BF16)
        xbc = _dot(hb, w_ref[:, CTX_COL0:W_COLS])
    else:
        xbc = _dot(hb, w_ref[:, CTX_COL0:W_COLS])
    dt_ref[0] = lax.dot_general(wdt_ref[...], hb, (((1,), (1,)), ((), ())),
                                preferred_element_type=F32)

    stride = row_len + SUBLANE
    nrow = tm // row_len
    zpad = jnp.zeros((SUBLANE, XBC_PAD), F32)
    pad_ref[0:SUBLANE, :] = zpad
    for r in range(nrow):
        base = SUBLANE + r * stride
        pad_ref[base:base + row_len, :] = xbc[r * row_len:(r + 1) * row_len, :]
        pad_ref[base + row_len:base + stride, :] = zpad

    piece = 64
    cwid = 256
    for r in range(nrow):
        for pi in range(row_len // piece):
            src = SUBLANE + r * stride + pi * piece
            dst = r * row_len + pi * piece
            for c0 in range(0, XBC_PAD, cwid):
                acc = jnp.broadcast_to(cb_ref[:, c0:c0 + cwid], (piece, cwid))
                for k in range(D_CONV):
                    win = pad_ref[src + k - HALF_CONV:src + k - HALF_CONV + piece, c0:c0 + cwid]
                    acc = acc + win * cw_ref[k:k + 1, c0:c0 + cwid]
                val = _silu(acc).astype(BF16)
                if c0 < SSD_PAD:
                    xs_ref[0, dst:dst + piece, c0:c0 + cwid] = val
                elif c0 < SSD_PAD + BC_WIDTH:
                    cc = c0 - SSD_PAD
                    bm_ref[0, dst:dst + piece, cc:cc + cwid] = val
                else:
                    cc = c0 - SSD_PAD - BC_WIDTH
                    cm_ref[0, dst:dst + piece, cc:cc + cwid] = val


def _inproj_call(xin, mod, g, w, wdt, cw, cb, wc, *, ctx, tm, row_len):
    bsz, seq, _ = xin.shape
    nt = seq // tm
    const2 = lambda b, j: (0, 0)
    tile3 = lambda b, j: (b, j, 0)
    in_specs = [
        pl.BlockSpec((1, tm, D_MODEL), tile3),
        pl.BlockSpec((MOD_ROWS, N_MOD * D_MODEL), const2),
        pl.BlockSpec((1, D_MODEL), const2),
        pl.BlockSpec((D_MODEL, W_COLS), const2),
        pl.BlockSpec((DT_ROWS, D_MODEL), const2),
        pl.BlockSpec((D_CONV, XBC_PAD), const2),
        pl.BlockSpec((1, XBC_PAD), const2),
    ]
    args = [xin, mod, g, w, wdt, cw, cb]
    out_specs = []
    out_shape = []
    if not ctx:
        in_specs.append(pl.BlockSpec((F_WIDTH, 2 * F_WIDTH), const2))
        args.append(wc)
        out_specs += [pl.BlockSpec((1, tm, 2 * F_WIDTH), tile3),
                      pl.BlockSpec((1, tm, SSD_PAD), tile3)]
        out_shape += [jax.ShapeDtypeStruct((bsz, seq, 2 * F_WIDTH), BF16),
                      jax.ShapeDtypeStruct((bsz, seq, SSD_PAD), BF16)]
    out_specs += [pl.BlockSpec((1, tm, SSD_PAD), tile3),
                  pl.BlockSpec((1, tm, BC_WIDTH), tile3),
                  pl.BlockSpec((1, tm, BC_WIDTH), tile3),
                  pl.BlockSpec((1, DT_ROWS, tm), lambda b, j: (b, 0, j))]
    out_shape += [jax.ShapeDtypeStruct((bsz, seq, SSD_PAD), BF16),
                  jax.ShapeDtypeStruct((bsz, seq, BC_WIDTH), BF16),
                  jax.ShapeDtypeStruct((bsz, seq, BC_WIDTH), BF16),
                  jax.ShapeDtypeStruct((bsz, DT_ROWS, seq), F32)]
    pad_rows = SUBLANE + (tm // row_len) * (row_len + SUBLANE)
    return pl.pallas_call(
        functools.partial(_inproj_kernel, ctx=ctx, tm=tm, row_len=row_len),
        grid=(bsz, nt),
        in_specs=in_specs,
        out_specs=out_specs,
        out_shape=out_shape,
        scratch_shapes=[pltpu.VMEM((pad_rows, XBC_PAD), F32)],
        compiler_params=pltpu.CompilerParams(
            dimension_semantics=("arbitrary", "arbitrary"),
            vmem_limit_bytes=VMEM_LIMIT),
        name="inproj_ctx" if ctx else "inproj_lat",
    )(*args)


def _ssd_kernel(xs_ref, bm_ref, cm_ref, z_ref, dt_ref,
                xsc_ref, bmc_ref, cmc_ref, dtc_ref,
                dtb_ref, alog_ref, dskip_ref, gssd_ref,
                out_ref, h_ref, y_ref, *, seq, ctx_len):
    nchunk = seq // CHUNK
    nchunk_ctx = ctx_len // CHUNK

    h_ref[...] = jnp.zeros(h_ref.shape, F32)
    y_ref[...] = jnp.zeros(y_ref.shape, F32)

    bias = dtb_ref[...]
    nega = -jnp.exp(alog_ref[...])
    lane16 = lax.broadcasted_iota(jnp.int32, (DT_DIR_ROWS, CHUNK), 1)
    sub_i = lax.broadcasted_iota(jnp.int32, (CHUNK, CHUNK), 0)
    lane_i = lax.broadcasted_iota(jnp.int32, (CHUNK, CHUNK), 1)
    tri = (lane_i <= sub_i, lane_i >= sub_i)
    lane_lo = lane_i < HEAD_DIM
    lane256 = lax.broadcasted_iota(jnp.int32, (1, GROUP_PAD), 1)

    def colb(mat_t, idx):
        return jnp.broadcast_to(mat_t[:, idx:idx + 1], (CHUNK, CHUNK))

    def direction(d, x, bc, cc, raw, row0, with_output):
        r0 = d * DT_DIR_ROWS
        v = raw + bias[r0:r0 + DT_DIR_ROWS]
        dt = jnp.maximum(v, 0.0) + jnp.log1p(jnp.exp(-jnp.abs(v)))
        da = dt * nega[r0:r0 + DT_DIR_ROWS]
        cs = da
        sh = 1
        while sh < CHUNK:
            if d == 0:
                cs = cs + jnp.where(lane16 >= sh, pltpu.roll(cs, sh, axis=1), 0.0)
            else:
                cs = cs + jnp.where(lane16 < CHUNK - sh, pltpu.roll(cs, CHUNK - sh, axis=1), 0.0)
            sh *= 2
        a_end = jnp.sum(da, axis=1, keepdims=True)
        wdt = jnp.exp(a_end - cs) * dt
        e_end = jnp.exp(a_end)
        q_mat = jnp.concatenate(
            [cs, wdt, jnp.zeros((CHUNK - 2 * DT_DIR_ROWS, CHUNK), F32)], axis=0)
        q_t = q_mat.T

        for g in range(SSD_GROUPS):
            bg = bc[:, g * D_STATE:(g + 1) * D_STATE]
            cg = cc[:, g * D_STATE:(g + 1) * D_STATE]
            bt = bg.astype(F32).T.astype(BF16)
            hg = h_ref[d, g]
            xg = x[:, g * GROUP_PAD:(g + 1) * GROUP_PAD]
            heads = [g * HEADS_PER_GROUP + r for r in range(HEADS_PER_GROUP)]
            cs_cols = [colb(q_t, hd) for hd in heads]
            if with_output:
                scores = _dot(cg, bt)
                y_off = _dot(cg, hg.astype(BF16))
                ms = []
                for r, hd in enumerate(heads):
                    seg = cs_cols[r] - cs[hd:hd + 1, :]
                    dec = jnp.exp(jnp.where(tri[d], seg, -jnp.inf))
                    ms.append((scores * dec * dt[hd:hd + 1, :]).astype(BF16))
                t0 = xg[:, 0:CHUNK]
                t1 = xg[:, CHUNK:2 * CHUNK]
                y0 = jnp.where(lane_lo, _dot(ms[0], t0), _dot(ms[1], t0))
                y1 = _dot(ms[2], t1)
                e0 = jnp.where(lane_lo, jnp.exp(cs_cols[0]), jnp.exp(cs_cols[1]))
                e1 = jnp.exp(cs_cols[2])
                y_g = (jnp.concatenate([y0, y1], axis=1)
                       + y_off * jnp.concatenate([e0, e1], axis=1))
                cur = y_ref[pl.ds(row0, CHUNK), g * GROUP_PAD:(g + 1) * GROUP_PAD]
                y_ref[pl.ds(row0, CHUNK), g * GROUP_PAD:(g + 1) * GROUP_PAD] = cur + y_g
            w_cols = [colb(q_t, DT_DIR_ROWS + hd) for hd in heads]
            w0 = jnp.where(lane_lo, w_cols[0], w_cols[1])
            wcat = jnp.concatenate([w0, w_cols[2]], axis=1)
            xsc = (xg.astype(F32) * wcat).astype(BF16)
            ee = [e_end[hd:hd + 1, :] for hd in heads]
            e3 = jnp.where(lane256 < HEAD_DIM, ee[0],
                           jnp.where(lane256 < 2 * HEAD_DIM, ee[1], ee[2]))
            h_ref[d, g] = e3 * hg + _dot(bt, xsc)

    for i in range(nchunk_ctx):
        for d in range(2):
            ci = i if d == 0 else nchunk_ctx - 1 - i
            rows = slice(ci * CHUNK, (ci + 1) * CHUNK)
            raw = dtc_ref[0, d * DT_DIR_ROWS:(d + 1) * DT_DIR_ROWS, rows]
            direction(d, xsc_ref[0, rows, :], bmc_ref[0, rows, :], cmc_ref[0, rows, :],
                      raw, 0, False)

    def step(i, carry):
        for d in range(2):
            ci = i if d == 0 else nchunk - 1 - i
            row0 = pl.multiple_of(ci * CHUNK, CHUNK)
            raw = dt_ref[0, d * DT_DIR_ROWS:(d + 1) * DT_DIR_ROWS, pl.ds(row0, CHUNK)]
            direction(d, xs_ref[0, pl.ds(row0, CHUNK), :], bm_ref[0, pl.ds(row0, CHUNK), :],
                      cm_ref[0, pl.ds(row0, CHUNK), :], raw, row0, True)
        return carry

    lax.fori_loop(0, nchunk, step, 0)

    te = 256

    def epi(i, carry):
        row0 = pl.multiple_of(i * te, te)
        xr = xs_ref[0, pl.ds(row0, te), :].astype(F32)
        zr = z_ref[0, pl.ds(row0, te), :].astype(F32)
        y = y_ref[pl.ds(row0, te), :] + xr * dskip_ref[...]
        y = y * _silu(zr)
        out_ref[0, pl.ds(row0, te), :] = _rms(y, gssd_ref[...], n=SSD_WIDTH).astype(BF16)
        return carry

    lax.fori_loop(0, seq // te, epi, 0)


def _ssd_call(xs, bm, cm, z, dt, xsc, bmc, cmc, dtc, dtb, alog, dskip, gssd):
    bsz, seq, _ = xs.shape
    ctx_len = xsc.shape[1]
    per_b = lambda b: (b, 0, 0)
    const2 = lambda b: (0, 0)
    return pl.pallas_call(
        functools.partial(_ssd_kernel, seq=seq, ctx_len=ctx_len),
        grid=(bsz,),
        in_specs=[
            pl.BlockSpec((1, seq, SSD_PAD), per_b),
            pl.BlockSpec((1, seq, BC_WIDTH), per_b),
            pl.BlockSpec((1, seq, BC_WIDTH), per_b),
            pl.BlockSpec((1, seq, SSD_PAD), per_b),
            pl.BlockSpec((1, DT_ROWS, seq), per_b),
            pl.BlockSpec((1, ctx_len, SSD_PAD), per_b),
            pl.BlockSpec((1, ctx_len, BC_WIDTH), per_b),
            pl.BlockSpec((1, ctx_len, BC_WIDTH), per_b),
            pl.BlockSpec((1, DT_ROWS, ctx_len), per_b),
            pl.BlockSpec((DT_ROWS, 1), const2),
            pl.BlockSpec((DT_ROWS, 1), const2),
            pl.BlockSpec((1, SSD_PAD), const2),
            pl.BlockSpec((1, SSD_PAD), const2),
        ],
        out_specs=pl.BlockSpec((1, seq, SSD_PAD), per_b),
        out_shape=jax.ShapeDtypeStruct((bsz, seq, SSD_PAD), BF16),
        scratch_shapes=[
            pltpu.VMEM((2, SSD_GROUPS, D_STATE, GROUP_PAD), F32),
            pltpu.VMEM((seq, SSD_PAD), F32),
        ],
        compiler_params=pltpu.CompilerParams(
            dimension_semantics=("arbitrary",), vmem_limit_bytes=VMEM_LIMIT),
        name="ssd_scan",
    )(xs, bm, cm, z, dt, xsc, bmc, cmc, dtc, dtb, alog, dskip, gssd)


def _out_ffn_kernel(x_ref, yn_ref, uu_ref, cl_ref, sl_ref, mod_ref,
                    wof_ref, woy_ref, gpm_ref, gpf_ref, gpo_ref,
                    wg_ref, wu_ref, wd_ref, out_ref):
    b = pl.program_id(1)
    yf = (_dot(cl_ref[...].astype(BF16), uu_ref[0, :, 0:F_WIDTH])
          + _dot(sl_ref[...].astype(BF16), uu_ref[0, :, F_WIDTH:2 * F_WIDTH]))
    mix = _dot(yf.astype(BF16), wof_ref[...]) + _dot(yn_ref[0], woy_ref[...])

    def mod(k):
        return mod_ref[pl.ds(b, 1), k * D_MODEL:(k + 1) * D_MODEL]

    x1 = x_ref[0] + mod(2) * _rms(mix, gpm_ref[...])
    h2 = (_rms(x1, gpf_ref[...]) * (1.0 + mod(4)) + mod(3)).astype(BF16)
    gate = _dot(h2, wg_ref[...])
    up = _dot(h2, wu_ref[...])
    act = (_silu(gate) * up).astype(BF16)
    ffn = _dot(act, wd_ref[...])
    out_ref[0] = x1 + mod(5) * _rms(ffn, gpo_ref[...])


def _out_ffn_call(x, yn, uu, cl, sl, mod, wof, woy, gpm, gpf, gpo, wg, wu, wd, *, tm):
    bsz, seq, _ = x.shape
    d_ff = wg.shape[1]
    nt = seq // tm
    const2 = lambda j, b: (0, 0)
    tile3 = lambda j, b: (b, j, 0)
    single = dict(pipeline_mode=pl.Buffered(1))
    return pl.pallas_call(
        _out_ffn_kernel,
        grid=(nt, bsz),
        in_specs=[
            pl.BlockSpec((1, tm, D_MODEL), tile3),
            pl.BlockSpec((1, tm, SSD_PAD), tile3),
            pl.BlockSpec((1, seq, 2 * F_WIDTH), lambda j, b: (b, 0, 0)),
            pl.BlockSpec((tm, seq), lambda j, b: (j, 0)),
            pl.BlockSpec((tm, seq), lambda j, b: (j, 0)),
            pl.BlockSpec((MOD_ROWS, N_MOD * D_MODEL), const2),
            pl.BlockSpec((F_WIDTH, D_MODEL), const2, **single),
            pl.BlockSpec((SSD_PAD, D_MODEL), const2, **single),
            pl.BlockSpec((1, D_MODEL), const2),
            pl.BlockSpec((1, D_MODEL), const2),
            pl.BlockSpec((1, D_MODEL), const2),
            pl.BlockSpec((D_MODEL, d_ff), const2, **single),
            pl.BlockSpec((D_MODEL, d_ff), const2, **single),
            pl.BlockSpec((d_ff, D_MODEL), const2, **single),
        ],
        out_specs=pl.BlockSpec((1, tm, D_MODEL), tile3),
        out_shape=jax.ShapeDtypeStruct((bsz, seq, D_MODEL), F32),
        compiler_params=pltpu.CompilerParams(
            dimension_semantics=("arbitrary", "arbitrary"),
            vmem_limit_bytes=VMEM_LIMIT),
        name="out_ffn",
    )(x, yn, uu, cl, sl, mod, wof, woy, gpm, gpf, gpo, wg, wu, wd)


@functools.lru_cache(maxsize=None)
def _dft_tables(seq):
    k = np.arange(seq, dtype=np.int64)
    ang = 2.0 * np.pi * ((k[:, None] * k[None, :]) % seq).astype(np.float64) / seq
    scale = 1.0 / math.sqrt(seq)
    cl = (np.cos(ang) * scale).astype(np.float32)
    sl = (np.sin(ang) * scale).astype(np.float32)
    c = np.arange(FGROUP_DIM, dtype=np.int64)
    angc = 2.0 * np.pi * ((c[:, None] * c[None, :]) % FGROUP_DIM).astype(np.float64) / FGROUP_DIM
    sc = 1.0 / math.sqrt(FGROUP_DIM)
    wc = np.zeros((F_WIDTH, 2 * F_WIDTH), np.float32)
    for g in range(N_FGROUPS):
        s = slice(g * FGROUP_DIM, (g + 1) * FGROUP_DIM)
        wc[s, g * FGROUP_DIM:(g + 1) * FGROUP_DIM] = np.cos(angc) * sc
        wc[s, F_WIDTH + g * FGROUP_DIM:F_WIDTH + (g + 1) * FGROUP_DIM] = -np.sin(angc) * sc
    return cl, sl, wc


def _pad_groups(a, axis):
    a = jnp.moveaxis(a, axis, -1)
    lead = a.shape[:-1]
    a = a.reshape(lead + (SSD_GROUPS, HEADS_PER_GROUP * HEAD_DIM))
    a = jnp.pad(a, [(0, 0)] * len(lead) + [(0, 0), (0, GROUP_PAD - HEADS_PER_GROUP * HEAD_DIM)])
    a = a.reshape(lead + (SSD_PAD,))
    return jnp.moveaxis(a, -1, axis)


def _pad_dirs(a):
    a = jnp.pad(a, ((0, 0), (0, DT_DIR_ROWS - SSD_HEADS)))
    return a.reshape(DT_ROWS, 1)


def kernel(x, c, ctx, c_ctx, w_ada, b_ada, g_pre_mix, g_post_mix, g_pre_ffn, g_post_ffn,
           w_in, conv_w, conv_b, dt_bias, a_log, d_skip, g_ssd, w_out, w_gate, w_up, w_down):
    bsz, seq, _ = x.shape
    ctx_len = ctx.shape[1]
    l = 0
    cl_np, sl_np, wc_np = _dft_tables(seq)
    cl = jnp.asarray(cl_np)
    sl = jnp.asarray(sl_np)
    wc = jnp.asarray(wc_np)

    cc = jnp.concatenate(
        [c, c_ctx[None, :], jnp.zeros((MOD_ROWS - bsz - 1, D_MODEL), F32)], axis=0)
    mod = _ada_call(cc, w_ada[l], b_ada[l][None, :])

    wi = w_in[l]
    w_cat = jnp.concatenate([
        wi[:, :F_WIDTH],
        _pad_groups(wi[:, F_WIDTH:XBC_OFF], 1),
        _pad_groups(wi[:, XBC_OFF:XBC_OFF + SSD_WIDTH], 1),
        wi[:, XBC_OFF + SSD_WIDTH:XBC_OFF + CONV_DIM],
    ], axis=1).astype(BF16)
    w_dt = wi[:, XBC_OFF + CONV_DIM:].T.reshape(2, SSD_HEADS, D_MODEL)
    w_dt = jnp.pad(w_dt, ((0, 0), (0, DT_DIR_ROWS - SSD_HEADS), (0, 0))).reshape(DT_ROWS, D_MODEL)
    w_dt = w_dt.astype(BF16)
    cw = conv_w[l].T
    cw = jnp.concatenate([_pad_groups(cw[:, :SSD_WIDTH], 1), cw[:, SSD_WIDTH:]], axis=1)
    cb = conv_b[l][None, :]
    cb = jnp.concatenate([_pad_groups(cb[:, :SSD_WIDTH], 1), cb[:, SSD_WIDTH:]], axis=1)
    g_pre = g_pre_mix[l][None, :]

    xs_c, bm_c, cm_c, dt_c = _inproj_call(
        ctx, mod, g_pre, w_cat, w_dt, cw, cb, None, ctx=True, tm=ctx_len, row_len=ctx_len)
    uu, z, xs, bm, cm, dt = _inproj_call(
        x, mod, g_pre, w_cat, w_dt, cw, cb, wc, ctx=False, tm=512, row_len=GRID_W)

    dskip = _pad_groups(jnp.repeat(d_skip[l], HEAD_DIM)[None, :], 1)
    gssd = _pad_groups(g_ssd[l][None, :], 1)
    yn = _ssd_call(xs, bm, cm, z, dt, xs_c, bm_c, cm_c, dt_c,
                   _pad_dirs(dt_bias[l]), _pad_dirs(a_log[l]), dskip, gssd)

    wo = w_out[l]
    wof = wo[:F_WIDTH].astype(BF16)
    woy = _pad_groups(wo[F_WIDTH:], 0).astype(BF16)
    return _out_ffn_call(
        x, yn, uu, cl, sl, mod, wof, woy,
        g_post_mix[l][None, :], g_pre_ffn[l][None, :], g_post_ffn[l][None, :],
        w_gate[l].astype(BF16), w_up[l].astype(BF16), w_down[l].astype(BF16), tm=256)
```

```python
import functools
import math

import jax
import jax.numpy as jnp
import numpy as np
from jax import lax
from jax.experimental import pallas as pl
from jax.experimental.pallas import tpu as pltpu

F32 = jnp.float32
BF16 = jnp.bfloat16

D_MODEL = 1024
GRID_W = 64
F_WIDTH = 256
N_FGROUPS = 4
FGROUP_DIM = F_WIDTH // N_FGROUPS
SSD_WIDTH = 768
HEAD_DIM = 64
SSD_HEADS = 12
SSD_GROUPS = 4
HEADS_PER_GROUP = 3
D_STATE = 128
BC_WIDTH = SSD_GROUPS * D_STATE
D_CONV = 7
HALF_CONV = D_CONV // 2
CHUNK = 128
XBC_OFF = F_WIDTH + SSD_WIDTH
CONV_DIM = SSD_WIDTH + 2 * BC_WIDTH
N_MOD = 6
EPS = 1e-6

GROUP_PAD = 256
SSD_PAD = SSD_GROUPS * GROUP_PAD
XBC_PAD = SSD_PAD + 2 * BC_WIDTH
W_COLS = F_WIDTH + SSD_PAD + XBC_PAD
CTX_COL0 = F_WIDTH + SSD_PAD
DT_ROWS = 32
DT_DIR_ROWS = 16
MOD_ROWS = 16
MOD_CTX_ROW = 8

SUBLANE = 8
LANE = 128
VMEM_LIMIT = 56 * 1024 * 1024


def _dot(a, b):
    return jnp.dot(a, b, preferred_element_type=F32)


def _silu(v):
    return v * jax.nn.sigmoid(v)


def _rms(v, g, n=None):
    n = v.shape[-1] if n is None else n
    ms = jnp.sum(v * v, axis=-1, keepdims=True) * (1.0 / n)
    return v * lax.rsqrt(ms + EPS) * g


def _ada_kernel(c_ref, w_ref, b_ref, o_ref):
    s = _silu(c_ref[...]).astype(BF16)
    o_ref[...] = _dot(s, w_ref[...].astype(BF16)) + b_ref[...]


def _ada_call(cc, w_ada, b_ada):
    n = w_ada.shape[1]
    tn = 768
    return pl.pallas_call(
        _ada_kernel,
        grid=(n // tn,),
        in_specs=[
            pl.BlockSpec((MOD_ROWS, D_MODEL), lambda j: (0, 0)),
            pl.BlockSpec((D_MODEL, tn), lambda j: (0, j)),
            pl.BlockSpec((1, tn), lambda j: (0, j)),
        ],
        out_specs=pl.BlockSpec((MOD_ROWS, tn), lambda j: (0, j)),
        out_shape=jax.ShapeDtypeStruct((MOD_ROWS, n), F32),
        compiler_params=pltpu.CompilerParams(dimension_semantics=("arbitrary",)),
        name="ada_mod",
    )(cc, w_ada, b_ada)


CONV_STRIDE = SUBLANE + 1
CONV_PIECE = SUBLANE * CONV_STRIDE


def _conv_pieces(tm, row_len):
    padded = (tm // row_len) * (row_len + SUBLANE)
    return -(-padded // CONV_PIECE)


def _inproj_kernel(*refs, ctx, tm, row_len):
    if ctx:
        (x_ref, mod_ref, g_ref, w_ref, wdt_ref, cw_ref, cb_ref,
         xs_ref, bm_ref, cm_ref, dt_ref, pad_ref, cv_ref) = refs
    else:
        (x_ref, mod_ref, g_ref, w_ref, wdt_ref, cw_ref, cb_ref, wc_ref,
         uu_ref, z_ref, xs_ref, bm_ref, cm_ref, dt_ref, pad_ref, cv_ref) = refs

    xt = x_ref[0]
    if ctx:
        shift = mod_ref[MOD_CTX_ROW:MOD_CTX_ROW + 1, 0:D_MODEL]
        scale = mod_ref[MOD_CTX_ROW:MOD_CTX_ROW + 1, D_MODEL:2 * D_MODEL]
    else:
        b = pl.program_id(0)
        shift = mod_ref[pl.ds(b, 1), 0:D_MODEL]
        scale = mod_ref[pl.ds(b, 1), D_MODEL:2 * D_MODEL]
    h = _rms(xt, g_ref[...]) * (1.0 + scale) + shift
    hb = h.astype(BF16)

    if not ctx:
        uf = _dot(hb, w_ref[:, 0:F_WIDTH])
        uu_ref[0] = _dot(uf.astype(BF16), wc_ref[...].astype(BF16)).astype(BF16)
        z_ref[0] = _dot(hb, w_ref[:, F_WIDTH:CTX_COL0]).astype(BF16)
        xbc = _dot(hb, w_ref[:, CTX_COL0:W_COLS])
    else:
        xbc = _dot(hb, w_ref[:, CTX_COL0:W_COLS])
    dt_ref[0] = lax.dot_general(wdt_ref[...], hb, (((1,), (1,)), ((), ())),
                                preferred_element_type=F32)

    pitch = row_len + SUBLANE
    nrow = tm // row_len
    npiece = _conv_pieces(tm, row_len)
    data_end = SUBLANE + nrow * pitch
    alloc_end = pad_ref.shape[1]
    zrow = jnp.zeros((SUBLANE, LANE), F32)
    for j in range(XBC_PAD // LANE):
        pad_ref[j, 0:SUBLANE, :] = zrow
        for r in range(nrow):
            base = SUBLANE + r * pitch
            pad_ref[j, base:base + row_len, :] = xbc[r * row_len:(r + 1) * row_len,
                                                     j * LANE:(j + 1) * LANE]
            pad_ref[j, base + row_len:base + pitch, :] = zrow
        for z0 in range(data_end, alloc_end, SUBLANE):
            pad_ref[j, z0:z0 + SUBLANE, :] = zrow

    for j in range(XBC_PAD // LANE):
        wk = [jnp.broadcast_to(cw_ref[k:k + 1, j * LANE:(j + 1) * LANE], (SUBLANE, LANE))
              for k in range(D_CONV)]
        bias = jnp.broadcast_to(cb_ref[:, j * LANE:(j + 1) * LANE], (SUBLANE, LANE))
        for p in range(npiece):
            base = SUBLANE + p * CONV_PIECE
            wins = {v: pad_ref[j, pl.ds(base + v, SUBLANE, stride=CONV_STRIDE), :]
                    for v in range(-HALF_CONV, CONV_STRIDE + HALF_CONV)}
            for a in range(CONV_STRIDE):
                acc = bias
                for k in range(D_CONV):
                    acc = acc + wins[a + k - HALF_CONV] * wk[k]
                cv_ref[j, pl.ds(base + a, SUBLANE, stride=CONV_STRIDE), :] = acc

    for r in range(nrow):
        base = SUBLANE + r * pitch
        for j in range(XBC_PAD // LANE):
            val = _silu(cv_ref[j, base:base + row_len, :]).astype(BF16)
            c0 = j * LANE
            rows = slice(r * row_len, (r + 1) * row_len)
            if c0 < SSD_PAD:
                xs_ref[0, rows, c0:c0 + LANE] = val
            elif c0 < SSD_PAD + BC_WIDTH:
                cc = c0 - SSD_PAD
                bm_ref[0, rows, cc:cc + LANE] = val
            else:
                cc = c0 - SSD_PAD - BC_WIDTH
                cm_ref[0, rows, cc:cc + LANE] = val


def _inproj_call(xin, mod, g, w, wdt, cw, cb, wc, *, ctx, tm, row_len):
    bsz, seq, _ = xin.shape
    nt = seq // tm
    const2 = lambda b, j: (0, 0)
    tile3 = lambda b, j: (b, j, 0)
    in_specs = [
        pl.BlockSpec((1, tm, D_MODEL), tile3),
        pl.BlockSpec((MOD_ROWS, N_MOD * D_MODEL), const2),
        pl.BlockSpec((1, D_MODEL), const2),
        pl.BlockSpec((D_MODEL, W_COLS), const2),
        pl.BlockSpec((DT_ROWS, D_MODEL), const2),
        pl.BlockSpec((D_CONV, XBC_PAD), const2),
        pl.BlockSpec((1, XBC_PAD), const2),
    ]
    args = [xin, mod, g, w, wdt, cw, cb]
    out_specs = []
    out_shape = []
    if not ctx:
        in_specs.append(pl.BlockSpec((F_WIDTH, 2 * F_WIDTH), const2))
        args.append(wc)
        out_specs += [pl.BlockSpec((1, tm, 2 * F_WIDTH), tile3),
                      pl.BlockSpec((1, tm, SSD_PAD), tile3)]
        out_shape += [jax.ShapeDtypeStruct((bsz, seq, 2 * F_WIDTH), BF16),
                      jax.ShapeDtypeStruct((bsz, seq, SSD_PAD), BF16)]
    out_specs += [pl.BlockSpec((1, tm, SSD_PAD), tile3),
                  pl.BlockSpec((1, tm, BC_WIDTH), tile3),
                  pl.BlockSpec((1, tm, BC_WIDTH), tile3),
                  pl.BlockSpec((1, DT_ROWS, tm), lambda b, j: (b, 0, j))]
    out_shape += [jax.ShapeDtypeStruct((bsz, seq, SSD_PAD), BF16),
                  jax.ShapeDtypeStruct((bsz, seq, BC_WIDTH), BF16),
                  jax.ShapeDtypeStruct((bsz, seq, BC_WIDTH), BF16),
                  jax.ShapeDtypeStruct((bsz, DT_ROWS, seq), F32)]
    pad_rows = 2 * SUBLANE + _conv_pieces(tm, row_len) * CONV_PIECE
    conv_scratch = pltpu.VMEM((XBC_PAD // LANE, pad_rows, LANE), F32)
    return pl.pallas_call(
        functools.partial(_inproj_kernel, ctx=ctx, tm=tm, row_len=row_len),
        grid=(bsz, nt),
        in_specs=in_specs,
        out_specs=out_specs,
        out_shape=out_shape,
        scratch_shapes=[conv_scratch, conv_scratch],
        compiler_params=pltpu.CompilerParams(
            dimension_semantics=("arbitrary", "arbitrary"),
            vmem_limit_bytes=VMEM_LIMIT),
        name="inproj_ctx" if ctx else "inproj_lat",
    )(*args)


def _ssd_kernel(xs_ref, bm_ref, cm_ref, z_ref, dt_ref,
                xsc_ref, bmc_ref, cmc_ref, dtc_ref,
                dtb_ref, alog_ref, dskip_ref, gssd_ref,
                out_ref, h_ref, y_ref, *, seq, ctx_len):
    nchunk = seq // CHUNK
    nchunk_ctx = ctx_len // CHUNK

    h_ref[...] = jnp.zeros(h_ref.shape, F32)
    y_ref[...] = jnp.zeros(y_ref.shape, F32)

    bias = dtb_ref[...]
    nega = -jnp.exp(alog_ref[...])
    lane16 = lax.broadcasted_iota(jnp.int32, (DT_DIR_ROWS, CHUNK), 1)
    sub_i = lax.broadcasted_iota(jnp.int32, (CHUNK, CHUNK), 0)
    lane_i = lax.broadcasted_iota(jnp.int32, (CHUNK, CHUNK), 1)
    tri = (lane_i <= sub_i, lane_i >= sub_i)
    lane_lo = lane_i < HEAD_DIM
    lane256 = lax.broadcasted_iota(jnp.int32, (1, GROUP_PAD), 1)

    def colb(mat_t, idx):
        return jnp.broadcast_to(mat_t[:, idx:idx + 1], (CHUNK, CHUNK))

    def direction(d, x, bc, cc, raw, row0, with_output):
        r0 = d * DT_DIR_ROWS
        v = raw + bias[r0:r0 + DT_DIR_ROWS]
        dt = jnp.maximum(v, 0.0) + jnp.log1p(jnp.exp(-jnp.abs(v)))
        da = dt * nega[r0:r0 + DT_DIR_ROWS]
        cs = da
        sh = 1
        while sh < CHUNK:
            if d == 0:
                cs = cs + jnp.where(lane16 >= sh, pltpu.roll(cs, sh, axis=1), 0.0)
            else:
                cs = cs + jnp.where(lane16 < CHUNK - sh, pltpu.roll(cs, CHUNK - sh, axis=1), 0.0)
            sh *= 2
        a_end = jnp.sum(da, axis=1, keepdims=True)
        wdt = jnp.exp(a_end - cs) * dt
        e_end = jnp.exp(a_end)
        q_mat = jnp.concatenate(
            [cs, wdt, jnp.zeros((CHUNK - 2 * DT_DIR_ROWS, CHUNK), F32)], axis=0)
        q_t = q_mat.T

        for g in range(SSD_GROUPS):
            bg = bc[:, g * D_STATE:(g + 1) * D_STATE]
            cg = cc[:, g * D_STATE:(g + 1) * D_STATE]
            bt = bg.astype(F32).T.astype(BF16)
            hg = h_ref[d, g]
            xg = x[:, g * GROUP_PAD:(g + 1) * GROUP_PAD]
            heads = [g * HEADS_PER_GROUP + r for r in range(HEADS_PER_GROUP)]
            cs_cols = [colb(q_t, hd) for hd in heads]
            if with_output:
                scores = _dot(cg, bt)
                y_off = _dot(cg, hg.astype(BF16))
                ms = []
                for r, hd in enumerate(heads):
                    seg = cs_cols[r] - cs[hd:hd + 1, :]
                    dec = jnp.exp(jnp.where(tri[d], seg, -jnp.inf))
                    ms.append((scores * dec * dt[hd:hd + 1, :]).astype(BF16))
                t0 = xg[:, 0:CHUNK]
                t1 = xg[:, CHUNK:2 * CHUNK]
                y0 = jnp.where(lane_lo, _dot(ms[0], t0), _dot(ms[1], t0))
                y1 = _dot(ms[2], t1)
                e0 = jnp.where(lane_lo, jnp.exp(cs_cols[0]), jnp.exp(cs_cols[1]))
                e1 = jnp.exp(cs_cols[2])
                y_g = (jnp.concatenate([y0, y1], axis=1)
                       + y_off * jnp.concatenate([e0, e1], axis=1))
                cur = y_ref[pl.ds(row0, CHUNK), g * GROUP_PAD:(g + 1) * GROUP_PAD]
                y_ref[pl.ds(row0, CHUNK), g * GROUP_PAD:(g + 1) * GROUP_PAD] = cur + y_g
            w_cols = [colb(q_t, DT_DIR_ROWS + hd) for hd in heads]
            w0 = jnp.where(lane_lo, w_cols[0], w_cols[1])
            wcat = jnp.concatenate([w0, w_cols[2]], axis=1)
            xsc = (xg.astype(F32) * wcat).astype(BF16)
            ee = [e_end[hd:hd + 1, :] for hd in heads]
            e3 = jnp.where(lane256 < HEAD_DIM, ee[0],
                           jnp.where(lane256 < 2 * HEAD_DIM, ee[1], ee[2]))
            h_ref[d, g] = e3 * hg + _dot(bt, xsc)

    for i in range(nchunk_ctx):
        for d in range(2):
            ci = i if d == 0 else nchunk_ctx - 1 - i
            rows = slice(ci * CHUNK, (ci + 1) * CHUNK)
            raw = dtc_ref[0, d * DT_DIR_ROWS:(d + 1) * DT_DIR_ROWS, rows]
            direction(d, xsc_ref[0, rows, :], bmc_ref[0, rows, :], cmc_ref[0, rows, :],
                      raw, 0, False)

    def step(i, carry):
        for d in range(2):
            ci = i if d == 0 else nchunk - 1 - i
            row0 = pl.multiple_of(ci * CHUNK, CHUNK)
            raw = dt_ref[0, d * DT_DIR_ROWS:(d + 1) * DT_DIR_ROWS, pl.ds(row0, CHUNK)]
            direction(d, xs_ref[0, pl.ds(row0, CHUNK), :], bm_ref[0, pl.ds(row0, CHUNK), :],
                      cm_ref[0, pl.ds(row0, CHUNK), :], raw, row0, True)
        return carry

    lax.fori_loop(0, nchunk, step, 0)

    te = 256

    def epi(i, carry):
        row0 = pl.multiple_of(i * te, te)
        xr = xs_ref[0, pl.ds(row0, te), :].astype(F32)
        zr = z_ref[0, pl.ds(row0, te), :].astype(F32)
        y = y_ref[pl.ds(row0, te), :] + xr * dskip_ref[...]
        y = y * _silu(zr)
        out_ref[0, pl.ds(row0, te), :] = _rms(y, gssd_ref[...], n=SSD_WIDTH).astype(BF16)
        return carry

    lax.fori_loop(0, seq // te, epi, 0)


def _ssd_call(xs, bm, cm, z, dt, xsc, bmc, cmc, dtc, dtb, alog, dskip, gssd):
    bsz, seq, _ = xs.shape
    ctx_len = xsc.shape[1]
    per_b = lambda b: (b, 0, 0)
    const2 = lambda b: (0, 0)
    return pl.pallas_call(
        functools.partial(_ssd_kernel, seq=seq, ctx_len=ctx_len),
        grid=(bsz,),
        in_specs=[
            pl.BlockSpec((1, seq, SSD_PAD), per_b),
            pl.BlockSpec((1, seq, BC_WIDTH), per_b),
            pl.BlockSpec((1, seq, BC_WIDTH), per_b),
            pl.BlockSpec((1, seq, SSD_PAD), per_b),
            pl.BlockSpec((1, DT_ROWS, seq), per_b),
            pl.BlockSpec((1, ctx_len, SSD_PAD), per_b),
            pl.BlockSpec((1, ctx_len, BC_WIDTH), per_b),
            pl.BlockSpec((1, ctx_len, BC_WIDTH), per_b),
            pl.BlockSpec((1, DT_ROWS, ctx_len), per_b),
            pl.BlockSpec((DT_ROWS, 1), const2),
            pl.BlockSpec((DT_ROWS, 1), const2),
            pl.BlockSpec((1, SSD_PAD), const2),
            pl.BlockSpec((1, SSD_PAD), const2),
        ],
        out_specs=pl.BlockSpec((1, seq, SSD_PAD), per_b),
        out_shape=jax.ShapeDtypeStruct((bsz, seq, SSD_PAD), BF16),
        scratch_shapes=[
            pltpu.VMEM((2, SSD_GROUPS, D_STATE, GROUP_PAD), F32),
            pltpu.VMEM((seq, SSD_PAD), F32),
        ],
        compiler_params=pltpu.CompilerParams(
            dimension_semantics=("arbitrary",), vmem_limit_bytes=VMEM_LIMIT),
        name="ssd_scan",
    )(xs, bm, cm, z, dt, xsc, bmc, cmc, dtc, dtb, alog, dskip, gssd)


def _out_ffn_kernel(x_ref, yn_ref, uu_ref, cl_ref, sl_ref, mod_ref,
                    wof_ref, woy_ref, gpm_ref, gpf_ref, gpo_ref,
                    wg_ref, wu_ref, wd_ref, out_ref):
    b = pl.program_id(1)
    yf = (_dot(cl_ref[...].astype(BF16), uu_ref[0, :, 0:F_WIDTH])
          + _dot(sl_ref[...].astype(BF16), uu_ref[0, :, F_WIDTH:2 * F_WIDTH]))
    mix = _dot(yf.astype(BF16), wof_ref[...]) + _dot(yn_ref[0], woy_ref[...])

    def mod(k):
        return mod_ref[pl.ds(b, 1), k * D_MODEL:(k + 1) * D_MODEL]

    x1 = x_ref[0] + mod(2) * _rms(mix, gpm_ref[...])
    h2 = (_rms(x1, gpf_ref[...]) * (1.0 + mod(4)) + mod(3)).astype(BF16)
    gate = _dot(h2, wg_ref[...])
    up = _dot(h2, wu_ref[...])
    act = (_silu(gate) * up).astype(BF16)
    ffn = _dot(act, wd_ref[...])
    out_ref[0] = x1 + mod(5) * _rms(ffn, gpo_ref[...])


def _out_ffn_call(x, yn, uu, cl, sl, mod, wof, woy, gpm, gpf, gpo, wg, wu, wd, *, tm):
    bsz, seq, _ = x.shape
    d_ff = wg.shape[1]
    nt = seq // tm
    const2 = lambda j, b: (0, 0)
    tile3 = lambda j, b: (b, j, 0)
    single = dict(pipeline_mode=pl.Buffered(1))
    return pl.pallas_call(
        _out_ffn_kernel,
        grid=(nt, bsz),
        in_specs=[
            pl.BlockSpec((1, tm, D_MODEL), tile3),
            pl.BlockSpec((1, tm, SSD_PAD), tile3),
            pl.BlockSpec((1, seq, 2 * F_WIDTH), lambda j, b: (b, 0, 0)),
            pl.BlockSpec((tm, seq), lambda j, b: (j, 0)),
            pl.BlockSpec((tm, seq), lambda j, b: (j, 0)),
            pl.BlockSpec((MOD_ROWS, N_MOD * D_MODEL), const2),
            pl.BlockSpec((F_WIDTH, D_MODEL), const2, **single),
            pl.BlockSpec((SSD_PAD, D_MODEL), const2, **single),
            pl.BlockSpec((1, D_MODEL), const2),
            pl.BlockSpec((1, D_MODEL), const2),
            pl.BlockSpec((1, D_MODEL), const2),
            pl.BlockSpec((D_MODEL, d_ff), const2, **single),
            pl.BlockSpec((D_MODEL, d_ff), const2, **single),
            pl.BlockSpec((d_ff, D_MODEL), const2, **single),
        ],
        out_specs=pl.BlockSpec((1, tm, D_MODEL), tile3),
        out_shape=jax.ShapeDtypeStruct((bsz, seq, D_MODEL), F32),
        compiler_params=pltpu.CompilerParams(
            dimension_semantics=("arbitrary", "arbitrary"),
            vmem_limit_bytes=VMEM_LIMIT),
        name="out_ffn",
    )(x, yn, uu, cl, sl, mod, wof, woy, gpm, gpf, gpo, wg, wu, wd)


@functools.lru_cache(maxsize=None)
def _dft_tables(seq):
    k = np.arange(seq, dtype=np.int64)
    ang = 2.0 * np.pi * ((k[:, None] * k[None, :]) % seq).astype(np.float64) / seq
    scale = 1.0 / math.sqrt(seq)
    cl = (np.cos(ang) * scale).astype(np.float32)
    sl = (np.sin(ang) * scale).astype(np.float32)
    c = np.arange(FGROUP_DIM, dtype=np.int64)
    angc = 2.0 * np.pi * ((c[:, None] * c[None, :]) % FGROUP_DIM).astype(np.float64) / FGROUP_DIM
    sc = 1.0 / math.sqrt(FGROUP_DIM)
    wc = np.zeros((F_WIDTH, 2 * F_WIDTH), np.float32)
    for g in range(N_FGROUPS):
        s = slice(g * FGROUP_DIM, (g + 1) * FGROUP_DIM)
        wc[s, g * FGROUP_DIM:(g + 1) * FGROUP_DIM] = np.cos(angc) * sc
        wc[s, F_WIDTH + g * FGROUP_DIM:F_WIDTH + (g + 1) * FGROUP_DIM] = -np.sin(angc) * sc
    return cl, sl, wc


def _pad_groups(a, axis):
    a = jnp.moveaxis(a, axis, -1)
    lead = a.shape[:-1]
    a = a.reshape(lead + (SSD_GROUPS, HEADS_PER_GROUP * HEAD_DIM))
    a = jnp.pad(a, [(0, 0)] * len(lead) + [(0, 0), (0, GROUP_PAD - HEADS_PER_GROUP * HEAD_DIM)])
    a = a.reshape(lead + (SSD_PAD,))
    return jnp.moveaxis(a, -1, axis)


def _pad_dirs(a):
    a = jnp.pad(a, ((0, 0), (0, DT_DIR_ROWS - SSD_HEADS)))
    return a.reshape(DT_ROWS, 1)


def kernel(x, c, ctx, c_ctx, w_ada, b_ada, g_pre_mix, g_post_mix, g_pre_ffn, g_post_ffn,
           w_in, conv_w, conv_b, dt_bias, a_log, d_skip, g_ssd, w_out, w_gate, w_up, w_down):
    bsz, seq, _ = x.shape
    ctx_len = ctx.shape[1]
    l = 0
    cl_np, sl_np, wc_np = _dft_tables(seq)
    cl = jnp.asarray(cl_np)
    sl = jnp.asarray(sl_np)
    wc = jnp.asarray(wc_np)

    cc = jnp.concatenate(
        [c, c_ctx[None, :], jnp.zeros((MOD_ROWS - bsz - 1, D_MODEL), F32)], axis=0)
    mod = _ada_call(cc, w_ada[l], b_ada[l][None, :])

    wi = w_in[l]
    w_cat = jnp.concatenate([
        wi[:, :F_WIDTH],
        _pad_groups(wi[:, F_WIDTH:XBC_OFF], 1),
        _pad_groups(wi[:, XBC_OFF:XBC_OFF + SSD_WIDTH], 1),
        wi[:, XBC_OFF + SSD_WIDTH:XBC_OFF + CONV_DIM],
    ], axis=1).astype(BF16)
    w_dt = wi[:, XBC_OFF + CONV_DIM:].T.reshape(2, SSD_HEADS, D_MODEL)
    w_dt = jnp.pad(w_dt, ((0, 0), (0, DT_DIR_ROWS - SSD_HEADS), (0, 0))).reshape(DT_ROWS, D_MODEL)
    w_dt = w_dt.astype(BF16)
    cw = conv_w[l].T
    cw = jnp.concatenate([_pad_groups(cw[:, :SSD_WIDTH], 1), cw[:, SSD_WIDTH:]], axis=1)
    cb = conv_b[l][None, :]
    cb = jnp.concatenate([_pad_groups(cb[:, :SSD_WIDTH], 1), cb[:, SSD_WIDTH:]], axis=1)
    g_pre = g_pre_mix[l][None, :]

    xs_c, bm_c, cm_c, dt_c = _inproj_call(
        ctx, mod, g_pre, w_cat, w_dt, cw, cb, None, ctx=True, tm=ctx_len, row_len=ctx_len)
    uu, z, xs, bm, cm, dt = _inproj_call(
        x, mod, g_pre, w_cat, w_dt, cw, cb, wc, ctx=False, tm=512, row_len=GRID_W)

    dskip = _pad_groups(jnp.repeat(d_skip[l], HEAD_DIM)[None, :], 1)
    gssd = _pad_groups(g_ssd[l][None, :], 1)
    yn = _ssd_call(xs, bm, cm, z, dt, xs_c, bm_c, cm_c, dt_c,
                   _pad_dirs(dt_bias[l]), _pad_dirs(a_log[l]), dskip, gssd)

    wo = w_out[l]
    wof = wo[:F_WIDTH].astype(BF16)
    woy = _pad_groups(wo[F_WIDTH:], 0).astype(BF16)
    return _out_ffn_call(
        x, yn, uu, cl, sl, mod, wof, woy,
        g_post_mix[l][None, :], g_pre_ffn[l][None, :], g_post_ffn[l][None, :],
        w_gate[l].astype(BF16), w_up[l].astype(BF16), w_down[l].astype(BF16), tm=256)
```

```python
import functools
import math

import jax
import jax.numpy as jnp
import numpy as np
from jax import lax
from jax.experimental import pallas as pl
from jax.experimental.pallas import tpu as pltpu

F32 = jnp.float32
BF16 = jnp.bfloat16

D_MODEL = 1024
GRID_W = 64
F_WIDTH = 256
N_FGROUPS = 4
FGROUP_DIM = F_WIDTH // N_FGROUPS
SSD_WIDTH = 768
HEAD_DIM = 64
SSD_HEADS = 12
SSD_GROUPS = 4
HEADS_PER_GROUP = 3
D_STATE = 128
BC_WIDTH = SSD_GROUPS * D_STATE
D_CONV = 7
HALF_CONV = D_CONV // 2
CHUNK = 128
XBC_OFF = F_WIDTH + SSD_WIDTH
CONV_DIM = SSD_WIDTH + 2 * BC_WIDTH
N_MOD = 6
EPS = 1e-6
LOG2E = math.log2(math.e)

GROUP_PAD = 256
SSD_PAD = SSD_GROUPS * GROUP_PAD
XBC_PAD = SSD_PAD + 2 * BC_WIDTH
W_COLS = F_WIDTH + SSD_PAD + XBC_PAD
CTX_COL0 = F_WIDTH + SSD_PAD
DT_ROWS = 32
DT_DIR_ROWS = 16
MOD_ROWS = 16
MOD_CTX_ROW = 8

SUBLANE = 8
LANE = 128
VMEM_LIMIT = 56 * 1024 * 1024


def _dot(a, b):
    return jnp.dot(a, b, preferred_element_type=F32)


def _silu(v):
    return v * jax.nn.sigmoid(v)


def _rms(v, g, n=None):
    n = v.shape[-1] if n is None else n
    ms = jnp.sum(v * v, axis=-1, keepdims=True) * (1.0 / n)
    return v * lax.rsqrt(ms + EPS) * g


def _ada_kernel(c_ref, w_ref, b_ref, o_ref):
    s = _silu(c_ref[...]).astype(BF16)
    o_ref[...] = _dot(s, w_ref[...].astype(BF16)) + b_ref[...]


def _ada_call(cc, w_ada, b_ada):
    n = w_ada.shape[1]
    tn = 768
    return pl.pallas_call(
        _ada_kernel,
        grid=(n // tn,),
        in_specs=[
            pl.BlockSpec((MOD_ROWS, D_MODEL), lambda j: (0, 0)),
            pl.BlockSpec((D_MODEL, tn), lambda j: (0, j)),
            pl.BlockSpec((1, tn), lambda j: (0, j)),
        ],
        out_specs=pl.BlockSpec((MOD_ROWS, tn), lambda j: (0, j)),
        out_shape=jax.ShapeDtypeStruct((MOD_ROWS, n), F32),
        compiler_params=pltpu.CompilerParams(dimension_semantics=("arbitrary",)),
        name="ada_mod",
    )(cc, w_ada, b_ada)


CONV_STRIDE = SUBLANE + 1
CONV_PIECE = SUBLANE * CONV_STRIDE


def _conv_pieces(tm, row_len):
    padded = (tm // row_len) * (row_len + SUBLANE)
    return -(-padded // CONV_PIECE)


def _inproj_kernel(*refs, ctx, tm, row_len):
    if ctx:
        (x_ref, mod_ref, g_ref, w_ref, wdt_ref, cw_ref, cb_ref,
         xs_ref, bt_ref, cm_ref, dt_ref, pad_ref, cv_ref) = refs
    else:
        (x_ref, mod_ref, g_ref, w_ref, wdt_ref, cw_ref, cb_ref, wc_ref,
         uu_ref, z_ref, xs_ref, bt_ref, cm_ref, dt_ref, pad_ref, cv_ref) = refs

    xt = x_ref[0]
    if ctx:
        shift = mod_ref[MOD_CTX_ROW:MOD_CTX_ROW + 1, 0:D_MODEL]
        scale = mod_ref[MOD_CTX_ROW:MOD_CTX_ROW + 1, D_MODEL:2 * D_MODEL]
    else:
        b = pl.program_id(0)
        shift = mod_ref[pl.ds(b, 1), 0:D_MODEL]
        scale = mod_ref[pl.ds(b, 1), D_MODEL:2 * D_MODEL]
    h = _rms(xt, g_ref[...]) * (1.0 + scale) + shift
    hb = h.astype(BF16)

    if not ctx:
        uf = _dot(hb, w_ref[:, 0:F_WIDTH])
        uu_ref[0] = _dot(uf.astype(BF16), wc_ref[...].astype(BF16)).astype(BF16)
        z_ref[0] = _dot(hb, w_ref[:, F_WIDTH:CTX_COL0]).astype(BF16)
        xbc = _dot(hb, w_ref[:, CTX_COL0:W_COLS])
    else:
        xbc = _dot(hb, w_ref[:, CTX_COL0:W_COLS])
    dt_ref[0] = lax.dot_general(wdt_ref[...], hb, (((1,), (1,)), ((), ())),
                                preferred_element_type=F32)

    pitch = row_len + SUBLANE
    nrow = tm // row_len
    npiece = _conv_pieces(tm, row_len)
    data_end = SUBLANE + nrow * pitch
    alloc_end = pad_ref.shape[1]
    zrow = jnp.zeros((SUBLANE, LANE), F32)
    for j in range(XBC_PAD // LANE):
        pad_ref[j, 0:SUBLANE, :] = zrow
        for r in range(nrow):
            base = SUBLANE + r * pitch
            pad_ref[j, base:base + row_len, :] = xbc[r * row_len:(r + 1) * row_len,
                                                     j * LANE:(j + 1) * LANE]
            pad_ref[j, base + row_len:base + pitch, :] = zrow
        for z0 in range(data_end, alloc_end, SUBLANE):
            pad_ref[j, z0:z0 + SUBLANE, :] = zrow

    for j in range(XBC_PAD // LANE):
        wk = [jnp.broadcast_to(cw_ref[k:k + 1, j * LANE:(j + 1) * LANE], (SUBLANE, LANE))
              for k in range(D_CONV)]
        bias = jnp.broadcast_to(cb_ref[:, j * LANE:(j + 1) * LANE], (SUBLANE, LANE))
        for p in range(npiece):
            base = SUBLANE + p * CONV_PIECE
            wins = {v: pad_ref[j, pl.ds(base + v, SUBLANE, stride=CONV_STRIDE), :]
                    for v in range(-HALF_CONV, CONV_STRIDE + HALF_CONV)}
            for a in range(CONV_STRIDE):
                acc = bias
                for k in range(D_CONV):
                    acc = acc + wins[a + k - HALF_CONV] * wk[k]
                cv_ref[j, pl.ds(base + a, SUBLANE, stride=CONV_STRIDE), :] = acc

    def conv_rows(j, t0, n):
        parts = []
        for t in range(t0, t0 + n, min(n, row_len)):
            src = SUBLANE + (t // row_len) * pitch + t % row_len
            parts.append(cv_ref[j, src:src + min(n, row_len), :])
        return _silu(parts[0] if len(parts) == 1 else jnp.concatenate(parts, axis=0))

    for j in range(XBC_PAD // LANE):
        c0 = j * LANE
        for ci in range(tm // CHUNK):
            val = conv_rows(j, ci * CHUNK, CHUNK)
            rows = slice(ci * CHUNK, (ci + 1) * CHUNK)
            if c0 < SSD_PAD:
                xs_ref[0, rows, c0:c0 + LANE] = val.astype(BF16)
            elif c0 < SSD_PAD + BC_WIDTH:
                cc = c0 - SSD_PAD
                bt_ref[0, ci, cc:cc + LANE, :] = val.T.astype(BF16)
            else:
                cc = c0 - SSD_PAD - BC_WIDTH
                cm_ref[0, rows, cc:cc + LANE] = val.astype(BF16)


def _inproj_call(xin, mod, g, w, wdt, cw, cb, wc, *, ctx, tm, row_len):
    bsz, seq, _ = xin.shape
    nt = seq // tm
    const2 = lambda b, j: (0, 0)
    tile3 = lambda b, j: (b, j, 0)
    in_specs = [
        pl.BlockSpec((1, tm, D_MODEL), tile3),
        pl.BlockSpec((MOD_ROWS, N_MOD * D_MODEL), const2),
        pl.BlockSpec((1, D_MODEL), const2),
        pl.BlockSpec((D_MODEL, W_COLS), const2),
        pl.BlockSpec((DT_ROWS, D_MODEL), const2),
        pl.BlockSpec((D_CONV, XBC_PAD), const2),
        pl.BlockSpec((1, XBC_PAD), const2),
    ]
    args = [xin, mod, g, w, wdt, cw, cb]
    out_specs = []
    out_shape = []
    if not ctx:
        in_specs.append(pl.BlockSpec((F_WIDTH, 2 * F_WIDTH), const2))
        args.append(wc)
        out_specs += [pl.BlockSpec((1, tm, 2 * F_WIDTH), tile3),
                      pl.BlockSpec((1, tm, SSD_PAD), tile3)]
        out_shape += [jax.ShapeDtypeStruct((bsz, seq, 2 * F_WIDTH), BF16),
                      jax.ShapeDtypeStruct((bsz, seq, SSD_PAD), BF16)]
    out_specs += [pl.BlockSpec((1, tm, SSD_PAD), tile3),
                  pl.BlockSpec((1, tm // CHUNK, BC_WIDTH, CHUNK), lambda b, j: (b, j, 0, 0)),
                  pl.BlockSpec((1, tm, BC_WIDTH), tile3),
                  pl.BlockSpec((1, DT_ROWS, tm), lambda b, j: (b, 0, j))]
    out_shape += [jax.ShapeDtypeStruct((bsz, seq, SSD_PAD), BF16),
                  jax.ShapeDtypeStruct((bsz, seq // CHUNK, BC_WIDTH, CHUNK), BF16),
                  jax.ShapeDtypeStruct((bsz, seq, BC_WIDTH), BF16),
                  jax.ShapeDtypeStruct((bsz, DT_ROWS, seq), F32)]
    pad_rows = 2 * SUBLANE + _conv_pieces(tm, row_len) * CONV_PIECE
    conv_scratch = pltpu.VMEM((XBC_PAD // LANE, pad_rows, LANE), F32)
    return pl.pallas_call(
        functools.partial(_inproj_kernel, ctx=ctx, tm=tm, row_len=row_len),
        grid=(bsz, nt),
        in_specs=in_specs,
        out_specs=out_specs,
        out_shape=out_shape,
        scratch_shapes=[conv_scratch, conv_scratch],
        compiler_params=pltpu.CompilerParams(
            dimension_semantics=("arbitrary", "arbitrary"),
            vmem_limit_bytes=VMEM_LIMIT),
        name="inproj_ctx" if ctx else "inproj_lat",
    )(*args)


def _ssd_kernel(xs_ref, bt_ref, cm_ref, z_ref, dt_ref,
                xsc_ref, btc_ref, cmc_ref, dtc_ref,
                dtb_ref, alog_ref, dskip_ref, gssd_ref,
                out_ref, h_ref, y_ref, r2_ref, w2_ref, ee_ref, qt_ref, *, seq, ctx_len):
    nchunk = seq // CHUNK
    nchunk_ctx = ctx_len // CHUNK

    h_ref[...] = jnp.zeros(h_ref.shape, F32)
    y_ref[...] = jnp.zeros(y_ref.shape, F32)

    bias = dtb_ref[...]
    nega = -jnp.exp(alog_ref[...])
    sub_i = lax.broadcasted_iota(jnp.int32, (CHUNK, CHUNK), 0)
    lane_i = lax.broadcasted_iota(jnp.int32, (CHUNK, CHUNK), 1)
    tri = (lane_i <= sub_i, lane_i >= sub_i)
    lane_lo = lane_i < HEAD_DIM
    lane_row = lax.broadcasted_iota(jnp.int32, (1, CHUNK), 1)

    def colb(mat_t, idx):
        return jnp.broadcast_to(mat_t[:, idx:idx + 1], (CHUNK, CHUNK))

    def prepare(d, raws):
        r0 = d * DT_DIR_ROWS
        n = len(raws)
        v = jnp.concatenate(raws, axis=0) + jnp.concatenate([bias[r0:r0 + DT_DIR_ROWS]] * n, axis=0)
        dt = jnp.maximum(v, 0.0) + jnp.log1p(jnp.exp(-jnp.abs(v)))
        da = dt * jnp.concatenate([nega[r0:r0 + DT_DIR_ROWS]] * n, axis=0)
        lane_n = lax.broadcasted_iota(jnp.int32, da.shape, 1)
        cs = da
        sh = 1
        while sh < CHUNK:
            if d == 0:
                cs = cs + jnp.where(lane_n >= sh, pltpu.roll(cs, sh, axis=1), 0.0)
            else:
                cs = cs + jnp.where(lane_n < CHUNK - sh, pltpu.roll(cs, CHUNK - sh, axis=1), 0.0)
            sh *= 2
        a_end = jnp.sum(da, axis=1, keepdims=True)
        cs2 = cs * LOG2E
        r2 = cs2 - jnp.log2(dt)
        w2 = jnp.exp2(a_end * LOG2E - r2)
        ee = jnp.broadcast_to(jnp.exp(a_end), da.shape)
        return cs2, r2, w2, ee

    def store_prep(d, k0, raws):
        cs2, r2, w2, ee = prepare(d, raws)
        for i in range(len(raws)):
            rows = slice(i * DT_DIR_ROWS, (i + 1) * DT_DIR_ROWS)
            r2_ref[d, k0 + i] = r2[rows]
            w2_ref[d, k0 + i] = w2[rows]
            ee_ref[d, k0 + i] = ee[rows]
            qt_ref[d, k0 + i] = jnp.concatenate(
                [cs2[rows], jnp.zeros((CHUNK - DT_DIR_ROWS, CHUNK), F32)], axis=0).T

    def direction(d, x, btc, cc, k, row0, with_output):
        r2 = r2_ref[d, k]
        w2 = w2_ref[d, k].astype(BF16)
        ee = ee_ref[d, k]
        q_t = qt_ref[d, k]

        for g in range(SSD_GROUPS):
            cg = cc[:, g * D_STATE:(g + 1) * D_STATE]
            bt = btc[g * D_STATE:(g + 1) * D_STATE, :]
            hg = h_ref[d, g]
            xg = x[:, g * GROUP_PAD:(g + 1) * GROUP_PAD]
            t0 = xg[:, 0:CHUNK]
            t1 = xg[:, CHUNK:2 * CHUNK]
            heads = [g * HEADS_PER_GROUP + r for r in range(HEADS_PER_GROUP)]
            if with_output:
                scores = _dot(cg, bt)
                hb = hg.astype(BF16)
                lhs = []
                for hd in heads:
                    col = colb(q_t, hd)
                    seg = col - r2[hd:hd + 1, :]
                    m = (scores * jnp.exp2(jnp.where(tri[d], seg, -jnp.inf))).astype(BF16)
                    ce = cg * jnp.exp2(col).astype(BF16)
                    lhs.append(jnp.concatenate([m, ce], axis=1))
                rhs0 = jnp.concatenate([t0, hb[:, 0:CHUNK]], axis=0)
                rhs1 = jnp.concatenate([t1, hb[:, CHUNK:2 * CHUNK]], axis=0)
                y01 = _dot(jnp.concatenate([lhs[0], lhs[1]], axis=0), rhs0)
                y0 = jnp.where(lane_lo, y01[0:CHUNK], y01[CHUNK:2 * CHUNK])
                y1 = _dot(lhs[2], rhs1)
                y_g = jnp.concatenate([y0, y1], axis=1)
                cur = y_ref[pl.ds(row0, CHUNK), g * GROUP_PAD:(g + 1) * GROUP_PAD]
                y_ref[pl.ds(row0, CHUNK), g * GROUP_PAD:(g + 1) * GROUP_PAD] = cur + y_g
            btw = [bt * jnp.broadcast_to(w2[hd:hd + 1, :], (D_STATE, CHUNK)) for hd in heads]
            s01 = _dot(jnp.concatenate([btw[0], btw[1]], axis=0), t0)
            s0 = jnp.where(lane_lo, s01[0:D_STATE], s01[D_STATE:2 * D_STATE])
            s1 = _dot(btw[2], t1)
            er = [ee[hd:hd + 1, :] for hd in heads]
            e3 = jnp.concatenate([jnp.where(lane_row < HEAD_DIM, er[0], er[1]), er[2]], axis=1)
            h_ref[d, g] = e3 * hg + jnp.concatenate([s0, s1], axis=1)

    for d in range(2):
        rows_d = slice(d * DT_DIR_ROWS, (d + 1) * DT_DIR_ROWS)
        store_prep(d, 0, [dtc_ref[0, rows_d, c * CHUNK:(c + 1) * CHUNK] for c in range(nchunk_ctx)])
        store_prep(d, nchunk_ctx, [dt_ref[0, rows_d, c * CHUNK:(c + 1) * CHUNK] for c in range(nchunk)])

    for i in range(nchunk_ctx):
        for d in range(2):
            ci = i if d == 0 else nchunk_ctx - 1 - i
            rows = slice(ci * CHUNK, (ci + 1) * CHUNK)
            direction(d, xsc_ref[0, rows, :], btc_ref[0, ci], cmc_ref[0, rows, :], ci, 0, False)

    def step(i, carry):
        for d in range(2):
            ci = i if d == 0 else nchunk - 1 - i
            row0 = pl.multiple_of(ci * CHUNK, CHUNK)
            direction(d, xs_ref[0, pl.ds(row0, CHUNK), :], bt_ref[0, ci],
                      cm_ref[0, pl.ds(row0, CHUNK), :], nchunk_ctx + ci, row0, True)
        return carry

    lax.fori_loop(0, nchunk, step, 0)

    te = 256

    def epi(i, carry):
        row0 = pl.multiple_of(i * te, te)
        xr = xs_ref[0, pl.ds(row0, te), :].astype(F32)
        zr = z_ref[0, pl.ds(row0, te), :].astype(F32)
        y = y_ref[pl.ds(row0, te), :] + xr * dskip_ref[...]
        y = y * _silu(zr)
        out_ref[0, pl.ds(row0, te), :] = _rms(y, gssd_ref[...], n=SSD_WIDTH).astype(BF16)
        return carry

    lax.fori_loop(0, seq // te, epi, 0)


def _ssd_call(xs, bm, cm, z, dt, xsc, bmc, cmc, dtc, dtb, alog, dskip, gssd):
    bsz, seq, _ = xs.shape
    ctx_len = xsc.shape[1]
    nck = (seq + ctx_len) // CHUNK
    per_b = lambda b: (b, 0, 0)
    const2 = lambda b: (0, 0)
    return pl.pallas_call(
        functools.partial(_ssd_kernel, seq=seq, ctx_len=ctx_len),
        grid=(bsz,),
        in_specs=[
            pl.BlockSpec((1, seq, SSD_PAD), per_b),
            pl.BlockSpec((1, seq // CHUNK, BC_WIDTH, CHUNK), lambda b: (b, 0, 0, 0)),
            pl.BlockSpec((1, seq, BC_WIDTH), per_b),
            pl.BlockSpec((1, seq, SSD_PAD), per_b),
            pl.BlockSpec((1, DT_ROWS, seq), per_b),
            pl.BlockSpec((1, ctx_len, SSD_PAD), per_b),
            pl.BlockSpec((1, ctx_len // CHUNK, BC_WIDTH, CHUNK), lambda b: (b, 0, 0, 0)),
            pl.BlockSpec((1, ctx_len, BC_WIDTH), per_b),
            pl.BlockSpec((1, DT_ROWS, ctx_len), per_b),
            pl.BlockSpec((DT_ROWS, 1), const2),
            pl.BlockSpec((DT_ROWS, 1), const2),
            pl.BlockSpec((1, SSD_PAD), const2),
            pl.BlockSpec((1, SSD_PAD), const2),
        ],
        out_specs=pl.BlockSpec((1, seq, SSD_PAD), per_b),
        out_shape=jax.ShapeDtypeStruct((bsz, seq, SSD_PAD), BF16),
        scratch_shapes=[
            pltpu.VMEM((2, SSD_GROUPS, D_STATE, GROUP_PAD), F32),
            pltpu.VMEM((seq, SSD_PAD), F32),
            pltpu.VMEM((2, nck, DT_DIR_ROWS, CHUNK), F32),
            pltpu.VMEM((2, nck, DT_DIR_ROWS, CHUNK), F32),
            pltpu.VMEM((2, nck, DT_DIR_ROWS, CHUNK), F32),
            pltpu.VMEM((2, nck, CHUNK, CHUNK), F32),
        ],
        compiler_params=pltpu.CompilerParams(
            dimension_semantics=("arbitrary",), vmem_limit_bytes=VMEM_LIMIT),
        name="ssd_scan",
    )(xs, bm, cm, z, dt, xsc, bmc, cmc, dtc, dtb, alog, dskip, gssd)


def _out_ffn_kernel(x_ref, yn_ref, uu_ref, cl_ref, sl_ref, mod_ref,
                    wof_ref, woy_ref, gpm_ref, gpf_ref, gpo_ref,
                    wg_ref, wu_ref, wd_ref, out_ref):
    b = pl.program_id(1)
    yf = (_dot(cl_ref[...].astype(BF16), uu_ref[0, :, 0:F_WIDTH])
          + _dot(sl_ref[...].astype(BF16), uu_ref[0, :, F_WIDTH:2 * F_WIDTH]))
    mix = _dot(yf.astype(BF16), wof_ref[...]) + _dot(yn_ref[0], woy_ref[...])

    def mod(k):
        return mod_ref[pl.ds(b, 1), k * D_MODEL:(k + 1) * D_MODEL]

    x1 = x_ref[0] + mod(2) * _rms(mix, gpm_ref[...])
    h2 = (_rms(x1, gpf_ref[...]) * (1.0 + mod(4)) + mod(3)).astype(BF16)
    gate = _dot(h2, wg_ref[...])
    up = _dot(h2, wu_ref[...])
    act = (_silu(gate) * up).astype(BF16)
    ffn = _dot(act, wd_ref[...])
    out_ref[0] = x1 + mod(5) * _rms(ffn, gpo_ref[...])


def _out_ffn_call(x, yn, uu, cl, sl, mod, wof, woy, gpm, gpf, gpo, wg, wu, wd, *, tm):
    bsz, seq, _ = x.shape
    d_ff = wg.shape[1]
    nt = seq // tm
    const2 = lambda j, b: (0, 0)
    tile3 = lambda j, b: (b, j, 0)
    single = dict(pipeline_mode=pl.Buffered(1))
    return pl.pallas_call(
        _out_ffn_kernel,
        grid=(nt, bsz),
        in_specs=[
            pl.BlockSpec((1, tm, D_MODEL), tile3),
            pl.BlockSpec((1, tm, SSD_PAD), tile3),
            pl.BlockSpec((1, seq, 2 * F_WIDTH), lambda j, b: (b, 0, 0)),
            pl.BlockSpec((tm, seq), lambda j, b: (j, 0)),
            pl.BlockSpec((tm, seq), lambda j, b: (j, 0)),
            pl.BlockSpec((MOD_ROWS, N_MOD * D_MODEL), const2),
            pl.BlockSpec((F_WIDTH, D_MODEL), const2, **single),
            pl.BlockSpec((SSD_PAD, D_MODEL), const2, **single),
            pl.BlockSpec((1, D_MODEL), const2),
            pl.BlockSpec((1, D_MODEL), const2),
            pl.BlockSpec((1, D_MODEL), const2),
            pl.BlockSpec((D_MODEL, d_ff), const2, **single),
            pl.BlockSpec((D_MODEL, d_ff), const2, **single),
            pl.BlockSpec((d_ff, D_MODEL), const2, **single),
        ],
        out_specs=pl.BlockSpec((1, tm, D_MODEL), tile3),
        out_shape=jax.ShapeDtypeStruct((bsz, seq, D_MODEL), F32),
        compiler_params=pltpu.CompilerParams(
            dimension_semantics=("arbitrary", "arbitrary"),
            vmem_limit_bytes=VMEM_LIMIT),
        name="out_ffn",
    )(x, yn, uu, cl, sl, mod, wof, woy, gpm, gpf, gpo, wg, wu, wd)


@functools.lru_cache(maxsize=None)
def _dft_tables(seq):
    k = np.arange(seq, dtype=np.int64)
    ang = 2.0 * np.pi * ((k[:, None] * k[None, :]) % seq).astype(np.float64) / seq
    scale = 1.0 / math.sqrt(seq)
    cl = (np.cos(ang) * scale).astype(np.float32)
    sl = (np.sin(ang) * scale).astype(np.float32)
    c = np.arange(FGROUP_DIM, dtype=np.int64)
    angc = 2.0 * np.pi * ((c[:, None] * c[None, :]) % FGROUP_DIM).astype(np.float64) / FGROUP_DIM
    sc = 1.0 / math.sqrt(FGROUP_DIM)
    wc = np.zeros((F_WIDTH, 2 * F_WIDTH), np.float32)
    for g in range(N_FGROUPS):
        s = slice(g * FGROUP_DIM, (g + 1) * FGROUP_DIM)
        wc[s, g * FGROUP_DIM:(g + 1) * FGROUP_DIM] = np.cos(angc) * sc
        wc[s, F_WIDTH + g * FGROUP_DIM:F_WIDTH + (g + 1) * FGROUP_DIM] = -np.sin(angc) * sc
    return cl, sl, wc


def _pad_groups(a, axis):
    a = jnp.moveaxis(a, axis, -1)
    lead = a.shape[:-1]
    a = a.reshape(lead + (SSD_GROUPS, HEADS_PER_GROUP * HEAD_DIM))
    a = jnp.pad(a, [(0, 0)] * len(lead) + [(0, 0), (0, GROUP_PAD - HEADS_PER_GROUP * HEAD_DIM)])
    a = a.reshape(lead + (SSD_PAD,))
    return jnp.moveaxis(a, -1, axis)


def _pad_dirs(a):
    a = jnp.pad(a, ((0, 0), (0, DT_DIR_ROWS - SSD_HEADS)))
    return a.reshape(DT_ROWS, 1)


def kernel(x, c, ctx, c_ctx, w_ada, b_ada, g_pre_mix, g_post_mix, g_pre_ffn, g_post_ffn,
           w_in, conv_w, conv_b, dt_bias, a_log, d_skip, g_ssd, w_out, w_gate, w_up, w_down):
    bsz, seq, _ = x.shape
    ctx_len = ctx.shape[1]
    l = 0
    cl_np, sl_np, wc_np = _dft_tables(seq)
    cl = jnp.asarray(cl_np)
    sl = jnp.asarray(sl_np)
    wc = jnp.asarray(wc_np)

    cc = jnp.concatenate(
        [c, c_ctx[None, :], jnp.zeros((MOD_ROWS - bsz - 1, D_MODEL), F32)], axis=0)
    mod = _ada_call(cc, w_ada[l], b_ada[l][None, :])

    wi = w_in[l]
    w_cat = jnp.concatenate([
        wi[:, :F_WIDTH],
        _pad_groups(wi[:, F_WIDTH:XBC_OFF], 1),
        _pad_groups(wi[:, XBC_OFF:XBC_OFF + SSD_WIDTH], 1),
        wi[:, XBC_OFF + SSD_WIDTH:XBC_OFF + CONV_DIM],
    ], axis=1).astype(BF16)
    w_dt = wi[:, XBC_OFF + CONV_DIM:].T.reshape(2, SSD_HEADS, D_MODEL)
    w_dt = jnp.pad(w_dt, ((0, 0), (0, DT_DIR_ROWS - SSD_HEADS), (0, 0))).reshape(DT_ROWS, D_MODEL)
    w_dt = w_dt.astype(BF16)
    cw = conv_w[l].T
    cw = jnp.concatenate([_pad_groups(cw[:, :SSD_WIDTH], 1), cw[:, SSD_WIDTH:]], axis=1)
    cb = conv_b[l][None, :]
    cb = jnp.concatenate([_pad_groups(cb[:, :SSD_WIDTH], 1), cb[:, SSD_WIDTH:]], axis=1)
    g_pre = g_pre_mix[l][None, :]

    xs_c, bm_c, cm_c, dt_c = _inproj_call(
        ctx, mod, g_pre, w_cat, w_dt, cw, cb, None, ctx=True, tm=ctx_len, row_len=ctx_len)
    uu, z, xs, bm, cm, dt = _inproj_call(
        x, mod, g_pre, w_cat, w_dt, cw, cb, wc, ctx=False, tm=512, row_len=GRID_W)

    dskip = _pad_groups(jnp.repeat(d_skip[l], HEAD_DIM)[None, :], 1)
    gssd = _pad_groups(g_ssd[l][None, :], 1)
    yn = _ssd_call(xs, bm, cm, z, dt, xs_c, bm_c, cm_c, dt_c,
                   _pad_dirs(dt_bias[l]), _pad_dirs(a_log[l]), dskip, gssd)

    wo = w_out[l]
    wof = wo[:F_WIDTH].astype(BF16)
    woy = _pad_groups(wo[F_WIDTH:], 0).astype(BF16)
    return _out_ffn_call(
        x, yn, uu, cl, sl, mod, wof, woy,
        g_post_mix[l][None, :], g_pre_ffn[l][None, :], g_post_ffn[l][None, :],
        w_gate[l].astype(BF16), w_up[l].astype(BF16), w_down[l].astype(BF16), tm=256)
```

```python
import functools
import math

import jax
import jax.numpy as jnp
import numpy as np
from jax import lax
from jax.experimental import pallas as pl
from jax.experimental.pallas import tpu as pltpu

F32 = jnp.float32
BF16 = jnp.bfloat16

D_MODEL = 1024
GRID_W = 64
F_WIDTH = 256
N_FGROUPS = 4
FGROUP_DIM = F_WIDTH // N_FGROUPS
SSD_WIDTH = 768
HEAD_DIM = 64
SSD_HEADS = 12
SSD_GROUPS = 4
HEADS_PER_GROUP = 3
D_STATE = 128
BC_WIDTH = SSD_GROUPS * D_STATE
D_CONV = 7
HALF_CONV = D_CONV // 2
CHUNK = 128
XBC_OFF = F_WIDTH + SSD_WIDTH
CONV_DIM = SSD_WIDTH + 2 * BC_WIDTH
N_MOD = 6
EPS = 1e-6
LOG2E = math.log2(math.e)

GROUP_PAD = 256
SSD_PAD = SSD_GROUPS * GROUP_PAD
XBC_PAD = SSD_PAD + 2 * BC_WIDTH
W_COLS = F_WIDTH + SSD_PAD + XBC_PAD
CTX_COL0 = F_WIDTH + SSD_PAD
DT_ROWS = 32
DT_DIR_ROWS = 16
MOD_ROWS = 16
MOD_CTX_ROW = 8

SUBLANE = 8
LANE = 128
VMEM_LIMIT = 56 * 1024 * 1024


def _dot(a, b):
    return jnp.dot(a, b, preferred_element_type=F32)


def _silu(v):
    return v * jax.nn.sigmoid(v)


def _rms(v, g, n=None):
    n = v.shape[-1] if n is None else n
    ms = jnp.sum(v * v, axis=-1, keepdims=True) * (1.0 / n)
    return v * lax.rsqrt(ms + EPS) * g


def _ada_kernel(c_ref, w_ref, b_ref, o_ref):
    s = _silu(c_ref[...]).astype(BF16)
    o_ref[...] = _dot(s, w_ref[...].astype(BF16)) + b_ref[...]


def _ada_call(cc, w_ada, b_ada, layer):
    n = w_ada.shape[2]
    tn = 768
    return pl.pallas_call(
        _ada_kernel,
        grid=(n // tn,),
        in_specs=[
            pl.BlockSpec((MOD_ROWS, D_MODEL), lambda j: (0, 0)),
            pl.BlockSpec((None, D_MODEL, tn), lambda j: (layer, 0, j)),
            pl.BlockSpec((1, tn), lambda j: (0, j)),
        ],
        out_specs=pl.BlockSpec((MOD_ROWS, tn), lambda j: (0, j)),
        out_shape=jax.ShapeDtypeStruct((MOD_ROWS, n), F32),
        compiler_params=pltpu.CompilerParams(dimension_semantics=("arbitrary",)),
        name="ada_mod",
    )(cc, w_ada, b_ada)


CONV_STRIDE = SUBLANE + 1
CONV_PIECE = SUBLANE * CONV_STRIDE


def _conv_pieces(tm, row_len):
    padded = (tm // row_len) * (row_len + SUBLANE)
    return -(-padded // CONV_PIECE)


def _inproj_kernel(*refs, ctx, tm, row_len):
    if ctx:
        (x_ref, mod_ref, g_ref, w_ref, wdt_ref, cw_ref, cb_ref,
         xs_ref, bt_ref, cm_ref, dt_ref, pad_ref, cv_ref) = refs
    else:
        (x_ref, mod_ref, g_ref, w_ref, wdt_ref, cw_ref, cb_ref, wc_ref,
         uu_ref, z_ref, xs_ref, bt_ref, cm_ref, dt_ref, pad_ref, cv_ref) = refs

    xt = x_ref[0]
    if ctx:
        shift = mod_ref[MOD_CTX_ROW:MOD_CTX_ROW + 1, 0:D_MODEL]
        scale = mod_ref[MOD_CTX_ROW:MOD_CTX_ROW + 1, D_MODEL:2 * D_MODEL]
    else:
        b = pl.program_id(0)
        shift = mod_ref[pl.ds(b, 1), 0:D_MODEL]
        scale = mod_ref[pl.ds(b, 1), D_MODEL:2 * D_MODEL]
    h = _rms(xt, g_ref[...]) * (1.0 + scale) + shift
    hb = h.astype(BF16)

    if not ctx:
        uf = _dot(hb, w_ref[:, 0:F_WIDTH])
        uu_ref[0] = _dot(uf.astype(BF16), wc_ref[...].astype(BF16)).astype(BF16)
        z_ref[0] = _dot(hb, w_ref[:, F_WIDTH:CTX_COL0]).astype(BF16)
        xbc = _dot(hb, w_ref[:, CTX_COL0:W_COLS])
    else:
        xbc = _dot(hb, w_ref[:, CTX_COL0:W_COLS])
    dt_ref[0] = lax.dot_general(wdt_ref[...], hb, (((1,), (1,)), ((), ())),
                                preferred_element_type=F32)

    pitch = row_len + SUBLANE
    nrow = tm // row_len
    npiece = _conv_pieces(tm, row_len)
    data_end = SUBLANE + nrow * pitch
    alloc_end = pad_ref.shape[1]
    zrow = jnp.zeros((SUBLANE, LANE), F32)
    for j in range(XBC_PAD // LANE):
        pad_ref[j, 0:SUBLANE, :] = zrow
        for r in range(nrow):
            base = SUBLANE + r * pitch
            pad_ref[j, base:base + row_len, :] = xbc[r * row_len:(r + 1) * row_len,
                                                     j * LANE:(j + 1) * LANE]
            pad_ref[j, base + row_len:base + pitch, :] = zrow
        for z0 in range(data_end, alloc_end, SUBLANE):
            pad_ref[j, z0:z0 + SUBLANE, :] = zrow

    for j in range(XBC_PAD // LANE):
        wk = [jnp.broadcast_to(cw_ref[k:k + 1, j * LANE:(j + 1) * LANE], (SUBLANE, LANE))
              for k in range(D_CONV)]
        bias = jnp.broadcast_to(cb_ref[:, j * LANE:(j + 1) * LANE], (SUBLANE, LANE))
        for p in range(npiece):
            base = SUBLANE + p * CONV_PIECE
            wins = {v: pad_ref[j, pl.ds(base + v, SUBLANE, stride=CONV_STRIDE), :]
                    for v in range(-HALF_CONV, CONV_STRIDE + HALF_CONV)}
            for a in range(CONV_STRIDE):
                acc = bias
                for k in range(D_CONV):
                    acc = acc + wins[a + k - HALF_CONV] * wk[k]
                cv_ref[j, pl.ds(base + a, SUBLANE, stride=CONV_STRIDE), :] = acc

    def conv_rows(j, t0, n):
        parts = []
        for t in range(t0, t0 + n, min(n, row_len)):
            src = SUBLANE + (t // row_len) * pitch + t % row_len
            parts.append(cv_ref[j, src:src + min(n, row_len), :])
        return _silu(parts[0] if len(parts) == 1 else jnp.concatenate(parts, axis=0))

    for j in range(XBC_PAD // LANE):
        c0 = j * LANE
        for ci in range(tm // CHUNK):
            val = conv_rows(j, ci * CHUNK, CHUNK)
            rows = slice(ci * CHUNK, (ci + 1) * CHUNK)
            if c0 < SSD_PAD:
                xs_ref[0, rows, c0:c0 + LANE] = val.astype(BF16)
            elif c0 < SSD_PAD + BC_WIDTH:
                cc = c0 - SSD_PAD
                bt_ref[0, ci, cc:cc + LANE, :] = val.T.astype(BF16)
            else:
                cc = c0 - SSD_PAD - BC_WIDTH
                cm_ref[0, rows, cc:cc + LANE] = val.astype(BF16)


def _inproj_call(xin, mod, g, w, wdt, cw, cb, wc, *, ctx, tm, row_len):
    bsz, seq, _ = xin.shape
    nt = seq // tm
    const2 = lambda b, j: (0, 0)
    tile3 = lambda b, j: (b, j, 0)
    in_specs = [
        pl.BlockSpec((1, tm, D_MODEL), tile3),
        pl.BlockSpec((MOD_ROWS, N_MOD * D_MODEL), const2),
        pl.BlockSpec((1, D_MODEL), const2),
        pl.BlockSpec((D_MODEL, W_COLS), const2),
        pl.BlockSpec((DT_ROWS, D_MODEL), const2),
        pl.BlockSpec((D_CONV, XBC_PAD), const2),
        pl.BlockSpec((1, XBC_PAD), const2),
    ]
    args = [xin, mod, g, w, wdt, cw, cb]
    out_specs = []
    out_shape = []
    if not ctx:
        in_specs.append(pl.BlockSpec((F_WIDTH, 2 * F_WIDTH), const2))
        args.append(wc)
        out_specs += [pl.BlockSpec((1, tm, 2 * F_WIDTH), tile3),
                      pl.BlockSpec((1, tm, SSD_PAD), tile3)]
        out_shape += [jax.ShapeDtypeStruct((bsz, seq, 2 * F_WIDTH), BF16),
                      jax.ShapeDtypeStruct((bsz, seq, SSD_PAD), BF16)]
    out_specs += [pl.BlockSpec((1, tm, SSD_PAD), tile3),
                  pl.BlockSpec((1, tm // CHUNK, BC_WIDTH, CHUNK), lambda b, j: (b, j, 0, 0)),
                  pl.BlockSpec((1, tm, BC_WIDTH), tile3),
                  pl.BlockSpec((1, DT_ROWS, tm), lambda b, j: (b, 0, j))]
    out_shape += [jax.ShapeDtypeStruct((bsz, seq, SSD_PAD), BF16),
                  jax.ShapeDtypeStruct((bsz, seq // CHUNK, BC_WIDTH, CHUNK), BF16),
                  jax.ShapeDtypeStruct((bsz, seq, BC_WIDTH), BF16),
                  jax.ShapeDtypeStruct((bsz, DT_ROWS, seq), F32)]
    pad_rows = 2 * SUBLANE + _conv_pieces(tm, row_len) * CONV_PIECE
    conv_scratch = pltpu.VMEM((XBC_PAD // LANE, pad_rows, LANE), F32)
    return pl.pallas_call(
        functools.partial(_inproj_kernel, ctx=ctx, tm=tm, row_len=row_len),
        grid=(bsz, nt),
        in_specs=in_specs,
        out_specs=out_specs,
        out_shape=out_shape,
        scratch_shapes=[conv_scratch, conv_scratch],
        compiler_params=pltpu.CompilerParams(
            dimension_semantics=("arbitrary", "arbitrary"),
            vmem_limit_bytes=VMEM_LIMIT),
        name="inproj_ctx" if ctx else "inproj_lat",
    )(*args)


def _ssd_kernel(xs_ref, bt_ref, cm_ref, z_ref, dt_ref,
                xsc_ref, btc_ref, cmc_ref, dtc_ref,
                dtb_ref, alog_ref, dskip_ref, gssd_ref,
                out_ref, h_ref, y_ref, r2_ref, w2_ref, ee_ref, qt_ref, *, seq, ctx_len):
    nchunk = seq // CHUNK
    nchunk_ctx = ctx_len // CHUNK

    h_ref[...] = jnp.zeros(h_ref.shape, F32)

    bias = dtb_ref[...]
    nega = -jnp.exp(alog_ref[...])
    sub_i = lax.broadcasted_iota(jnp.int32, (CHUNK, CHUNK), 0)
    lane_i = lax.broadcasted_iota(jnp.int32, (CHUNK, CHUNK), 1)
    tri = (lane_i <= sub_i, lane_i >= sub_i)
    lane_lo = lane_i < HEAD_DIM
    lane_row = lax.broadcasted_iota(jnp.int32, (1, CHUNK), 1)

    def colb(mat_t, idx):
        return jnp.broadcast_to(mat_t[:, idx:idx + 1], (CHUNK, CHUNK))

    def prepare(d, raws):
        r0 = d * DT_DIR_ROWS
        n = len(raws)
        v = jnp.concatenate(raws, axis=0) + jnp.concatenate([bias[r0:r0 + DT_DIR_ROWS]] * n, axis=0)
        dt = jnp.maximum(v, 0.0) + jnp.log1p(jnp.exp(-jnp.abs(v)))
        da = dt * jnp.concatenate([nega[r0:r0 + DT_DIR_ROWS]] * n, axis=0)
        lane_n = lax.broadcasted_iota(jnp.int32, da.shape, 1)
        cs = da
        sh = 1
        while sh < CHUNK:
            if d == 0:
                cs = cs + jnp.where(lane_n >= sh, pltpu.roll(cs, sh, axis=1), 0.0)
            else:
                cs = cs + jnp.where(lane_n < CHUNK - sh, pltpu.roll(cs, CHUNK - sh, axis=1), 0.0)
            sh *= 2
        a_end = jnp.sum(da, axis=1, keepdims=True)
        cs2 = cs * LOG2E
        r2 = cs2 - jnp.log2(dt)
        w2 = jnp.exp2(a_end * LOG2E - r2)
        ee = jnp.broadcast_to(jnp.exp(a_end), da.shape)
        return cs2, r2, w2, ee

    def store_prep(d, k0, raws):
        cs2, r2, w2, ee = prepare(d, raws)
        for i in range(len(raws)):
            rows = slice(i * DT_DIR_ROWS, (i + 1) * DT_DIR_ROWS)
            r2_ref[d, k0 + i] = r2[rows]
            w2_ref[d, k0 + i] = w2[rows]
            ee_ref[d, k0 + i] = ee[rows]
            qt_ref[d, k0 + i] = jnp.concatenate(
                [cs2[rows], jnp.zeros((CHUNK - DT_DIR_ROWS, CHUNK), F32)], axis=0).T

    def direction(d, x, btc, cc, k, row0, mode):
        with_output = mode is not None
        r2 = r2_ref[d, k]
        w2 = w2_ref[d, k]
        ee = ee_ref[d, k]
        q_t = qt_ref[d, k]

        for g in range(SSD_GROUPS):
            cg = cc[:, g * D_STATE:(g + 1) * D_STATE]
            bt = btc[g * D_STATE:(g + 1) * D_STATE, :]
            hg = h_ref[d, g]
            xg = x[:, g * GROUP_PAD:(g + 1) * GROUP_PAD]
            t0 = xg[:, 0:CHUNK]
            t1 = xg[:, CHUNK:2 * CHUNK]
            heads = [g * HEADS_PER_GROUP + r for r in range(HEADS_PER_GROUP)]
            if with_output:
                scores = _dot(cg, bt)
                hb = hg.astype(BF16)
                lhs = []
                for hd in heads:
                    col = colb(q_t, hd)
                    seg = col - r2[hd:hd + 1, :]
                    m = (scores * jnp.exp2(jnp.where(tri[d], seg, -jnp.inf))).astype(BF16)
                    ce = cg * jnp.exp2(col).astype(BF16)
                    lhs.append(jnp.concatenate([m, ce], axis=1))
                rhs0 = jnp.concatenate([t0, hb[:, 0:CHUNK]], axis=0)
                rhs1 = jnp.concatenate([t1, hb[:, CHUNK:2 * CHUNK]], axis=0)
                y01 = _dot(jnp.concatenate([lhs[0], lhs[1]], axis=0), rhs0)
                y0 = jnp.where(lane_lo, y01[0:CHUNK], y01[CHUNK:2 * CHUNK])
                y1 = _dot(lhs[2], rhs1)
                y_g = jnp.concatenate([y0, y1], axis=1)
                if mode:
                    y_g = y_g + y_ref[pl.ds(row0, CHUNK), g * GROUP_PAD:(g + 1) * GROUP_PAD]
                y_ref[pl.ds(row0, CHUNK), g * GROUP_PAD:(g + 1) * GROUP_PAD] = y_g
            btf = bt.astype(F32)
            btw = [(btf * w2[hd:hd + 1, :]).astype(BF16) for hd in heads]
            s01 = _dot(jnp.concatenate([btw[0], btw[1]], axis=0), t0)
            s0 = jnp.where(lane_lo, s01[0:D_STATE], s01[D_STATE:2 * D_STATE])
            s1 = _dot(btw[2], t1)
            er = [ee[hd:hd + 1, :] for hd in heads]
            e3 = jnp.concatenate([jnp.where(lane_row < HEAD_DIM, er[0], er[1]), er[2]], axis=1)
            h_ref[d, g] = e3 * hg + jnp.concatenate([s0, s1], axis=1)

    for d in range(2):
        rows_d = slice(d * DT_DIR_ROWS, (d + 1) * DT_DIR_ROWS)
        store_prep(d, 0, [dtc_ref[0, rows_d, c * CHUNK:(c + 1) * CHUNK] for c in range(nchunk_ctx)])
        store_prep(d, nchunk_ctx, [dt_ref[0, rows_d, c * CHUNK:(c + 1) * CHUNK] for c in range(nchunk)])

    for i in range(nchunk_ctx):
        for d in range(2):
            ci = i if d == 0 else nchunk_ctx - 1 - i
            rows = slice(ci * CHUNK, (ci + 1) * CHUNK)
            direction(d, xsc_ref[0, rows, :], btc_ref[0, ci], cmc_ref[0, rows, :], ci, 0, None)

    def step(i, carry, *, accumulate):
        for d in range(2):
            ci = i if d == 0 else nchunk - 1 - i
            row0 = pl.multiple_of(ci * CHUNK, CHUNK)
            direction(d, xs_ref[0, pl.ds(row0, CHUNK), :], bt_ref[0, ci],
                      cm_ref[0, pl.ds(row0, CHUNK), :], nchunk_ctx + ci, row0, accumulate)
        return carry

    half = nchunk // 2
    lax.fori_loop(0, half, functools.partial(step, accumulate=False), 0, unroll=2)
    lax.fori_loop(half, nchunk, functools.partial(step, accumulate=True), 0, unroll=2)

    te = 256

    def epi(i, carry):
        row0 = pl.multiple_of(i * te, te)
        xr = xs_ref[0, pl.ds(row0, te), :].astype(F32)
        zr = z_ref[0, pl.ds(row0, te), :].astype(F32)
        y = y_ref[pl.ds(row0, te), :] + xr * dskip_ref[...]
        y = y * _silu(zr)
        out_ref[0, pl.ds(row0, te), :] = _rms(y, gssd_ref[...], n=SSD_WIDTH).astype(BF16)
        return carry

    lax.fori_loop(0, seq // te, epi, 0)


def _ssd_call(xs, bm, cm, z, dt, xsc, bmc, cmc, dtc, dtb, alog, dskip, gssd):
    bsz, seq, _ = xs.shape
    ctx_len = xsc.shape[1]
    nck = (seq + ctx_len) // CHUNK
    per_b = lambda b: (b, 0, 0)
    const2 = lambda b: (0, 0)
    return pl.pallas_call(
        functools.partial(_ssd_kernel, seq=seq, ctx_len=ctx_len),
        grid=(bsz,),
        in_specs=[
            pl.BlockSpec((1, seq, SSD_PAD), per_b),
            pl.BlockSpec((1, seq // CHUNK, BC_WIDTH, CHUNK), lambda b: (b, 0, 0, 0)),
            pl.BlockSpec((1, seq, BC_WIDTH), per_b),
            pl.BlockSpec((1, seq, SSD_PAD), per_b),
            pl.BlockSpec((1, DT_ROWS, seq), per_b),
            pl.BlockSpec((1, ctx_len, SSD_PAD), per_b),
            pl.BlockSpec((1, ctx_len // CHUNK, BC_WIDTH, CHUNK), lambda b: (b, 0, 0, 0)),
            pl.BlockSpec((1, ctx_len, BC_WIDTH), per_b),
            pl.BlockSpec((1, DT_ROWS, ctx_len), per_b),
            pl.BlockSpec((DT_ROWS, 1), const2),
            pl.BlockSpec((DT_ROWS, 1), const2),
            pl.BlockSpec((1, SSD_PAD), const2),
            pl.BlockSpec((1, SSD_PAD), const2),
        ],
        out_specs=pl.BlockSpec((1, seq, SSD_PAD), per_b),
        out_shape=jax.ShapeDtypeStruct((bsz, seq, SSD_PAD), BF16),
        scratch_shapes=[
            pltpu.VMEM((2, SSD_GROUPS, D_STATE, GROUP_PAD), F32),
            pltpu.VMEM((seq, SSD_PAD), F32),
            pltpu.VMEM((2, nck, DT_DIR_ROWS, CHUNK), F32),
            pltpu.VMEM((2, nck, DT_DIR_ROWS, CHUNK), F32),
            pltpu.VMEM((2, nck, DT_DIR_ROWS, CHUNK), F32),
            pltpu.VMEM((2, nck, CHUNK, CHUNK), F32),
        ],
        compiler_params=pltpu.CompilerParams(
            dimension_semantics=("arbitrary",), vmem_limit_bytes=VMEM_LIMIT),
        name="ssd_scan",
    )(xs, bm, cm, z, dt, xsc, bmc, cmc, dtc, dtb, alog, dskip, gssd)


def _out_ffn_kernel(x_ref, yn_ref, uu_ref, cl_ref, sl_ref, mod_ref,
                    wof_ref, woy_ref, gpm_ref, gpf_ref, gpo_ref,
                    wg_ref, wu_ref, wd_ref, out_ref):
    b = pl.program_id(1)
    yf = (_dot(cl_ref[...].astype(BF16), uu_ref[0, :, 0:F_WIDTH])
          + _dot(sl_ref[...].astype(BF16), uu_ref[0, :, F_WIDTH:2 * F_WIDTH]))
    mix = _dot(yf.astype(BF16), wof_ref[...]) + _dot(yn_ref[0], woy_ref[...])

    def mod(k):
        return mod_ref[pl.ds(b, 1), k * D_MODEL:(k + 1) * D_MODEL]

    x1 = x_ref[0] + mod(2) * _rms(mix, gpm_ref[...])
    h2 = (_rms(x1, gpf_ref[...]) * (1.0 + mod(4)) + mod(3)).astype(BF16)
    gate = _dot(h2, wg_ref[...])
    up = _dot(h2, wu_ref[...])
    act = (_silu(gate) * up).astype(BF16)
    ffn = _dot(act, wd_ref[...])
    out_ref[0] = x1 + mod(5) * _rms(ffn, gpo_ref[...])


def _out_ffn_call(x, yn, uu, cl, sl, mod, wof, woy, gpm, gpf, gpo, wg, wu, wd, *, tm):
    bsz, seq, _ = x.shape
    d_ff = wg.shape[1]
    nt = seq // tm
    const2 = lambda j, b: (0, 0)
    tile3 = lambda j, b: (b, j, 0)
    single = dict(pipeline_mode=pl.Buffered(1))
    return pl.pallas_call(
        _out_ffn_kernel,
        grid=(nt, bsz),
        in_specs=[
            pl.BlockSpec((1, tm, D_MODEL), tile3),
            pl.BlockSpec((1, tm, SSD_PAD), tile3),
            pl.BlockSpec((1, seq, 2 * F_WIDTH), lambda j, b: (b, 0, 0)),
            pl.BlockSpec((tm, seq), lambda j, b: (j, 0)),
            pl.BlockSpec((tm, seq), lambda j, b: (j, 0)),
            pl.BlockSpec((MOD_ROWS, N_MOD * D_MODEL), const2),
            pl.BlockSpec((F_WIDTH, D_MODEL), const2, **single),
            pl.BlockSpec((SSD_PAD, D_MODEL), const2, **single),
            pl.BlockSpec((1, D_MODEL), const2),
            pl.BlockSpec((1, D_MODEL), const2),
            pl.BlockSpec((1, D_MODEL), const2),
            pl.BlockSpec((D_MODEL, d_ff), const2, **single),
            pl.BlockSpec((D_MODEL, d_ff), const2, **single),
            pl.BlockSpec((d_ff, D_MODEL), const2, **single),
        ],
        out_specs=pl.BlockSpec((1, tm, D_MODEL), tile3),
        out_shape=jax.ShapeDtypeStruct((bsz, seq, D_MODEL), F32),
        compiler_params=pltpu.CompilerParams(
            dimension_semantics=("arbitrary", "arbitrary"),
            vmem_limit_bytes=VMEM_LIMIT),
        name="out_ffn",
    )(x, yn, uu, cl, sl, mod, wof, woy, gpm, gpf, gpo, wg, wu, wd)


@functools.lru_cache(maxsize=None)
def _dft_tables(seq):
    k = np.arange(seq, dtype=np.int64)
    ang = 2.0 * np.pi * ((k[:, None] * k[None, :]) % seq).astype(np.float64) / seq
    scale = 1.0 / math.sqrt(seq)
    cl = (np.cos(ang) * scale).astype(np.float32)
    sl = (np.sin(ang) * scale).astype(np.float32)
    c = np.arange(FGROUP_DIM, dtype=np.int64)
    angc = 2.0 * np.pi * ((c[:, None] * c[None, :]) % FGROUP_DIM).astype(np.float64) / FGROUP_DIM
    sc = 1.0 / math.sqrt(FGROUP_DIM)
    wc = np.zeros((F_WIDTH, 2 * F_WIDTH), np.float32)
    for g in range(N_FGROUPS):
        s = slice(g * FGROUP_DIM, (g + 1) * FGROUP_DIM)
        wc[s, g * FGROUP_DIM:(g + 1) * FGROUP_DIM] = np.cos(angc) * sc
        wc[s, F_WIDTH + g * FGROUP_DIM:F_WIDTH + (g + 1) * FGROUP_DIM] = -np.sin(angc) * sc
    return cl, sl, wc


def _pad_groups(a, axis):
    a = jnp.moveaxis(a, axis, -1)
    lead = a.shape[:-1]
    a = a.reshape(lead + (SSD_GROUPS, HEADS_PER_GROUP * HEAD_DIM))
    a = jnp.pad(a, [(0, 0)] * len(lead) + [(0, 0), (0, GROUP_PAD - HEADS_PER_GROUP * HEAD_DIM)])
    a = a.reshape(lead + (SSD_PAD,))
    return jnp.moveaxis(a, -1, axis)


def _pad_dirs(a):
    a = jnp.pad(a, ((0, 0), (0, DT_DIR_ROWS - SSD_HEADS)))
    return a.reshape(DT_ROWS, 1)


def kernel(x, c, ctx, c_ctx, w_ada, b_ada, g_pre_mix, g_post_mix, g_pre_ffn, g_post_ffn,
           w_in, conv_w, conv_b, dt_bias, a_log, d_skip, g_ssd, w_out, w_gate, w_up, w_down):
    bsz, seq, _ = x.shape
    ctx_len = ctx.shape[1]
    l = 0
    cl_np, sl_np, wc_np = _dft_tables(seq)
    cl = jnp.asarray(cl_np)
    sl = jnp.asarray(sl_np)
    wc = jnp.asarray(wc_np)

    cc = jnp.concatenate(
        [c, c_ctx[None, :], jnp.zeros((MOD_ROWS - bsz - 1, D_MODEL), F32)], axis=0)
    mod = _ada_call(cc, w_ada, b_ada[l][None, :], l)

    wi = w_in[l]
    w_cat = jnp.concatenate([
        wi[:, :F_WIDTH],
        _pad_groups(wi[:, F_WIDTH:XBC_OFF], 1),
        _pad_groups(wi[:, XBC_OFF:XBC_OFF + SSD_WIDTH], 1),
        wi[:, XBC_OFF + SSD_WIDTH:XBC_OFF + CONV_DIM],
    ], axis=1).astype(BF16)
    w_dt = wi[:, XBC_OFF + CONV_DIM:].T.reshape(2, SSD_HEADS, D_MODEL)
    w_dt = jnp.pad(w_dt, ((0, 0), (0, DT_DIR_ROWS - SSD_HEADS), (0, 0))).reshape(DT_ROWS, D_MODEL)
    w_dt = w_dt.astype(BF16)
    cw = conv_w[l].T
    cw = jnp.concatenate([_pad_groups(cw[:, :SSD_WIDTH], 1), cw[:, SSD_WIDTH:]], axis=1)
    cb = conv_b[l][None, :]
    cb = jnp.concatenate([_pad_groups(cb[:, :SSD_WIDTH], 1), cb[:, SSD_WIDTH:]], axis=1)
    g_pre = g_pre_mix[l][None, :]

    xs_c, bm_c, cm_c, dt_c = _inproj_call(
        ctx, mod, g_pre, w_cat, w_dt, cw, cb, None, ctx=True, tm=ctx_len, row_len=ctx_len)
    uu, z, xs, bm, cm, dt = _inproj_call(
        x, mod, g_pre, w_cat, w_dt, cw, cb, wc, ctx=False, tm=512, row_len=GRID_W)

    dskip = _pad_groups(jnp.repeat(d_skip[l], HEAD_DIM)[None, :], 1)
    gssd = _pad_groups(g_ssd[l][None, :], 1)
    yn = _ssd_call(xs, bm, cm, z, dt, xs_c, bm_c, cm_c, dt_c,
                   _pad_dirs(dt_bias[l]), _pad_dirs(a_log[l]), dskip, gssd)

    wo = w_out[l]
    wof = wo[:F_WIDTH].astype(BF16)
    woy = _pad_groups(wo[F_WIDTH:], 0).astype(BF16)
    return _out_ffn_call(
        x, yn, uu, cl, sl, mod, wof, woy,
        g_post_mix[l][None, :], g_pre_ffn[l][None, :], g_post_ffn[l][None, :],
        w_gate[l].astype(BF16), w_up[l].astype(BF16), w_down[l].astype(BF16), tm=256)
```

```python
import functools
import math

import jax
import jax.numpy as jnp
import numpy as np
from jax import lax
from jax.experimental import pallas as pl
from jax.experimental.pallas import tpu as pltpu

F32 = jnp.float32
BF16 = jnp.bfloat16

D_MODEL = 1024
GRID_W = 64
F_WIDTH = 256
N_FGROUPS = 4
FGROUP_DIM = F_WIDTH // N_FGROUPS
SSD_WIDTH = 768
HEAD_DIM = 64
SSD_HEADS = 12
SSD_GROUPS = 4
HEADS_PER_GROUP = 3
D_STATE = 128
BC_WIDTH = SSD_GROUPS * D_STATE
D_CONV = 7
HALF_CONV = D_CONV // 2
CHUNK = 128
XBC_OFF = F_WIDTH + SSD_WIDTH
CONV_DIM = SSD_WIDTH + 2 * BC_WIDTH
N_MOD = 6
EPS = 1e-6
LOG2E = math.log2(math.e)

GROUP_PAD = 256
SSD_PAD = SSD_GROUPS * GROUP_PAD
XBC_PAD = SSD_PAD + 2 * BC_WIDTH
W_COLS = F_WIDTH + SSD_PAD + XBC_PAD
CTX_COL0 = F_WIDTH + SSD_PAD
DT_ROWS = 32
DT_DIR_ROWS = 16
MOD_ROWS = 16
MOD_CTX_ROW = 8

SUBLANE = 8
LANE = 128
VMEM_LIMIT = 56 * 1024 * 1024


def _dot(a, b):
    return jnp.dot(a, b, preferred_element_type=F32)


def _silu(v):
    return v * jax.nn.sigmoid(v)


def _rms(v, g, n=None):
    n = v.shape[-1] if n is None else n
    ms = jnp.sum(v * v, axis=-1, keepdims=True) * (1.0 / n)
    return v * lax.rsqrt(ms + EPS) * g


def _ada_kernel(c_ref, w_ref, b_ref, o_ref):
    s = _silu(c_ref[...]).astype(BF16)
    o_ref[...] = _dot(s, w_ref[...].astype(BF16)) + b_ref[...]


def _ada_call(cc, w_ada, b_ada, layer):
    n = w_ada.shape[2]
    tn = 768
    return pl.pallas_call(
        _ada_kernel,
        grid=(n // tn,),
        in_specs=[
            pl.BlockSpec((MOD_ROWS, D_MODEL), lambda j: (0, 0)),
            pl.BlockSpec((None, D_MODEL, tn), lambda j: (layer, 0, j)),
            pl.BlockSpec((1, tn), lambda j: (0, j)),
        ],
        out_specs=pl.BlockSpec((MOD_ROWS, tn), lambda j: (0, j)),
        out_shape=jax.ShapeDtypeStruct((MOD_ROWS, n), F32),
        compiler_params=pltpu.CompilerParams(dimension_semantics=("arbitrary",)),
        name="ada_mod",
    )(cc, w_ada, b_ada)


CONV_STRIDE = SUBLANE + 1
CONV_PIECE = SUBLANE * CONV_STRIDE


def _conv_pieces(tm, row_len):
    padded = (tm // row_len) * (row_len + SUBLANE)
    return -(-padded // CONV_PIECE)


def _inproj_kernel(*refs, ctx, tm, row_len):
    if ctx:
        (x_ref, mod_ref, g_ref, w_ref, wdt_ref, cw_ref, cb_ref,
         xs_ref, bt_ref, cm_ref, dt_ref, pad_ref, cv_ref) = refs
    else:
        (x_ref, mod_ref, g_ref, w_ref, wdt_ref, cw_ref, cb_ref, wc_ref,
         uu_ref, z_ref, xs_ref, bt_ref, cm_ref, dt_ref, pad_ref, cv_ref) = refs

    xt = x_ref[0]
    if ctx:
        shift = mod_ref[MOD_CTX_ROW:MOD_CTX_ROW + 1, 0:D_MODEL]
        scale = mod_ref[MOD_CTX_ROW:MOD_CTX_ROW + 1, D_MODEL:2 * D_MODEL]
    else:
        b = pl.program_id(0)
        shift = mod_ref[pl.ds(b, 1), 0:D_MODEL]
        scale = mod_ref[pl.ds(b, 1), D_MODEL:2 * D_MODEL]
    h = _rms(xt, g_ref[...]) * (1.0 + scale) + shift
    hb = h.astype(BF16)

    if not ctx:
        uf = _dot(hb, w_ref[:, 0:F_WIDTH])
        uu_ref[0] = _dot(uf.astype(BF16), wc_ref[...].astype(BF16)).astype(BF16)
        z_ref[0] = _dot(hb, w_ref[:, F_WIDTH:CTX_COL0]).astype(BF16)
        xbc = _dot(hb, w_ref[:, CTX_COL0:W_COLS])
    else:
        xbc = _dot(hb, w_ref[:, CTX_COL0:W_COLS])
    dt_ref[0] = lax.dot_general(wdt_ref[...], hb, (((1,), (1,)), ((), ())),
                                preferred_element_type=F32)

    pitch = row_len + SUBLANE
    nrow = tm // row_len
    npiece = _conv_pieces(tm, row_len)
    data_end = SUBLANE + nrow * pitch
    alloc_end = pad_ref.shape[1]
    zrow = jnp.zeros((SUBLANE, LANE), F32)
    for j in range(XBC_PAD // LANE):
        pad_ref[j, 0:SUBLANE, :] = zrow
        for r in range(nrow):
            base = SUBLANE + r * pitch
            pad_ref[j, base:base + row_len, :] = xbc[r * row_len:(r + 1) * row_len,
                                                     j * LANE:(j + 1) * LANE]
            pad_ref[j, base + row_len:base + pitch, :] = zrow
        for z0 in range(data_end, alloc_end, SUBLANE):
            pad_ref[j, z0:z0 + SUBLANE, :] = zrow

    for j in range(XBC_PAD // LANE):
        wk = [jnp.broadcast_to(cw_ref[k:k + 1, j * LANE:(j + 1) * LANE], (SUBLANE, LANE))
              for k in range(D_CONV)]
        bias = jnp.broadcast_to(cb_ref[:, j * LANE:(j + 1) * LANE], (SUBLANE, LANE))
        for p in range(npiece):
            base = SUBLANE + p * CONV_PIECE
            wins = {v: pad_ref[j, pl.ds(base + v, SUBLANE, stride=CONV_STRIDE), :]
                    for v in range(-HALF_CONV, CONV_STRIDE + HALF_CONV)}
            for a in range(CONV_STRIDE):
                acc = bias
                for k in range(D_CONV):
                    acc = acc + wins[a + k - HALF_CONV] * wk[k]
                cv_ref[j, pl.ds(base + a, SUBLANE, stride=CONV_STRIDE), :] = acc

    def conv_rows(j, t0, n):
        parts = []
        for t in range(t0, t0 + n, min(n, row_len)):
            src = SUBLANE + (t // row_len) * pitch + t % row_len
            parts.append(cv_ref[j, src:src + min(n, row_len), :])
        return _silu(parts[0] if len(parts) == 1 else jnp.concatenate(parts, axis=0))

    for j in range(XBC_PAD // LANE):
        c0 = j * LANE
        for ci in range(tm // CHUNK):
            val = conv_rows(j, ci * CHUNK, CHUNK)
            rows = slice(ci * CHUNK, (ci + 1) * CHUNK)
            if c0 < SSD_PAD:
                xs_ref[0, rows, c0:c0 + LANE] = val.astype(BF16)
            elif c0 < SSD_PAD + BC_WIDTH:
                cc = c0 - SSD_PAD
                bt_ref[0, ci, cc:cc + LANE, :] = val.T.astype(BF16)
            else:
                cc = c0 - SSD_PAD - BC_WIDTH
                cm_ref[0, rows, cc:cc + LANE] = val.astype(BF16)


def _inproj_call(xin, mod, g, w, wdt, cw, cb, wc, *, ctx, tm, row_len):
    bsz, seq, _ = xin.shape
    nt = seq // tm
    const2 = lambda b, j: (0, 0)
    tile3 = lambda b, j: (b, j, 0)
    in_specs = [
        pl.BlockSpec((1, tm, D_MODEL), tile3),
        pl.BlockSpec((MOD_ROWS, N_MOD * D_MODEL), const2),
        pl.BlockSpec((1, D_MODEL), const2),
        pl.BlockSpec((D_MODEL, W_COLS), const2),
        pl.BlockSpec((DT_ROWS, D_MODEL), const2),
        pl.BlockSpec((D_CONV, XBC_PAD), const2),
        pl.BlockSpec((1, XBC_PAD), const2),
    ]
    args = [xin, mod, g, w, wdt, cw, cb]
    out_specs = []
    out_shape = []
    if not ctx:
        in_specs.append(pl.BlockSpec((F_WIDTH, 2 * F_WIDTH), const2))
        args.append(wc)
        out_specs += [pl.BlockSpec((1, tm, 2 * F_WIDTH), tile3),
                      pl.BlockSpec((1, tm, SSD_PAD), tile3)]
        out_shape += [jax.ShapeDtypeStruct((bsz, seq, 2 * F_WIDTH), BF16),
                      jax.ShapeDtypeStruct((bsz, seq, SSD_PAD), BF16)]
    out_specs += [pl.BlockSpec((1, tm, SSD_PAD), tile3),
                  pl.BlockSpec((1, tm // CHUNK, BC_WIDTH, CHUNK), lambda b, j: (b, j, 0, 0)),
                  pl.BlockSpec((1, tm, BC_WIDTH), tile3),
                  pl.BlockSpec((1, DT_ROWS, tm), lambda b, j: (b, 0, j))]
    out_shape += [jax.ShapeDtypeStruct((bsz, seq, SSD_PAD), BF16),
                  jax.ShapeDtypeStruct((bsz, seq // CHUNK, BC_WIDTH, CHUNK), BF16),
                  jax.ShapeDtypeStruct((bsz, seq, BC_WIDTH), BF16),
                  jax.ShapeDtypeStruct((bsz, DT_ROWS, seq), F32)]
    pad_rows = 2 * SUBLANE + _conv_pieces(tm, row_len) * CONV_PIECE
    conv_scratch = pltpu.VMEM((XBC_PAD // LANE, pad_rows, LANE), F32)
    return pl.pallas_call(
        functools.partial(_inproj_kernel, ctx=ctx, tm=tm, row_len=row_len),
        grid=(bsz, nt),
        in_specs=in_specs,
        out_specs=out_specs,
        out_shape=out_shape,
        scratch_shapes=[conv_scratch, conv_scratch],
        compiler_params=pltpu.CompilerParams(
            dimension_semantics=("arbitrary", "arbitrary"),
            vmem_limit_bytes=VMEM_LIMIT),
        name="inproj_ctx" if ctx else "inproj_lat",
    )(*args)


def _ssd_kernel(xs_ref, bt_ref, cm_ref, z_ref, dt_ref,
                xsc_ref, btc_ref, cmc_ref, dtc_ref,
                dtb_ref, alog_ref, dskip_ref, gssd_ref,
                out_ref, h_ref, y_ref, r2_ref, w2_ref, ee_ref, qt_ref, *, seq, ctx_len):
    nchunk = seq // CHUNK
    nchunk_ctx = ctx_len // CHUNK

    h_ref[...] = jnp.zeros(h_ref.shape, F32)

    bias = dtb_ref[...]
    nega = -jnp.exp(alog_ref[...])
    sub_i = lax.broadcasted_iota(jnp.int32, (CHUNK, CHUNK), 0)
    lane_i = lax.broadcasted_iota(jnp.int32, (CHUNK, CHUNK), 1)
    tri = (lane_i <= sub_i, lane_i >= sub_i)
    lane_lo = lane_i < HEAD_DIM
    lane_row = lax.broadcasted_iota(jnp.int32, (1, CHUNK), 1)

    def colb(mat_t, idx):
        return jnp.broadcast_to(mat_t[:, idx:idx + 1], (CHUNK, CHUNK))

    def prepare(d, raws):
        r0 = d * DT_DIR_ROWS
        n = len(raws)
        v = jnp.concatenate(raws, axis=0) + jnp.concatenate([bias[r0:r0 + DT_DIR_ROWS]] * n, axis=0)
        dt = jnp.maximum(v, 0.0) + jnp.log1p(jnp.exp(-jnp.abs(v)))
        da = dt * jnp.concatenate([nega[r0:r0 + DT_DIR_ROWS]] * n, axis=0)
        lane_n = lax.broadcasted_iota(jnp.int32, da.shape, 1)
        cs = da
        sh = 1
        while sh < CHUNK:
            if d == 0:
                cs = cs + jnp.where(lane_n >= sh, pltpu.roll(cs, sh, axis=1), 0.0)
            else:
                cs = cs + jnp.where(lane_n < CHUNK - sh, pltpu.roll(cs, CHUNK - sh, axis=1), 0.0)
            sh *= 2
        a_end = jnp.sum(da, axis=1, keepdims=True)
        cs2 = cs * LOG2E
        r2 = cs2 - jnp.log2(dt)
        w2 = jnp.exp2(a_end * LOG2E - r2)
        ee = jnp.broadcast_to(jnp.exp(a_end), da.shape)
        return cs2, r2, w2, ee

    def store_prep(d, k0, raws):
        cs2, r2, w2, ee = prepare(d, raws)
        for i in range(len(raws)):
            rows = slice(i * DT_DIR_ROWS, (i + 1) * DT_DIR_ROWS)
            r2_ref[d, k0 + i] = r2[rows]
            w2_ref[d, k0 + i] = w2[rows]
            ee_ref[d, k0 + i] = ee[rows]
            qt_ref[d, k0 + i] = jnp.concatenate(
                [cs2[rows], jnp.zeros((CHUNK - DT_DIR_ROWS, CHUNK), F32)], axis=0).T

    def direction(d, x, btc, cc, k, row0, mode):
        with_output = mode is not None
        r2 = r2_ref[d, k]
        w2 = w2_ref[d, k]
        ee = ee_ref[d, k]
        q_t = qt_ref[d, k]

        for g in range(SSD_GROUPS):
            cg = cc[:, g * D_STATE:(g + 1) * D_STATE]
            bt = btc[g * D_STATE:(g + 1) * D_STATE, :]
            hg = h_ref[d, g]
            xg = x[:, g * GROUP_PAD:(g + 1) * GROUP_PAD]
            t0 = xg[:, 0:CHUNK]
            t1 = xg[:, CHUNK:2 * CHUNK]
            heads = [g * HEADS_PER_GROUP + r for r in range(HEADS_PER_GROUP)]
            if with_output:
                scores = _dot(cg, bt)
                hb = hg.astype(BF16)
                lhs = []
                for hd in heads:
                    col = colb(q_t, hd)
                    seg = col - r2[hd:hd + 1, :]
                    m = (scores * jnp.exp2(jnp.where(tri[d], seg, -jnp.inf))).astype(BF16)
                    ce = cg * jnp.exp2(col).astype(BF16)
                    lhs.append(jnp.concatenate([m, ce], axis=1))
                rhs0 = jnp.concatenate([t0, hb[:, 0:CHUNK]], axis=0)
                rhs1 = jnp.concatenate([t1, hb[:, CHUNK:2 * CHUNK]], axis=0)
                y01 = _dot(jnp.concatenate([lhs[0], lhs[1]], axis=0), rhs0)
                y0 = jnp.where(lane_lo, y01[0:CHUNK], y01[CHUNK:2 * CHUNK])
                y1 = _dot(lhs[2], rhs1)
                y_g = jnp.concatenate([y0, y1], axis=1)
                if mode:
                    y_g = y_g + y_ref[pl.ds(row0, CHUNK), g * GROUP_PAD:(g + 1) * GROUP_PAD]
                y_ref[pl.ds(row0, CHUNK), g * GROUP_PAD:(g + 1) * GROUP_PAD] = y_g
            btf = bt.astype(F32)
            btw = [(btf * w2[hd:hd + 1, :]).astype(BF16) for hd in heads]
            s01 = _dot(jnp.concatenate([btw[0], btw[1]], axis=0), t0)
            s0 = jnp.where(lane_lo, s01[0:D_STATE], s01[D_STATE:2 * D_STATE])
            s1 = _dot(btw[2], t1)
            er = [ee[hd:hd + 1, :] for hd in heads]
            e3 = jnp.concatenate([jnp.where(lane_row < HEAD_DIM, er[0], er[1]), er[2]], axis=1)
            h_ref[d, g] = e3 * hg + jnp.concatenate([s0, s1], axis=1)

    for d in range(2):
        rows_d = slice(d * DT_DIR_ROWS, (d + 1) * DT_DIR_ROWS)
        store_prep(d, 0, [dtc_ref[0, rows_d, c * CHUNK:(c + 1) * CHUNK] for c in range(nchunk_ctx)])
        store_prep(d, nchunk_ctx, [dt_ref[0, rows_d, c * CHUNK:(c + 1) * CHUNK] for c in range(nchunk)])

    for i in range(nchunk_ctx):
        for d in range(2):
            ci = i if d == 0 else nchunk_ctx - 1 - i
            rows = slice(ci * CHUNK, (ci + 1) * CHUNK)
            direction(d, xsc_ref[0, rows, :], btc_ref[0, ci], cmc_ref[0, rows, :], ci, 0, None)

    def step(i, carry, *, accumulate):
        for d in range(2):
            ci = i if d == 0 else nchunk - 1 - i
            row0 = pl.multiple_of(ci * CHUNK, CHUNK)
            direction(d, xs_ref[0, pl.ds(row0, CHUNK), :], bt_ref[0, ci],
                      cm_ref[0, pl.ds(row0, CHUNK), :], nchunk_ctx + ci, row0, accumulate)
        return carry

    half = nchunk // 2
    lax.fori_loop(0, half, functools.partial(step, accumulate=False), 0, unroll=2)
    lax.fori_loop(half, nchunk, functools.partial(step, accumulate=True), 0, unroll=2)

    te = 256

    def epi(i, carry):
        row0 = pl.multiple_of(i * te, te)
        xr = xs_ref[0, pl.ds(row0, te), :].astype(F32)
        zr = z_ref[0, pl.ds(row0, te), :].astype(F32)
        y = y_ref[pl.ds(row0, te), :] + xr * dskip_ref[...]
        y = y * _silu(zr)
        out_ref[0, pl.ds(row0, te), :] = _rms(y, gssd_ref[...], n=SSD_WIDTH).astype(BF16)
        return carry

    lax.fori_loop(0, seq // te, epi, 0)


def _ssd_call(xs, bm, cm, z, dt, xsc, bmc, cmc, dtc, dtb, alog, dskip, gssd):
    bsz, seq, _ = xs.shape
    ctx_len = xsc.shape[1]
    nck = (seq + ctx_len) // CHUNK
    per_b = lambda b: (b, 0, 0)
    const2 = lambda b: (0, 0)
    return pl.pallas_call(
        functools.partial(_ssd_kernel, seq=seq, ctx_len=ctx_len),
        grid=(bsz,),
        in_specs=[
            pl.BlockSpec((1, seq, SSD_PAD), per_b),
            pl.BlockSpec((1, seq // CHUNK, BC_WIDTH, CHUNK), lambda b: (b, 0, 0, 0)),
            pl.BlockSpec((1, seq, BC_WIDTH), per_b),
            pl.BlockSpec((1, seq, SSD_PAD), per_b),
            pl.BlockSpec((1, DT_ROWS, seq), per_b),
            pl.BlockSpec((1, ctx_len, SSD_PAD), per_b),
            pl.BlockSpec((1, ctx_len // CHUNK, BC_WIDTH, CHUNK), lambda b: (b, 0, 0, 0)),
            pl.BlockSpec((1, ctx_len, BC_WIDTH), per_b),
            pl.BlockSpec((1, DT_ROWS, ctx_len), per_b),
            pl.BlockSpec((DT_ROWS, 1), const2),
            pl.BlockSpec((DT_ROWS, 1), const2),
            pl.BlockSpec((1, SSD_PAD), const2),
            pl.BlockSpec((1, SSD_PAD), const2),
        ],
        out_specs=pl.BlockSpec((1, seq, SSD_PAD), per_b),
        out_shape=jax.ShapeDtypeStruct((bsz, seq, SSD_PAD), BF16),
        scratch_shapes=[
            pltpu.VMEM((2, SSD_GROUPS, D_STATE, GROUP_PAD), F32),
            pltpu.VMEM((seq, SSD_PAD), F32),
            pltpu.VMEM((2, nck, DT_DIR_ROWS, CHUNK), F32),
            pltpu.VMEM((2, nck, DT_DIR_ROWS, CHUNK), F32),
            pltpu.VMEM((2, nck, DT_DIR_ROWS, CHUNK), F32),
            pltpu.VMEM((2, nck, CHUNK, CHUNK), F32),
        ],
        compiler_params=pltpu.CompilerParams(
            dimension_semantics=("arbitrary",), vmem_limit_bytes=VMEM_LIMIT),
        name="ssd_scan",
    )(xs, bm, cm, z, dt, xsc, bmc, cmc, dtc, dtb, alog, dskip, gssd)


FFN_CHUNK = 256


def _out_ffn_kernel(x_ref, yn_ref, uu_ref, cl_ref, sl_ref, mod_ref,
                    wof_ref, woy_ref, gpm_ref, gpf_ref, gpo_ref,
                    wg_ref, wu_ref, wd_ref, out_ref):
    b = pl.program_id(1)
    yf = (_dot(cl_ref[...], uu_ref[0, :, 0:F_WIDTH])
          + _dot(sl_ref[...], uu_ref[0, :, F_WIDTH:2 * F_WIDTH]))
    mix = _dot(yf.astype(BF16), wof_ref[...]) + _dot(yn_ref[0], woy_ref[...])

    def mod(k):
        return mod_ref[pl.ds(b, 1), k * D_MODEL:(k + 1) * D_MODEL]

    x1 = x_ref[0] + mod(2) * _rms(mix, gpm_ref[...])
    h2 = (_rms(x1, gpf_ref[...]) * (1.0 + mod(4)) + mod(3)).astype(BF16)
    ffn = None
    for c0 in range(0, wg_ref.shape[1], FFN_CHUNK):
        gate = _dot(h2, wg_ref[:, c0:c0 + FFN_CHUNK])
        up = _dot(h2, wu_ref[:, c0:c0 + FFN_CHUNK])
        act = (_silu(gate) * up).astype(BF16)
        part = _dot(act, wd_ref[c0:c0 + FFN_CHUNK, :])
        ffn = part if ffn is None else ffn + part
    out_ref[0] = x1 + mod(5) * _rms(ffn, gpo_ref[...])


def _out_ffn_call(x, yn, uu, cl, sl, mod, wof, woy, gpm, gpf, gpo, wg, wu, wd, *, tm):
    bsz, seq, _ = x.shape
    d_ff = wg.shape[1]
    nt = seq // tm
    const2 = lambda j, b: (0, 0)
    tile3 = lambda j, b: (b, j, 0)
    single = dict(pipeline_mode=pl.Buffered(1))
    return pl.pallas_call(
        _out_ffn_kernel,
        grid=(nt, bsz),
        in_specs=[
            pl.BlockSpec((1, tm, D_MODEL), tile3),
            pl.BlockSpec((1, tm, SSD_PAD), tile3),
            pl.BlockSpec((1, seq, 2 * F_WIDTH), lambda j, b: (b, 0, 0)),
            pl.BlockSpec((tm, seq), lambda j, b: (j, 0)),
            pl.BlockSpec((tm, seq), lambda j, b: (j, 0)),
            pl.BlockSpec((MOD_ROWS, N_MOD * D_MODEL), const2),
            pl.BlockSpec((F_WIDTH, D_MODEL), const2, **single),
            pl.BlockSpec((SSD_PAD, D_MODEL), const2, **single),
            pl.BlockSpec((1, D_MODEL), const2),
            pl.BlockSpec((1, D_MODEL), const2),
            pl.BlockSpec((1, D_MODEL), const2),
            pl.BlockSpec((D_MODEL, d_ff), const2, **single),
            pl.BlockSpec((D_MODEL, d_ff), const2, **single),
            pl.BlockSpec((d_ff, D_MODEL), const2, **single),
        ],
        out_specs=pl.BlockSpec((1, tm, D_MODEL), tile3),
        out_shape=jax.ShapeDtypeStruct((bsz, seq, D_MODEL), F32),
        compiler_params=pltpu.CompilerParams(
            dimension_semantics=("arbitrary", "arbitrary"),
            vmem_limit_bytes=VMEM_LIMIT),
        name="out_ffn",
    )(x, yn, uu, cl, sl, mod, wof, woy, gpm, gpf, gpo, wg, wu, wd)


@functools.lru_cache(maxsize=None)
def _dft_tables(seq):
    k = np.arange(seq, dtype=np.int64)
    ang = 2.0 * np.pi * ((k[:, None] * k[None, :]) % seq).astype(np.float64) / seq
    scale = 1.0 / math.sqrt(seq)
    cl = (np.cos(ang) * scale).astype(np.float32)
    sl = (np.sin(ang) * scale).astype(np.float32)
    c = np.arange(FGROUP_DIM, dtype=np.int64)
    angc = 2.0 * np.pi * ((c[:, None] * c[None, :]) % FGROUP_DIM).astype(np.float64) / FGROUP_DIM
    sc = 1.0 / math.sqrt(FGROUP_DIM)
    wc = np.zeros((F_WIDTH, 2 * F_WIDTH), np.float32)
    for g in range(N_FGROUPS):
        s = slice(g * FGROUP_DIM, (g + 1) * FGROUP_DIM)
        wc[s, g * FGROUP_DIM:(g + 1) * FGROUP_DIM] = np.cos(angc) * sc
        wc[s, F_WIDTH + g * FGROUP_DIM:F_WIDTH + (g + 1) * FGROUP_DIM] = -np.sin(angc) * sc
    return cl, sl, wc


def _pad_groups(a, axis):
    a = jnp.moveaxis(a, axis, -1)
    lead = a.shape[:-1]
    a = a.reshape(lead + (SSD_GROUPS, HEADS_PER_GROUP * HEAD_DIM))
    a = jnp.pad(a, [(0, 0)] * len(lead) + [(0, 0), (0, GROUP_PAD - HEADS_PER_GROUP * HEAD_DIM)])
    a = a.reshape(lead + (SSD_PAD,))
    return jnp.moveaxis(a, -1, axis)


def _pad_dirs(a):
    a = jnp.pad(a, ((0, 0), (0, DT_DIR_ROWS - SSD_HEADS)))
    return a.reshape(DT_ROWS, 1)


def kernel(x, c, ctx, c_ctx, w_ada, b_ada, g_pre_mix, g_post_mix, g_pre_ffn, g_post_ffn,
           w_in, conv_w, conv_b, dt_bias, a_log, d_skip, g_ssd, w_out, w_gate, w_up, w_down):
    bsz, seq, _ = x.shape
    ctx_len = ctx.shape[1]
    l = 0
    cl_np, sl_np, wc_np = _dft_tables(seq)
    cl = jnp.asarray(cl_np).astype(BF16)
    sl = jnp.asarray(sl_np).astype(BF16)
    wc = jnp.asarray(wc_np)

    cc = jnp.concatenate(
        [c, c_ctx[None, :], jnp.zeros((MOD_ROWS - bsz - 1, D_MODEL), F32)], axis=0)
    mod = _ada_call(cc, w_ada, b_ada[l][None, :], l)

    wi = w_in[l]
    w_cat = jnp.concatenate([
        wi[:, :F_WIDTH],
        _pad_groups(wi[:, F_WIDTH:XBC_OFF], 1),
        _pad_groups(wi[:, XBC_OFF:XBC_OFF + SSD_WIDTH], 1),
        wi[:, XBC_OFF + SSD_WIDTH:XBC_OFF + CONV_DIM],
    ], axis=1).astype(BF16)
    w_dt = wi[:, XBC_OFF + CONV_DIM:].T.reshape(2, SSD_HEADS, D_MODEL)
    w_dt = jnp.pad(w_dt, ((0, 0), (0, DT_DIR_ROWS - SSD_HEADS), (0, 0))).reshape(DT_ROWS, D_MODEL)
    w_dt = w_dt.astype(BF16)
    cw = conv_w[l].T
    cw = jnp.concatenate([_pad_groups(cw[:, :SSD_WIDTH], 1), cw[:, SSD_WIDTH:]], axis=1)
    cb = conv_b[l][None, :]
    cb = jnp.concatenate([_pad_groups(cb[:, :SSD_WIDTH], 1), cb[:, SSD_WIDTH:]], axis=1)
    g_pre = g_pre_mix[l][None, :]

    xs_c, bm_c, cm_c, dt_c = _inproj_call(
        ctx, mod, g_pre, w_cat, w_dt, cw, cb, None, ctx=True, tm=ctx_len, row_len=ctx_len)
    uu, z, xs, bm, cm, dt = _inproj_call(
        x, mod, g_pre, w_cat, w_dt, cw, cb, wc, ctx=False, tm=512, row_len=GRID_W)

    dskip = _pad_groups(jnp.repeat(d_skip[l], HEAD_DIM)[None, :], 1)
    gssd = _pad_groups(g_ssd[l][None, :], 1)
    yn = _ssd_call(xs, bm, cm, z, dt, xs_c, bm_c, cm_c, dt_c,
                   _pad_dirs(dt_bias[l]), _pad_dirs(a_log[l]), dskip, gssd)

    wo = w_out[l]
    wof = wo[:F_WIDTH].astype(BF16)
    woy = _pad_groups(wo[F_WIDTH:], 0).astype(BF16)
    return _out_ffn_call(
        x, yn, uu, cl, sl, mod, wof, woy,
        g_post_mix[l][None, :], g_pre_ffn[l][None, :], g_post_ffn[l][None, :],
        w_gate[l].astype(BF16), w_up[l].astype(BF16), w_down[l].astype(BF16), tm=512)
```

```python
import functools
import math

import jax
import jax.numpy as jnp
import numpy as np
from jax import lax
from jax.experimental import pallas as pl
from jax.experimental.pallas import tpu as pltpu

F32 = jnp.float32
BF16 = jnp.bfloat16

D_MODEL = 1024
GRID_W = 64
F_WIDTH = 256
N_FGROUPS = 4
FGROUP_DIM = F_WIDTH // N_FGROUPS
SSD_WIDTH = 768
HEAD_DIM = 64
SSD_HEADS = 12
SSD_GROUPS = 4
HEADS_PER_GROUP = 3
D_STATE = 128
BC_WIDTH = SSD_GROUPS * D_STATE
D_CONV = 7
HALF_CONV = D_CONV // 2
CHUNK = 128
XBC_OFF = F_WIDTH + SSD_WIDTH
CONV_DIM = SSD_WIDTH + 2 * BC_WIDTH
N_MOD = 6
EPS = 1e-6
LOG2E = math.log2(math.e)

GROUP_PAD = 256
SSD_PAD = SSD_GROUPS * GROUP_PAD
XBC_PAD = SSD_PAD + 2 * BC_WIDTH
W_COLS = F_WIDTH + SSD_PAD + XBC_PAD
CTX_COL0 = F_WIDTH + SSD_PAD
DT_ROWS = 32
DT_DIR_ROWS = 16
MOD_ROWS = 16
MOD_CTX_ROW = 8

SUBLANE = 8
LANE = 128
VMEM_LIMIT = 58 * 1024 * 1024


def _dot(a, b):
    return jnp.dot(a, b, preferred_element_type=F32)


def _silu(v):
    return v * jax.nn.sigmoid(v)


def _rms(v, g, n=None):
    n = v.shape[-1] if n is None else n
    ms = jnp.sum(v * v, axis=-1, keepdims=True) * (1.0 / n)
    return v * lax.rsqrt(ms + EPS) * g


def _ada_kernel(c_ref, w_ref, b_ref, o_ref):
    s = _silu(c_ref[...]).astype(BF16)
    o_ref[...] = _dot(s, w_ref[...].astype(BF16)) + b_ref[...]


def _ada_call(cc, w_ada, b_ada, layer):
    n = w_ada.shape[2]
    tn = 768
    return pl.pallas_call(
        _ada_kernel,
        grid=(n // tn,),
        in_specs=[
            pl.BlockSpec((MOD_ROWS, D_MODEL), lambda j: (0, 0)),
            pl.BlockSpec((None, D_MODEL, tn), lambda j: (layer, 0, j)),
            pl.BlockSpec((1, tn), lambda j: (0, j)),
        ],
        out_specs=pl.BlockSpec((MOD_ROWS, tn), lambda j: (0, j)),
        out_shape=jax.ShapeDtypeStruct((MOD_ROWS, n), F32),
        compiler_params=pltpu.CompilerParams(dimension_semantics=("arbitrary",)),
        name="ada_mod",
    )(cc, w_ada, b_ada)


CONV_STRIDE = SUBLANE + 1
CONV_PIECE = SUBLANE * CONV_STRIDE


def _conv_pieces(tm, row_len):
    padded = (tm // row_len) * (row_len + SUBLANE)
    return -(-padded // CONV_PIECE)


def _inproj_kernel(*refs, ctx, tm, row_len):
    if ctx:
        (x_ref, mod_ref, g_ref, w_ref, wdt_ref, cw_ref, cb_ref,
         xs_ref, bt_ref, cm_ref, dt_ref, pad_ref, cv_ref) = refs
    else:
        (x_ref, mod_ref, g_ref, w_ref, wdt_ref, cw_ref, cb_ref, wc_ref,
         uu_ref, z_ref, xs_ref, bt_ref, cm_ref, dt_ref, pad_ref, cv_ref) = refs

    xt = x_ref[0]
    if ctx:
        shift = mod_ref[MOD_CTX_ROW:MOD_CTX_ROW + 1, 0:D_MODEL]
        scale = mod_ref[MOD_CTX_ROW:MOD_CTX_ROW + 1, D_MODEL:2 * D_MODEL]
    else:
        b = pl.program_id(0)
        shift = mod_ref[pl.ds(b, 1), 0:D_MODEL]
        scale = mod_ref[pl.ds(b, 1), D_MODEL:2 * D_MODEL]
    h = _rms(xt, g_ref[...]) * (1.0 + scale) + shift
    hb = h.astype(BF16)

    if not ctx:
        uf = _dot(hb, w_ref[:, 0:F_WIDTH])
        uu_ref[0] = _dot(uf.astype(BF16), wc_ref[...].astype(BF16)).astype(BF16)
        z_ref[0] = _dot(hb, w_ref[:, F_WIDTH:CTX_COL0]).astype(BF16)
        xbc = _dot(hb, w_ref[:, CTX_COL0:W_COLS])
    else:
        xbc = _dot(hb, w_ref[:, CTX_COL0:W_COLS])
    dt_ref[0] = lax.dot_general(wdt_ref[...], hb, (((1,), (1,)), ((), ())),
                                preferred_element_type=F32)

    pitch = row_len + SUBLANE
    nrow = tm // row_len
    npiece = _conv_pieces(tm, row_len)
    data_end = SUBLANE + nrow * pitch
    alloc_end = pad_ref.shape[1]
    zrow = jnp.zeros((SUBLANE, LANE), F32)
    for j in range(XBC_PAD // LANE):
        pad_ref[j, 0:SUBLANE, :] = zrow
        for r in range(nrow):
            base = SUBLANE + r * pitch
            pad_ref[j, base:base + row_len, :] = xbc[r * row_len:(r + 1) * row_len,
                                                     j * LANE:(j + 1) * LANE]
            pad_ref[j, base + row_len:base + pitch, :] = zrow
        for z0 in range(data_end, alloc_end, SUBLANE):
            pad_ref[j, z0:z0 + SUBLANE, :] = zrow

    for j in range(XBC_PAD // LANE):
        wk = [jnp.broadcast_to(cw_ref[k:k + 1, j * LANE:(j + 1) * LANE], (SUBLANE, LANE))
              for k in range(D_CONV)]
        bias = jnp.broadcast_to(cb_ref[:, j * LANE:(j + 1) * LANE], (SUBLANE, LANE))
        for p in range(npiece):
            base = SUBLANE + p * CONV_PIECE
            wins = {v: pad_ref[j, pl.ds(base + v, SUBLANE, stride=CONV_STRIDE), :]
                    for v in range(-HALF_CONV, CONV_STRIDE + HALF_CONV)}
            for a in range(CONV_STRIDE):
                acc = bias
                for k in range(D_CONV):
                    acc = acc + wins[a + k - HALF_CONV] * wk[k]
                cv_ref[j, pl.ds(base + a, SUBLANE, stride=CONV_STRIDE), :] = acc

    def conv_rows(j, t0, n):
        parts = []
        for t in range(t0, t0 + n, min(n, row_len)):
            src = SUBLANE + (t // row_len) * pitch + t % row_len
            parts.append(cv_ref[j, src:src + min(n, row_len), :])
        return _silu(parts[0] if len(parts) == 1 else jnp.concatenate(parts, axis=0))

    for j in range(XBC_PAD // LANE):
        c0 = j * LANE
        for ci in range(tm // CHUNK):
            val = conv_rows(j, ci * CHUNK, CHUNK)
            rows = slice(ci * CHUNK, (ci + 1) * CHUNK)
            if c0 < SSD_PAD:
                xs_ref[0, rows, c0:c0 + LANE] = val.astype(BF16)
            elif c0 < SSD_PAD + BC_WIDTH:
                cc = c0 - SSD_PAD
                bt_ref[0, ci, cc:cc + LANE, :] = val.T.astype(BF16)
            else:
                cc = c0 - SSD_PAD - BC_WIDTH
                cm_ref[0, rows, cc:cc + LANE] = val.astype(BF16)


def _inproj_call(xin, mod, g, w, wdt, cw, cb, wc, *, ctx, tm, row_len):
    bsz, seq, _ = xin.shape
    nt = seq // tm
    const2 = lambda b, j: (0, 0)
    tile3 = lambda b, j: (b, j, 0)
    in_specs = [
        pl.BlockSpec((1, tm, D_MODEL), tile3),
        pl.BlockSpec((MOD_ROWS, N_MOD * D_MODEL), const2),
        pl.BlockSpec((1, D_MODEL), const2),
        pl.BlockSpec((D_MODEL, W_COLS), const2),
        pl.BlockSpec((DT_ROWS, D_MODEL), const2),
        pl.BlockSpec((D_CONV, XBC_PAD), const2),
        pl.BlockSpec((1, XBC_PAD), const2),
    ]
    args = [xin, mod, g, w, wdt, cw, cb]
    out_specs = []
    out_shape = []
    if not ctx:
        in_specs.append(pl.BlockSpec((F_WIDTH, 2 * F_WIDTH), const2))
        args.append(wc)
        out_specs += [pl.BlockSpec((1, tm, 2 * F_WIDTH), tile3),
                      pl.BlockSpec((1, tm, SSD_PAD), tile3)]
        out_shape += [jax.ShapeDtypeStruct((bsz, seq, 2 * F_WIDTH), BF16),
                      jax.ShapeDtypeStruct((bsz, seq, SSD_PAD), BF16)]
    out_specs += [pl.BlockSpec((1, tm, SSD_PAD), tile3),
                  pl.BlockSpec((1, tm // CHUNK, BC_WIDTH, CHUNK), lambda b, j: (b, j, 0, 0)),
                  pl.BlockSpec((1, tm, BC_WIDTH), tile3),
                  pl.BlockSpec((1, DT_ROWS, tm), lambda b, j: (b, 0, j))]
    out_shape += [jax.ShapeDtypeStruct((bsz, seq, SSD_PAD), BF16),
                  jax.ShapeDtypeStruct((bsz, seq // CHUNK, BC_WIDTH, CHUNK), BF16),
                  jax.ShapeDtypeStruct((bsz, seq, BC_WIDTH), BF16),
                  jax.ShapeDtypeStruct((bsz, DT_ROWS, seq), F32)]
    pad_rows = 2 * SUBLANE + _conv_pieces(tm, row_len) * CONV_PIECE
    conv_scratch = pltpu.VMEM((XBC_PAD // LANE, pad_rows, LANE), F32)
    return pl.pallas_call(
        functools.partial(_inproj_kernel, ctx=ctx, tm=tm, row_len=row_len),
        grid=(bsz, nt),
        in_specs=in_specs,
        out_specs=out_specs,
        out_shape=out_shape,
        scratch_shapes=[conv_scratch, conv_scratch],
        compiler_params=pltpu.CompilerParams(
            dimension_semantics=("arbitrary", "arbitrary"),
            vmem_limit_bytes=VMEM_LIMIT),
        name="inproj_ctx" if ctx else "inproj_lat",
    )(*args)


def _ssd_kernel(xs_ref, bt_ref, cm_ref, dt_ref,
                xsc_ref, btc_ref, cmc_ref, dtc_ref,
                dtb_ref, alog_ref,
                y_ref, h_ref, r2_ref, w2_ref, ee_ref, qt_ref, *, seq, ctx_len):
    nchunk = seq // CHUNK
    nchunk_ctx = ctx_len // CHUNK

    h_ref[...] = jnp.zeros(h_ref.shape, F32)

    bias = dtb_ref[...]
    nega = -jnp.exp(alog_ref[...])
    sub_i = lax.broadcasted_iota(jnp.int32, (CHUNK, CHUNK), 0)
    lane_i = lax.broadcasted_iota(jnp.int32, (CHUNK, CHUNK), 1)
    tri = (lane_i <= sub_i, lane_i >= sub_i)
    lane_lo = lane_i < HEAD_DIM
    lane_row = lax.broadcasted_iota(jnp.int32, (1, CHUNK), 1)

    def colb(mat_t, idx):
        return jnp.broadcast_to(mat_t[:, idx:idx + 1], (CHUNK, CHUNK))

    def prepare(d, raws):
        r0 = d * DT_DIR_ROWS
        n = len(raws)
        v = jnp.concatenate(raws, axis=0) + jnp.concatenate([bias[r0:r0 + DT_DIR_ROWS]] * n, axis=0)
        dt = jnp.maximum(v, 0.0) + jnp.log1p(jnp.exp(-jnp.abs(v)))
        da = dt * jnp.concatenate([nega[r0:r0 + DT_DIR_ROWS]] * n, axis=0)
        lane_n = lax.broadcasted_iota(jnp.int32, da.shape, 1)
        cs = da
        sh = 1
        while sh < CHUNK:
            if d == 0:
                cs = cs + jnp.where(lane_n >= sh, pltpu.roll(cs, sh, axis=1), 0.0)
            else:
                cs = cs + jnp.where(lane_n < CHUNK - sh, pltpu.roll(cs, CHUNK - sh, axis=1), 0.0)
            sh *= 2
        a_end = jnp.sum(da, axis=1, keepdims=True)
        cs2 = cs * LOG2E
        r2 = cs2 - jnp.log2(dt)
        w2 = jnp.exp2(a_end * LOG2E - r2)
        ee = jnp.broadcast_to(jnp.exp(a_end), da.shape)
        return cs2, r2, w2, ee

    def store_prep(d, k0, raws):
        cs2, r2, w2, ee = prepare(d, raws)
        for i in range(len(raws)):
            rows = slice(i * DT_DIR_ROWS, (i + 1) * DT_DIR_ROWS)
            r2_ref[d, k0 + i] = r2[rows]
            w2_ref[d, k0 + i] = w2[rows]
            ee_ref[d, k0 + i] = ee[rows]
            qt_ref[d, k0 + i] = jnp.concatenate(
                [cs2[rows], jnp.zeros((CHUNK - DT_DIR_ROWS, CHUNK), F32)], axis=0).T

    def direction(d, x, btc, cc, k, row0, mode):
        with_output = mode is not None
        r2 = r2_ref[d, k]
        w2 = w2_ref[d, k]
        ee = ee_ref[d, k]
        q_t = qt_ref[d, k]

        for g in range(SSD_GROUPS):
            cg = cc[:, g * D_STATE:(g + 1) * D_STATE]
            bt = btc[g * D_STATE:(g + 1) * D_STATE, :]
            hg = h_ref[d, g]
            xg = x[:, g * GROUP_PAD:(g + 1) * GROUP_PAD]
            t0 = xg[:, 0:CHUNK]
            t1 = xg[:, CHUNK:2 * CHUNK]
            heads = [g * HEADS_PER_GROUP + r for r in range(HEADS_PER_GROUP)]
            if with_output:
                scores = _dot(cg, bt)
                hb = hg.astype(BF16)
                lhs = []
                for hd in heads:
                    col = colb(q_t, hd)
                    seg = col - r2[hd:hd + 1, :]
                    m = (scores * jnp.exp2(jnp.where(tri[d], seg, -jnp.inf))).astype(BF16)
                    ce = cg * jnp.exp2(col).astype(BF16)
                    lhs.append(jnp.concatenate([m, ce], axis=1))
                rhs0 = jnp.concatenate([t0, hb[:, 0:CHUNK]], axis=0)
                rhs1 = jnp.concatenate([t1, hb[:, CHUNK:2 * CHUNK]], axis=0)
                y01 = _dot(jnp.concatenate([lhs[0], lhs[1]], axis=0), rhs0)
                y0 = jnp.where(lane_lo, y01[0:CHUNK], y01[CHUNK:2 * CHUNK])
                y1 = _dot(lhs[2], rhs1)
                y_g = jnp.concatenate([y0, y1], axis=1)
                if mode:
                    y_g = y_g + y_ref[0, pl.ds(row0, CHUNK), g * GROUP_PAD:(g + 1) * GROUP_PAD]
                y_ref[0, pl.ds(row0, CHUNK), g * GROUP_PAD:(g + 1) * GROUP_PAD] = y_g
            btf = bt.astype(F32)
            btw = [(btf * w2[hd:hd + 1, :]).astype(BF16) for hd in heads]
            s01 = _dot(jnp.concatenate([btw[0], btw[1]], axis=0), t0)
            s0 = jnp.where(lane_lo, s01[0:D_STATE], s01[D_STATE:2 * D_STATE])
            s1 = _dot(btw[2], t1)
            er = [ee[hd:hd + 1, :] for hd in heads]
            e3 = jnp.concatenate([jnp.where(lane_row < HEAD_DIM, er[0], er[1]), er[2]], axis=1)
            h_ref[d, g] = e3 * hg + jnp.concatenate([s0, s1], axis=1)

    for d in range(2):
        rows_d = slice(d * DT_DIR_ROWS, (d + 1) * DT_DIR_ROWS)
        store_prep(d, 0, [dtc_ref[0, rows_d, c * CHUNK:(c + 1) * CHUNK] for c in range(nchunk_ctx)])
        store_prep(d, nchunk_ctx, [dt_ref[0, rows_d, c * CHUNK:(c + 1) * CHUNK] for c in range(nchunk)])

    for i in range(nchunk_ctx):
        for d in range(2):
            ci = i if d == 0 else nchunk_ctx - 1 - i
            rows = slice(ci * CHUNK, (ci + 1) * CHUNK)
            direction(d, xsc_ref[0, rows, :], btc_ref[0, ci], cmc_ref[0, rows, :], ci, 0, None)

    def step(i, carry, *, accumulate):
        for d in range(2):
            ci = i if d == 0 else nchunk - 1 - i
            row0 = pl.multiple_of(ci * CHUNK, CHUNK)
            direction(d, xs_ref[0, pl.ds(row0, CHUNK), :], bt_ref[0, ci],
                      cm_ref[0, pl.ds(row0, CHUNK), :], nchunk_ctx + ci, row0, accumulate)
        return carry

    half = nchunk // 2
    lax.fori_loop(0, half, functools.partial(step, accumulate=False), 0, unroll=2)
    lax.fori_loop(half, nchunk, functools.partial(step, accumulate=True), 0, unroll=2)


def _ssd_call(xs, bm, cm, dt, xsc, bmc, cmc, dtc, dtb, alog):
    bsz, seq, _ = xs.shape
    ctx_len = xsc.shape[1]
    nck = (seq + ctx_len) // CHUNK
    per_b = lambda b: (b, 0, 0)
    const2 = lambda b: (0, 0)
    return pl.pallas_call(
        functools.partial(_ssd_kernel, seq=seq, ctx_len=ctx_len),
        grid=(bsz,),
        in_specs=[
            pl.BlockSpec((1, seq, SSD_PAD), per_b),
            pl.BlockSpec((1, seq // CHUNK, BC_WIDTH, CHUNK), lambda b: (b, 0, 0, 0)),
            pl.BlockSpec((1, seq, BC_WIDTH), per_b),
            pl.BlockSpec((1, DT_ROWS, seq), per_b),
            pl.BlockSpec((1, ctx_len, SSD_PAD), per_b),
            pl.BlockSpec((1, ctx_len // CHUNK, BC_WIDTH, CHUNK), lambda b: (b, 0, 0, 0)),
            pl.BlockSpec((1, ctx_len, BC_WIDTH), per_b),
            pl.BlockSpec((1, DT_ROWS, ctx_len), per_b),
            pl.BlockSpec((DT_ROWS, 1), const2),
            pl.BlockSpec((DT_ROWS, 1), const2),
        ],
        out_specs=pl.BlockSpec((1, seq, SSD_PAD), per_b),
        out_shape=jax.ShapeDtypeStruct((bsz, seq, SSD_PAD), F32),
        scratch_shapes=[
            pltpu.VMEM((2, SSD_GROUPS, D_STATE, GROUP_PAD), F32),
            pltpu.VMEM((2, nck, DT_DIR_ROWS, CHUNK), F32),
            pltpu.VMEM((2, nck, DT_DIR_ROWS, CHUNK), F32),
            pltpu.VMEM((2, nck, DT_DIR_ROWS, CHUNK), F32),
            pltpu.VMEM((2, nck, CHUNK, CHUNK), F32),
        ],
        compiler_params=pltpu.CompilerParams(
            dimension_semantics=("arbitrary",), vmem_limit_bytes=VMEM_LIMIT),
        name="ssd_scan",
    )(xs, bm, cm, dt, xsc, bmc, cmc, dtc, dtb, alog)


FFN_CHUNK = 256


def _out_ffn_kernel(x_ref, y_ref, xs_ref, z_ref, uu_ref, cl_ref, sl_ref, mod_ref,
                    dskip_ref, gssd_ref, wof_ref, woy_ref, gpm_ref, gpf_ref, gpo_ref,
                    wg_ref, wu_ref, wd_ref, out_ref):
    b = pl.program_id(1)
    yf = (_dot(cl_ref[...], uu_ref[0, :, 0:F_WIDTH])
          + _dot(sl_ref[...], uu_ref[0, :, F_WIDTH:2 * F_WIDTH]))
    y = y_ref[0] + xs_ref[0].astype(F32) * dskip_ref[...]
    y = y * _silu(z_ref[0].astype(F32))
    yn = _rms(y, gssd_ref[...], n=SSD_WIDTH).astype(BF16)
    mix = _dot(yf.astype(BF16), wof_ref[...]) + _dot(yn, woy_ref[...])

    def mod(k):
        return mod_ref[pl.ds(b, 1), k * D_MODEL:(k + 1) * D_MODEL]

    x1 = x_ref[0] + mod(2) * _rms(mix, gpm_ref[...])
    h2 = (_rms(x1, gpf_ref[...]) * (1.0 + mod(4)) + mod(3)).astype(BF16)
    ffn = None
    for c0 in range(0, wg_ref.shape[1], FFN_CHUNK):
        gate = _dot(h2, wg_ref[:, c0:c0 + FFN_CHUNK])
        up = _dot(h2, wu_ref[:, c0:c0 + FFN_CHUNK])
        act = (_silu(gate) * up).astype(BF16)
        part = _dot(act, wd_ref[c0:c0 + FFN_CHUNK, :])
        ffn = part if ffn is None else ffn + part
    out_ref[0] = x1 + mod(5) * _rms(ffn, gpo_ref[...])


def _out_ffn_call(x, y, xs, z, uu, cl, sl, mod, dskip, gssd, wof, woy, gpm, gpf, gpo, wg, wu, wd, *, tm):
    bsz, seq, _ = x.shape
    d_ff = wg.shape[1]
    nt = seq // tm
    const2 = lambda j, b: (0, 0)
    tile3 = lambda j, b: (b, j, 0)
    single = dict(pipeline_mode=pl.Buffered(1))
    return pl.pallas_call(
        _out_ffn_kernel,
        grid=(nt, bsz),
        in_specs=[
            pl.BlockSpec((1, tm, D_MODEL), tile3),
            pl.BlockSpec((1, tm, SSD_PAD), tile3),
            pl.BlockSpec((1, tm, SSD_PAD), tile3),
            pl.BlockSpec((1, tm, SSD_PAD), tile3),
            pl.BlockSpec((1, seq, 2 * F_WIDTH), lambda j, b: (b, 0, 0)),
            pl.BlockSpec((tm, seq), lambda j, b: (j, 0)),
            pl.BlockSpec((tm, seq), lambda j, b: (j, 0)),
            pl.BlockSpec((MOD_ROWS, N_MOD * D_MODEL), const2),
            pl.BlockSpec((1, SSD_PAD), const2),
            pl.BlockSpec((1, SSD_PAD), const2),
            pl.BlockSpec((F_WIDTH, D_MODEL), const2, **single),
            pl.BlockSpec((SSD_PAD, D_MODEL), const2, **single),
            pl.BlockSpec((1, D_MODEL), const2),
            pl.BlockSpec((1, D_MODEL), const2),
            pl.BlockSpec((1, D_MODEL), const2),
            pl.BlockSpec((D_MODEL, d_ff), const2, **single),
            pl.BlockSpec((D_MODEL, d_ff), const2, **single),
            pl.BlockSpec((d_ff, D_MODEL), const2, **single),
        ],
        out_specs=pl.BlockSpec((1, tm, D_MODEL), tile3),
        out_shape=jax.ShapeDtypeStruct((bsz, seq, D_MODEL), F32),
        compiler_params=pltpu.CompilerParams(
            dimension_semantics=("arbitrary", "arbitrary"),
            vmem_limit_bytes=VMEM_LIMIT),
        name="out_ffn",
    )(x, y, xs, z, uu, cl, sl, mod, dskip, gssd, wof, woy, gpm, gpf, gpo, wg, wu, wd)


@functools.lru_cache(maxsize=None)
def _dft_tables(seq):
    k = np.arange(seq, dtype=np.int64)
    ang = 2.0 * np.pi * ((k[:, None] * k[None, :]) % seq).astype(np.float64) / seq
    scale = 1.0 / math.sqrt(seq)
    cl = (np.cos(ang) * scale).astype(np.float32)
    sl = (np.sin(ang) * scale).astype(np.float32)
    c = np.arange(FGROUP_DIM, dtype=np.int64)
    angc = 2.0 * np.pi * ((c[:, None] * c[None, :]) % FGROUP_DIM).astype(np.float64) / FGROUP_DIM
    sc = 1.0 / math.sqrt(FGROUP_DIM)
    wc = np.zeros((F_WIDTH, 2 * F_WIDTH), np.float32)
    for g in range(N_FGROUPS):
        s = slice(g * FGROUP_DIM, (g + 1) * FGROUP_DIM)
        wc[s, g * FGROUP_DIM:(g + 1) * FGROUP_DIM] = np.cos(angc) * sc
        wc[s, F_WIDTH + g * FGROUP_DIM:F_WIDTH + (g + 1) * FGROUP_DIM] = -np.sin(angc) * sc
    return cl, sl, wc


def _pad_groups(a, axis):
    a = jnp.moveaxis(a, axis, -1)
    lead = a.shape[:-1]
    a = a.reshape(lead + (SSD_GROUPS, HEADS_PER_GROUP * HEAD_DIM))
    a = jnp.pad(a, [(0, 0)] * len(lead) + [(0, 0), (0, GROUP_PAD - HEADS_PER_GROUP * HEAD_DIM)])
    a = a.reshape(lead + (SSD_PAD,))
    return jnp.moveaxis(a, -1, axis)


def _pad_dirs(a):
    a = jnp.pad(a, ((0, 0), (0, DT_DIR_ROWS - SSD_HEADS)))
    return a.reshape(DT_ROWS, 1)


def kernel(x, c, ctx, c_ctx, w_ada, b_ada, g_pre_mix, g_post_mix, g_pre_ffn, g_post_ffn,
           w_in, conv_w, conv_b, dt_bias, a_log, d_skip, g_ssd, w_out, w_gate, w_up, w_down):
    bsz, seq, _ = x.shape
    ctx_len = ctx.shape[1]
    l = 0
    cl_np, sl_np, wc_np = _dft_tables(seq)
    cl = jnp.asarray(cl_np).astype(BF16)
    sl = jnp.asarray(sl_np).astype(BF16)
    wc = jnp.asarray(wc_np)

    cc = jnp.concatenate(
        [c, c_ctx[None, :], jnp.zeros((MOD_ROWS - bsz - 1, D_MODEL), F32)], axis=0)
    mod = _ada_call(cc, w_ada, b_ada[l][None, :], l)

    wi = w_in[l]
    w_cat = jnp.concatenate([
        wi[:, :F_WIDTH],
        _pad_groups(wi[:, F_WIDTH:XBC_OFF], 1),
        _pad_groups(wi[:, XBC_OFF:XBC_OFF + SSD_WIDTH], 1),
        wi[:, XBC_OFF + SSD_WIDTH:XBC_OFF + CONV_DIM],
    ], axis=1).astype(BF16)
    w_dt = wi[:, XBC_OFF + CONV_DIM:].T.reshape(2, SSD_HEADS, D_MODEL)
    w_dt = jnp.pad(w_dt, ((0, 0), (0, DT_DIR_ROWS - SSD_HEADS), (0, 0))).reshape(DT_ROWS, D_MODEL)
    w_dt = w_dt.astype(BF16)
    cw = conv_w[l].T
    cw = jnp.concatenate([_pad_groups(cw[:, :SSD_WIDTH], 1), cw[:, SSD_WIDTH:]], axis=1)
    cb = conv_b[l][None, :]
    cb = jnp.concatenate([_pad_groups(cb[:, :SSD_WIDTH], 1), cb[:, SSD_WIDTH:]], axis=1)
    g_pre = g_pre_mix[l][None, :]

    xs_c, bm_c, cm_c, dt_c = _inproj_call(
        ctx, mod, g_pre, w_cat, w_dt, cw, cb, None, ctx=True, tm=ctx_len, row_len=ctx_len)
    uu, z, xs, bm, cm, dt = _inproj_call(
        x, mod, g_pre, w_cat, w_dt, cw, cb, wc, ctx=False, tm=512, row_len=GRID_W)

    dskip = _pad_groups(jnp.repeat(d_skip[l], HEAD_DIM)[None, :], 1)
    gssd = _pad_groups(g_ssd[l][None, :], 1)
    y = _ssd_call(xs, bm, cm, dt, xs_c, bm_c, cm_c, dt_c,
                  _pad_dirs(dt_bias[l]), _pad_dirs(a_log[l]))

    wo = w_out[l]
    wof = wo[:F_WIDTH].astype(BF16)
    woy = _pad_groups(wo[F_WIDTH:], 0).astype(BF16)
    return _out_ffn_call(
        x, y, xs, z, uu, cl, sl, mod, dskip, gssd, wof, woy,
        g_post_mix[l][None, :], g_pre_ffn[l][None, :], g_post_ffn[l][None, :],
        w_gate[l].astype(BF16), w_up[l].astype(BF16), w_down[l].astype(BF16), tm=512)
```

```python
import functools
import math

import jax
import jax.numpy as jnp
import numpy as np
from jax import lax
from jax.experimental import pallas as pl
from jax.experimental.pallas import tpu as pltpu

F32 = jnp.float32
BF16 = jnp.bfloat16

D_MODEL = 1024
GRID_W = 64
F_WIDTH = 256
N_FGROUPS = 4
FGROUP_DIM = F_WIDTH // N_FGROUPS
SSD_WIDTH = 768
HEAD_DIM = 64
SSD_HEADS = 12
SSD_GROUPS = 4
HEADS_PER_GROUP = 3
D_STATE = 128
BC_WIDTH = SSD_GROUPS * D_STATE
D_CONV = 7
HALF_CONV = D_CONV // 2
CHUNK = 128
XBC_OFF = F_WIDTH + SSD_WIDTH
CONV_DIM = SSD_WIDTH + 2 * BC_WIDTH
N_MOD = 6
EPS = 1e-6
LOG2E = math.log2(math.e)

GROUP_PAD = 256
SSD_PAD = SSD_GROUPS * GROUP_PAD
XBC_PAD = SSD_PAD + 2 * BC_WIDTH
W_COLS = F_WIDTH + SSD_PAD + XBC_PAD
CTX_COL0 = F_WIDTH + SSD_PAD
DT_ROWS = 32
DT_DIR_ROWS = 16
MOD_ROWS = 16
MOD_CTX_ROW = 8

SUBLANE = 8
LANE = 128
VMEM_LIMIT = 56 * 1024 * 1024


def _dot(a, b):
    return jnp.dot(a, b, preferred_element_type=F32)


def _silu(v):
    return v * jax.nn.sigmoid(v)


def _rms(v, g, n=None):
    n = v.shape[-1] if n is None else n
    ms = jnp.sum(v * v, axis=-1, keepdims=True) * (1.0 / n)
    return v * lax.rsqrt(ms + EPS) * g


def _ada_kernel(c_ref, w_ref, b_ref, o_ref):
    s = _silu(c_ref[...]).astype(BF16)
    o_ref[...] = _dot(s, w_ref[...].astype(BF16)) + b_ref[...]


def _ada_call(cc, w_ada, b_ada, layer):
    n = w_ada.shape[2]
    tn = 768
    return pl.pallas_call(
        _ada_kernel,
        grid=(n // tn,),
        in_specs=[
            pl.BlockSpec((MOD_ROWS, D_MODEL), lambda j: (0, 0)),
            pl.BlockSpec((None, D_MODEL, tn), lambda j: (layer, 0, j)),
            pl.BlockSpec((1, tn), lambda j: (0, j)),
        ],
        out_specs=pl.BlockSpec((MOD_ROWS, tn), lambda j: (0, j)),
        out_shape=jax.ShapeDtypeStruct((MOD_ROWS, n), F32),
        compiler_params=pltpu.CompilerParams(dimension_semantics=("arbitrary",)),
        name="ada_mod",
    )(cc, w_ada, b_ada)


CONV_STRIDE = SUBLANE + 1
CONV_PIECE = SUBLANE * CONV_STRIDE
INPROJ_SPLIT = 1


def _conv_pieces(tm, row_len):
    padded = (tm // row_len) * (row_len + SUBLANE)
    return -(-padded // CONV_PIECE)


def _inproj_kernel(*refs, ctx, tm, row_len):
    if ctx:
        (x_ref, mod_ref, g_ref, w_ref, wdt_ref, cw_ref, cb_ref,
         xs_ref, bt_ref, cm_ref, dt_ref, pad_ref, cv_ref) = refs
    else:
        (x_ref, mod_ref, g_ref, w_ref, wdt_ref, cw_ref, cb_ref, wc_ref,
         uu_ref, z_ref, xs_ref, bt_ref, cm_ref, dt_ref, pad_ref, cv_ref) = refs

    if ctx:
        shift = mod_ref[MOD_CTX_ROW:MOD_CTX_ROW + 1, 0:D_MODEL]
        scale = mod_ref[MOD_CTX_ROW:MOD_CTX_ROW + 1, D_MODEL:2 * D_MODEL]
    else:
        b = pl.program_id(0)
        shift = mod_ref[pl.ds(b, 1), 0:D_MODEL]
        scale = mod_ref[pl.ds(b, 1), D_MODEL:2 * D_MODEL]

    nsub = pad_ref.shape[0]
    tms = tm // nsub
    pitch = row_len + SUBLANE
    nrow = tms // row_len
    npiece = _conv_pieces(tms, row_len)
    data_end = SUBLANE + nrow * pitch
    alloc_end = pad_ref.shape[2]
    zrow = jnp.zeros((SUBLANE, LANE), F32)

    for sub in range(nsub):
        r0 = sub * tms
        xt = x_ref[0, r0:r0 + tms, :]
        h = _rms(xt, g_ref[...]) * (1.0 + scale) + shift
        hb = h.astype(BF16)

        if not ctx:
            uf = _dot(hb, w_ref[:, 0:F_WIDTH])
            uu_ref[0, r0:r0 + tms, :] = _dot(uf.astype(BF16), wc_ref[...].astype(BF16)).astype(BF16)
            z_ref[0, r0:r0 + tms, :] = _dot(hb, w_ref[:, F_WIDTH:CTX_COL0]).astype(BF16)
        xbc = _dot(hb, w_ref[:, CTX_COL0:W_COLS])
        dt_ref[0, :, r0:r0 + tms] = lax.dot_general(wdt_ref[...], hb, (((1,), (1,)), ((), ())),
                                                    preferred_element_type=F32)

        for j in range(XBC_PAD // LANE):
            pad_ref[sub, j, 0:SUBLANE, :] = zrow
            for r in range(nrow):
                base = SUBLANE + r * pitch
                pad_ref[sub, j, base:base + row_len, :] = xbc[r * row_len:(r + 1) * row_len,
                                                              j * LANE:(j + 1) * LANE]
                pad_ref[sub, j, base + row_len:base + pitch, :] = zrow
            for z0 in range(data_end, alloc_end, SUBLANE):
                pad_ref[sub, j, z0:z0 + SUBLANE, :] = zrow

        for j in range(XBC_PAD // LANE):
            wk = [jnp.broadcast_to(cw_ref[k:k + 1, j * LANE:(j + 1) * LANE], (SUBLANE, LANE))
                  for k in range(D_CONV)]
            bias = jnp.broadcast_to(cb_ref[:, j * LANE:(j + 1) * LANE], (SUBLANE, LANE))
            for p in range(npiece):
                base = SUBLANE + p * CONV_PIECE
                wins = {v: pad_ref[sub, j, pl.ds(base + v, SUBLANE, stride=CONV_STRIDE), :]
                        for v in range(-HALF_CONV, CONV_STRIDE + HALF_CONV)}
                for a in range(CONV_STRIDE):
                    acc = bias
                    for k in range(D_CONV):
                        acc = acc + wins[a + k - HALF_CONV] * wk[k]
                    cv_ref[sub, j, pl.ds(base + a, SUBLANE, stride=CONV_STRIDE), :] = acc

        def conv_rows(j, t0, n):
            parts = []
            for t in range(t0, t0 + n, min(n, row_len)):
                src = SUBLANE + (t // row_len) * pitch + t % row_len
                parts.append(cv_ref[sub, j, src:src + min(n, row_len), :])
            return _silu(parts[0] if len(parts) == 1 else jnp.concatenate(parts, axis=0))

        for j in range(XBC_PAD // LANE):
            c0 = j * LANE
            for ci in range(tms // CHUNK):
                val = conv_rows(j, ci * CHUNK, CHUNK)
                rows = slice(r0 + ci * CHUNK, r0 + (ci + 1) * CHUNK)
                if c0 < SSD_PAD:
                    xs_ref[0, rows, c0:c0 + LANE] = val.astype(BF16)
                elif c0 < SSD_PAD + BC_WIDTH:
                    cc = c0 - SSD_PAD
                    bt_ref[0, r0 // CHUNK + ci, cc:cc + LANE, :] = val.T.astype(BF16)
                else:
                    cc = c0 - SSD_PAD - BC_WIDTH
                    cm_ref[0, rows, cc:cc + LANE] = val.astype(BF16)


def _inproj_call(xin, mod, g, w, wdt, cw, cb, wc, *, ctx, tm, row_len):
    bsz, seq, _ = xin.shape
    nt = seq // tm
    const2 = lambda b, j: (0, 0)
    tile3 = lambda b, j: (b, j, 0)
    in_specs = [
        pl.BlockSpec((1, tm, D_MODEL), tile3),
        pl.BlockSpec((MOD_ROWS, N_MOD * D_MODEL), const2),
        pl.BlockSpec((1, D_MODEL), const2),
        pl.BlockSpec((D_MODEL, W_COLS), const2),
        pl.BlockSpec((DT_ROWS, D_MODEL), const2),
        pl.BlockSpec((D_CONV, XBC_PAD), const2),
        pl.BlockSpec((1, XBC_PAD), const2),
    ]
    args = [xin, mod, g, w, wdt, cw, cb]
    out_specs = []
    out_shape = []
    if not ctx:
        in_specs.append(pl.BlockSpec((F_WIDTH, 2 * F_WIDTH), const2))
        args.append(wc)
        out_specs += [pl.BlockSpec((1, tm, 2 * F_WIDTH), tile3),
                      pl.BlockSpec((1, tm, SSD_PAD), tile3)]
        out_shape += [jax.ShapeDtypeStruct((bsz, seq, 2 * F_WIDTH), BF16),
                      jax.ShapeDtypeStruct((bsz, seq, SSD_PAD), BF16)]
    out_specs += [pl.BlockSpec((1, tm, SSD_PAD), tile3),
                  pl.BlockSpec((1, tm // CHUNK, BC_WIDTH, CHUNK), lambda b, j: (b, j, 0, 0)),
                  pl.BlockSpec((1, tm, BC_WIDTH), tile3),
                  pl.BlockSpec((1, DT_ROWS, tm), lambda b, j: (b, 0, j))]
    out_shape += [jax.ShapeDtypeStruct((bsz, seq, SSD_PAD), BF16),
                  jax.ShapeDtypeStruct((bsz, seq // CHUNK, BC_WIDTH, CHUNK), BF16),
                  jax.ShapeDtypeStruct((bsz, seq, BC_WIDTH), BF16),
                  jax.ShapeDtypeStruct((bsz, DT_ROWS, seq), F32)]
    nsub = INPROJ_SPLIT if tm % (INPROJ_SPLIT * max(row_len, CHUNK)) == 0 else 1
    pad_rows = 2 * SUBLANE + _conv_pieces(tm // nsub, row_len) * CONV_PIECE
    conv_scratch = pltpu.VMEM((nsub, XBC_PAD // LANE, pad_rows, LANE), F32)
    return pl.pallas_call(
        functools.partial(_inproj_kernel, ctx=ctx, tm=tm, row_len=row_len),
        grid=(bsz, nt),
        in_specs=in_specs,
        out_specs=out_specs,
        out_shape=out_shape,
        scratch_shapes=[conv_scratch, conv_scratch],
        compiler_params=pltpu.CompilerParams(
            dimension_semantics=("arbitrary", "arbitrary"),
            vmem_limit_bytes=VMEM_LIMIT),
        name="inproj_ctx" if ctx else "inproj_lat",
    )(*args)


def _ssd_kernel(xs_ref, bt_ref, cm_ref, z_ref, dt_ref,
                xsc_ref, btc_ref, cmc_ref, dtc_ref,
                dtb_ref, alog_ref, dskip_ref, gssd_ref,
                out_ref, h_ref, y_ref, r2_ref, w2_ref, ee_ref, qt_ref, *, seq, ctx_len):
    nchunk = seq // CHUNK
    nchunk_ctx = ctx_len // CHUNK

    h_ref[...] = jnp.zeros(h_ref.shape, F32)

    bias = dtb_ref[...]
    nega = -jnp.exp(alog_ref[...])
    sub_i = lax.broadcasted_iota(jnp.int32, (CHUNK, CHUNK), 0)
    lane_i = lax.broadcasted_iota(jnp.int32, (CHUNK, CHUNK), 1)
    tri = (lane_i <= sub_i, lane_i >= sub_i)
    lane_lo = lane_i < HEAD_DIM
    lane_row = lax.broadcasted_iota(jnp.int32, (1, CHUNK), 1)

    def colb(mat_t, idx):
        return jnp.broadcast_to(mat_t[:, idx:idx + 1], (CHUNK, CHUNK))

    def prepare(d, raws):
        r0 = d * DT_DIR_ROWS
        n = len(raws)
        v = jnp.concatenate(raws, axis=0) + jnp.concatenate([bias[r0:r0 + DT_DIR_ROWS]] * n, axis=0)
        dt = jnp.maximum(v, 0.0) + jnp.log1p(jnp.exp(-jnp.abs(v)))
        da = dt * jnp.concatenate([nega[r0:r0 + DT_DIR_ROWS]] * n, axis=0)
        lane_n = lax.broadcasted_iota(jnp.int32, da.shape, 1)
        cs = da
        sh = 1
        while sh < CHUNK:
            if d == 0:
                cs = cs + jnp.where(lane_n >= sh, pltpu.roll(cs, sh, axis=1), 0.0)
            else:
                cs = cs + jnp.where(lane_n < CHUNK - sh, pltpu.roll(cs, CHUNK - sh, axis=1), 0.0)
            sh *= 2
        a_end = jnp.sum(da, axis=1, keepdims=True)
        cs2 = cs * LOG2E
        r2 = cs2 - jnp.log2(dt)
        w2 = jnp.exp2(a_end * LOG2E - r2)
        ee = jnp.broadcast_to(jnp.exp(a_end), da.shape)
        return cs2, r2, w2, ee

    def store_prep(d, k0, raws):
        cs2, r2, w2, ee = prepare(d, raws)
        for i in range(len(raws)):
            rows = slice(i * DT_DIR_ROWS, (i + 1) * DT_DIR_ROWS)
            r2_ref[d, k0 + i] = r2[rows]
            w2_ref[d, k0 + i] = w2[rows]
            ee_ref[d, k0 + i] = ee[rows]
            qt_ref[d, k0 + i] = jnp.concatenate(
                [cs2[rows], jnp.zeros((CHUNK - DT_DIR_ROWS, CHUNK), F32)], axis=0).T

    def direction(d, x, btc, cc, k, row0, mode):
        with_output = mode is not None
        r2 = r2_ref[d, k]
        w2 = w2_ref[d, k]
        ee = ee_ref[d, k]
        q_t = qt_ref[d, k]

        for g in range(SSD_GROUPS):
            cg = cc[:, g * D_STATE:(g + 1) * D_STATE]
            bt = btc[g * D_STATE:(g + 1) * D_STATE, :]
            hg = h_ref[d, g]
            xg = x[:, g * GROUP_PAD:(g + 1) * GROUP_PAD]
            t0 = xg[:, 0:CHUNK]
            t1 = xg[:, CHUNK:2 * CHUNK]
            heads = [g * HEADS_PER_GROUP + r for r in range(HEADS_PER_GROUP)]
            if with_output:
                scores = _dot(cg, bt)
                hb = hg.astype(BF16)
                lhs = []
                for hd in heads:
                    col = colb(q_t, hd)
                    seg = col - r2[hd:hd + 1, :]
                    m = (scores * jnp.exp2(jnp.where(tri[d], seg, -jnp.inf))).astype(BF16)
                    ce = cg * jnp.exp2(col).astype(BF16)
                    lhs.append(jnp.concatenate([m, ce], axis=1))
                rhs0 = jnp.concatenate([t0, hb[:, 0:CHUNK]], axis=0)
                rhs1 = jnp.concatenate([t1, hb[:, CHUNK:2 * CHUNK]], axis=0)
                y01 = _dot(jnp.concatenate([lhs[0], lhs[1]], axis=0), rhs0)
                y0 = jnp.where(lane_lo, y01[0:CHUNK], y01[CHUNK:2 * CHUNK])
                y1 = _dot(lhs[2], rhs1)
                y_g = jnp.concatenate([y0, y1], axis=1)
                if mode:
                    y_g = y_g + y_ref[pl.ds(row0, CHUNK), g * GROUP_PAD:(g + 1) * GROUP_PAD]
                y_ref[pl.ds(row0, CHUNK), g * GROUP_PAD:(g + 1) * GROUP_PAD] = y_g
            btf = bt.astype(F32)
            btw = [(btf * w2[hd:hd + 1, :]).astype(BF16) for hd in heads]
            s01 = _dot(jnp.concatenate([btw[0], btw[1]], axis=0), t0)
            s0 = jnp.where(lane_lo, s01[0:D_STATE], s01[D_STATE:2 * D_STATE])
            s1 = _dot(btw[2], t1)
            er = [ee[hd:hd + 1, :] for hd in heads]
            e3 = jnp.concatenate([jnp.where(lane_row < HEAD_DIM, er[0], er[1]), er[2]], axis=1)
            h_ref[d, g] = e3 * hg + jnp.concatenate([s0, s1], axis=1)

    for d in range(2):
        rows_d = slice(d * DT_DIR_ROWS, (d + 1) * DT_DIR_ROWS)
        store_prep(d, 0, [dtc_ref[0, rows_d, c * CHUNK:(c + 1) * CHUNK] for c in range(nchunk_ctx)])
        store_prep(d, nchunk_ctx, [dt_ref[0, rows_d, c * CHUNK:(c + 1) * CHUNK] for c in range(nchunk)])

    for i in range(nchunk_ctx):
        for d in range(2):
            ci = i if d == 0 else nchunk_ctx - 1 - i
            rows = slice(ci * CHUNK, (ci + 1) * CHUNK)
            direction(d, xsc_ref[0, rows, :], btc_ref[0, ci], cmc_ref[0, rows, :], ci, 0, None)

    def step(i, carry, *, accumulate):
        for d in range(2):
            ci = i if d == 0 else nchunk - 1 - i
            row0 = pl.multiple_of(ci * CHUNK, CHUNK)
            direction(d, xs_ref[0, pl.ds(row0, CHUNK), :], bt_ref[0, ci],
                      cm_ref[0, pl.ds(row0, CHUNK), :], nchunk_ctx + ci, row0, accumulate)
        return carry

    half = nchunk // 2
    lax.fori_loop(0, half, functools.partial(step, accumulate=False), 0, unroll=2)
    lax.fori_loop(half, nchunk, functools.partial(step, accumulate=True), 0, unroll=2)

    te = 256

    def epi(i, carry):
        row0 = pl.multiple_of(i * te, te)
        xr = xs_ref[0, pl.ds(row0, te), :].astype(F32)
        zr = z_ref[0, pl.ds(row0, te), :].astype(F32)
        y = y_ref[pl.ds(row0, te), :] + xr * dskip_ref[...]
        y = y * _silu(zr)
        out_ref[0, pl.ds(row0, te), :] = _rms(y, gssd_ref[...], n=SSD_WIDTH).astype(BF16)
        return carry

    lax.fori_loop(0, seq // te, epi, 0)


def _ssd_call(xs, bm, cm, z, dt, xsc, bmc, cmc, dtc, dtb, alog, dskip, gssd):
    bsz, seq, _ = xs.shape
    ctx_len = xsc.shape[1]
    nck = (seq + ctx_len) // CHUNK
    per_b = lambda b: (b, 0, 0)
    const2 = lambda b: (0, 0)
    return pl.pallas_call(
        functools.partial(_ssd_kernel, seq=seq, ctx_len=ctx_len),
        grid=(bsz,),
        in_specs=[
            pl.BlockSpec((1, seq, SSD_PAD), per_b),
            pl.BlockSpec((1, seq // CHUNK, BC_WIDTH, CHUNK), lambda b: (b, 0, 0, 0)),
            pl.BlockSpec((1, seq, BC_WIDTH), per_b),
            pl.BlockSpec((1, seq, SSD_PAD), per_b),
            pl.BlockSpec((1, DT_ROWS, seq), per_b),
            pl.BlockSpec((1, ctx_len, SSD_PAD), per_b),
            pl.BlockSpec((1, ctx_len // CHUNK, BC_WIDTH, CHUNK), lambda b: (b, 0, 0, 0)),
            pl.BlockSpec((1, ctx_len, BC_WIDTH), per_b),
            pl.BlockSpec((1, DT_ROWS, ctx_len), per_b),
            pl.BlockSpec((DT_ROWS, 1), const2),
            pl.BlockSpec((DT_ROWS, 1), const2),
            pl.BlockSpec((1, SSD_PAD), const2),
            pl.BlockSpec((1, SSD_PAD), const2),
        ],
        out_specs=pl.BlockSpec((1, seq, SSD_PAD), per_b),
        out_shape=jax.ShapeDtypeStruct((bsz, seq, SSD_PAD), BF16),
        scratch_shapes=[
            pltpu.VMEM((2, SSD_GROUPS, D_STATE, GROUP_PAD), F32),
            pltpu.VMEM((seq, SSD_PAD), F32),
            pltpu.VMEM((2, nck, DT_DIR_ROWS, CHUNK), F32),
            pltpu.VMEM((2, nck, DT_DIR_ROWS, CHUNK), F32),
            pltpu.VMEM((2, nck, DT_DIR_ROWS, CHUNK), F32),
            pltpu.VMEM((2, nck, CHUNK, CHUNK), F32),
        ],
        compiler_params=pltpu.CompilerParams(
            dimension_semantics=("arbitrary",), vmem_limit_bytes=VMEM_LIMIT),
        name="ssd_scan",
    )(xs, bm, cm, z, dt, xsc, bmc, cmc, dtc, dtb, alog, dskip, gssd)


FFN_CHUNK = 256
MIX_SPLIT = 2


def _out_ffn_kernel(x_ref, yn_ref, uu_ref, cl_ref, sl_ref, mod_ref,
                    wof_ref, woy_ref, gpm_ref, gpf_ref, gpo_ref,
                    wg_ref, wu_ref, wd_ref, out_ref):
    b = pl.program_id(1)

    def mod(k):
        return mod_ref[pl.ds(b, 1), k * D_MODEL:(k + 1) * D_MODEL]

    half = x_ref.shape[1] // MIX_SPLIT
    x1_parts, h2_parts = [], []
    for r0 in range(0, x_ref.shape[1], half):
        rows = slice(r0, r0 + half)
        yf = (_dot(cl_ref[rows, :], uu_ref[0, :, 0:F_WIDTH])
              + _dot(sl_ref[rows, :], uu_ref[0, :, F_WIDTH:2 * F_WIDTH]))
        mix = _dot(yf.astype(BF16), wof_ref[...]) + _dot(yn_ref[0, rows, :], woy_ref[...])
        x1_h = x_ref[0, rows, :] + mod(2) * _rms(mix, gpm_ref[...])
        x1_parts.append(x1_h)
        h2_parts.append((_rms(x1_h, gpf_ref[...]) * (1.0 + mod(4)) + mod(3)).astype(BF16))
    x1 = jnp.concatenate(x1_parts, axis=0)
    h2 = jnp.concatenate(h2_parts, axis=0)
    ffn = None
    for c0 in range(0, wg_ref.shape[1], FFN_CHUNK):
        gate = _dot(h2, wg_ref[:, c0:c0 + FFN_CHUNK])
        up = _dot(h2, wu_ref[:, c0:c0 + FFN_CHUNK])
        act = (_silu(gate) * up).astype(BF16)
        part = _dot(act, wd_ref[c0:c0 + FFN_CHUNK, :])
        ffn = part if ffn is None else ffn + part
    out_ref[0] = x1 + mod(5) * _rms(ffn, gpo_ref[...])


def _out_ffn_call(x, yn, uu, cl, sl, mod, wof, woy, gpm, gpf, gpo, wg, wu, wd, *, tm):
    bsz, seq, _ = x.shape
    d_ff = wg.shape[1]
    nt = seq // tm
    const2 = lambda j, b: (0, 0)
    tile3 = lambda j, b: (b, j, 0)
    single = dict(pipeline_mode=pl.Buffered(1))
    return pl.pallas_call(
        _out_ffn_kernel,
        grid=(nt, bsz),
        in_specs=[
            pl.BlockSpec((1, tm, D_MODEL), tile3),
            pl.BlockSpec((1, tm, SSD_PAD), tile3),
            pl.BlockSpec((1, seq, 2 * F_WIDTH), lambda j, b: (b, 0, 0)),
            pl.BlockSpec((tm, seq), lambda j, b: (j, 0)),
            pl.BlockSpec((tm, seq), lambda j, b: (j, 0)),
            pl.BlockSpec((MOD_ROWS, N_MOD * D_MODEL), const2),
            pl.BlockSpec((F_WIDTH, D_MODEL), const2, **single),
            pl.BlockSpec((SSD_PAD, D_MODEL), const2, **single),
            pl.BlockSpec((1, D_MODEL), const2),
            pl.BlockSpec((1, D_MODEL), const2),
            pl.BlockSpec((1, D_MODEL), const2),
            pl.BlockSpec((D_MODEL, d_ff), const2, **single),
            pl.BlockSpec((D_MODEL, d_ff), const2, **single),
            pl.BlockSpec((d_ff, D_MODEL), const2, **single),
        ],
        out_specs=pl.BlockSpec((1, tm, D_MODEL), tile3),
        out_shape=jax.ShapeDtypeStruct((bsz, seq, D_MODEL), F32),
        compiler_params=pltpu.CompilerParams(
            dimension_semantics=("arbitrary", "arbitrary"),
            vmem_limit_bytes=VMEM_LIMIT),
        name="out_ffn",
    )(x, yn, uu, cl, sl, mod, wof, woy, gpm, gpf, gpo, wg, wu, wd)


@functools.lru_cache(maxsize=None)
def _dft_tables(seq):
    k = np.arange(seq, dtype=np.int64)
    ang = 2.0 * np.pi * ((k[:, None] * k[None, :]) % seq).astype(np.float64) / seq
    scale = 1.0 / math.sqrt(seq)
    cl = (np.cos(ang) * scale).astype(np.float32)
    sl = (np.sin(ang) * scale).astype(np.float32)
    c = np.arange(FGROUP_DIM, dtype=np.int64)
    angc = 2.0 * np.pi * ((c[:, None] * c[None, :]) % FGROUP_DIM).astype(np.float64) / FGROUP_DIM
    sc = 1.0 / math.sqrt(FGROUP_DIM)
    wc = np.zeros((F_WIDTH, 2 * F_WIDTH), np.float32)
    for g in range(N_FGROUPS):
        s = slice(g * FGROUP_DIM, (g + 1) * FGROUP_DIM)
        wc[s, g * FGROUP_DIM:(g + 1) * FGROUP_DIM] = np.cos(angc) * sc
        wc[s, F_WIDTH + g * FGROUP_DIM:F_WIDTH + (g + 1) * FGROUP_DIM] = -np.sin(angc) * sc
    return cl, sl, wc


def _pad_groups(a, axis):
    a = jnp.moveaxis(a, axis, -1)
    lead = a.shape[:-1]
    a = a.reshape(lead + (SSD_GROUPS, HEADS_PER_GROUP * HEAD_DIM))
    a = jnp.pad(a, [(0, 0)] * len(lead) + [(0, 0), (0, GROUP_PAD - HEADS_PER_GROUP * HEAD_DIM)])
    a = a.reshape(lead + (SSD_PAD,))
    return jnp.moveaxis(a, -1, axis)


def _pad_dirs(a):
    a = jnp.pad(a, ((0, 0), (0, DT_DIR_ROWS - SSD_HEADS)))
    return a.reshape(DT_ROWS, 1)


def kernel(x, c, ctx, c_ctx, w_ada, b_ada, g_pre_mix, g_post_mix, g_pre_ffn, g_post_ffn,
           w_in, conv_w, conv_b, dt_bias, a_log, d_skip, g_ssd, w_out, w_gate, w_up, w_down):
    bsz, seq, _ = x.shape
    ctx_len = ctx.shape[1]
    l = 0
    cl_np, sl_np, wc_np = _dft_tables(seq)
    cl = jnp.asarray(cl_np).astype(BF16)
    sl = jnp.asarray(sl_np).astype(BF16)
    wc = jnp.asarray(wc_np)

    cc = jnp.concatenate(
        [c, c_ctx[None, :], jnp.zeros((MOD_ROWS - bsz - 1, D_MODEL), F32)], axis=0)
    mod = _ada_call(cc, w_ada, b_ada[l][None, :], l)

    wi = w_in[l]
    w_cat = jnp.concatenate([
        wi[:, :F_WIDTH],
        _pad_groups(wi[:, F_WIDTH:XBC_OFF], 1),
        _pad_groups(wi[:, XBC_OFF:XBC_OFF + SSD_WIDTH], 1),
        wi[:, XBC_OFF + SSD_WIDTH:XBC_OFF + CONV_DIM],
    ], axis=1).astype(BF16)
    w_dt = wi[:, XBC_OFF + CONV_DIM:].T.reshape(2, SSD_HEADS, D_MODEL)
    w_dt = jnp.pad(w_dt, ((0, 0), (0, DT_DIR_ROWS - SSD_HEADS), (0, 0))).reshape(DT_ROWS, D_MODEL)
    w_dt = w_dt.astype(BF16)
    cw = conv_w[l].T
    cw = jnp.concatenate([_pad_groups(cw[:, :SSD_WIDTH], 1), cw[:, SSD_WIDTH:]], axis=1)
    cb = conv_b[l][None, :]
    cb = jnp.concatenate([_pad_groups(cb[:, :SSD_WIDTH], 1), cb[:, SSD_WIDTH:]], axis=1)
    g_pre = g_pre_mix[l][None, :]

    xs_c, bm_c, cm_c, dt_c = _inproj_call(
        ctx, mod, g_pre, w_cat, w_dt, cw, cb, None, ctx=True, tm=ctx_len, row_len=ctx_len)
    uu, z, xs, bm, cm, dt = _inproj_call(
        x, mod, g_pre, w_cat, w_dt, cw, cb, wc, ctx=False, tm=512, row_len=GRID_W)

    dskip = _pad_groups(jnp.repeat(d_skip[l], HEAD_DIM)[None, :], 1)
    gssd = _pad_groups(g_ssd[l][None, :], 1)
    yn = _ssd_call(xs, bm, cm, z, dt, xs_c, bm_c, cm_c, dt_c,
                   _pad_dirs(dt_bias[l]), _pad_dirs(a_log[l]), dskip, gssd)

    wo = w_out[l]
    wof = wo[:F_WIDTH].astype(BF16)
    woy = _pad_groups(wo[F_WIDTH:], 0).astype(BF16)
    return _out_ffn_call(
        x, yn, uu, cl, sl, mod, wof, woy,
        g_post_mix[l][None, :], g_pre_ffn[l][None, :], g_post_ffn[l][None, :],
        w_gate[l].astype(BF16), w_up[l].astype(BF16), w_down[l].astype(BF16), tm=512)
```

```python
import functools
import math

import jax
import jax.numpy as jnp
import numpy as np
from jax import lax
from jax.experimental import pallas as pl
from jax.experimental.pallas import tpu as pltpu

F32 = jnp.float32
BF16 = jnp.bfloat16

D_MODEL = 1024
GRID_W = 64
F_WIDTH = 256
N_FGROUPS = 4
FGROUP_DIM = F_WIDTH // N_FGROUPS
SSD_WIDTH = 768
HEAD_DIM = 64
SSD_HEADS = 12
SSD_GROUPS = 4
HEADS_PER_GROUP = 3
D_STATE = 128
BC_WIDTH = SSD_GROUPS * D_STATE
D_CONV = 7
HALF_CONV = D_CONV // 2
CHUNK = 128
XBC_OFF = F_WIDTH + SSD_WIDTH
CONV_DIM = SSD_WIDTH + 2 * BC_WIDTH
N_MOD = 6
EPS = 1e-6
LOG2E = math.log2(math.e)

GROUP_PAD = 256
SSD_PAD = SSD_GROUPS * GROUP_PAD
XBC_PAD = SSD_PAD + 2 * BC_WIDTH
W_COLS = F_WIDTH + SSD_PAD + XBC_PAD
CTX_COL0 = F_WIDTH + SSD_PAD
DT_ROWS = 32
DT_DIR_ROWS = 16
MOD_ROWS = 16
MOD_CTX_ROW = 8

SUBLANE = 8
LANE = 128
VMEM_LIMIT = 56 * 1024 * 1024


def _dot(a, b):
    return jnp.dot(a, b, preferred_element_type=F32)


def _dot_mixed(a_f32, b_bf16):
    return lax.dot_general(a_f32, b_bf16, (((1,), (0,)), ((), ())), preferred_element_type=F32)


def _silu(v):
    return v * jax.nn.sigmoid(v)


def _rms(v, g, n=None):
    n = v.shape[-1] if n is None else n
    ms = jnp.sum(v * v, axis=-1, keepdims=True) * (1.0 / n)
    return v * lax.rsqrt(ms + EPS) * g


def _ada_kernel(c_ref, w_ref, b_ref, o_ref):
    s = _silu(c_ref[...]).astype(BF16)
    o_ref[...] = _dot(s, w_ref[...].astype(BF16)) + b_ref[...]


def _ada_call(cc, w_ada, b_ada, layer):
    n = w_ada.shape[2]
    tn = 768
    return pl.pallas_call(
        _ada_kernel,
        grid=(n // tn,),
        in_specs=[
            pl.BlockSpec((MOD_ROWS, D_MODEL), lambda j: (0, 0)),
            pl.BlockSpec((None, D_MODEL, tn), lambda j: (layer, 0, j)),
            pl.BlockSpec((1, tn), lambda j: (0, j)),
        ],
        out_specs=pl.BlockSpec((MOD_ROWS, tn), lambda j: (0, j)),
        out_shape=jax.ShapeDtypeStruct((MOD_ROWS, n), F32),
        compiler_params=pltpu.CompilerParams(dimension_semantics=("arbitrary",)),
        name="ada_mod",
    )(cc, w_ada, b_ada)


CONV_STRIDE = SUBLANE + 1
CONV_PIECE = SUBLANE * CONV_STRIDE
INPROJ_SPLIT = 1


def _conv_pieces(tm, row_len):
    padded = (tm // row_len) * (row_len + SUBLANE)
    return -(-padded // CONV_PIECE)


def _inproj_kernel(*refs, ctx, tm, row_len):
    if ctx:
        (x_ref, mod_ref, g_ref, w_ref, wdt_ref, cw_ref, cb_ref,
         xs_ref, bt_ref, cm_ref, dt_ref, pad_ref, cv_ref) = refs
    else:
        (x_ref, mod_ref, g_ref, w_ref, wdt_ref, cw_ref, cb_ref, wc_ref,
         uu_ref, z_ref, xs_ref, bt_ref, cm_ref, dt_ref, pad_ref, cv_ref) = refs

    if ctx:
        shift = mod_ref[MOD_CTX_ROW:MOD_CTX_ROW + 1, 0:D_MODEL]
        scale = mod_ref[MOD_CTX_ROW:MOD_CTX_ROW + 1, D_MODEL:2 * D_MODEL]
    else:
        b = pl.program_id(0)
        shift = mod_ref[pl.ds(b, 1), 0:D_MODEL]
        scale = mod_ref[pl.ds(b, 1), D_MODEL:2 * D_MODEL]

    nsub = pad_ref.shape[0]
    tms = tm // nsub
    pitch = row_len + SUBLANE
    nrow = tms // row_len
    npiece = _conv_pieces(tms, row_len)
    data_end = SUBLANE + nrow * pitch
    alloc_end = pad_ref.shape[2]
    zrow = jnp.zeros((SUBLANE, LANE), F32)

    for sub in range(nsub):
        r0 = sub * tms
        xt = x_ref[0, r0:r0 + tms, :]
        h = _rms(xt, g_ref[...]) * (1.0 + scale) + shift
        hb = h.astype(BF16)

        if not ctx:
            uf = _dot(hb, w_ref[:, 0:F_WIDTH])
            uu_ref[0, r0:r0 + tms, :] = _dot(uf.astype(BF16), wc_ref[...].astype(BF16)).astype(BF16)
            z_ref[0, r0:r0 + tms, :] = _dot(hb, w_ref[:, F_WIDTH:CTX_COL0]).astype(BF16)
        xbc = _dot(hb, w_ref[:, CTX_COL0:W_COLS])
        dt_ref[0, :, r0:r0 + tms] = lax.dot_general(wdt_ref[...], hb, (((1,), (1,)), ((), ())),
                                                    preferred_element_type=F32)

        for j in range(XBC_PAD // LANE):
            pad_ref[sub, j, 0:SUBLANE, :] = zrow
            for r in range(nrow):
                base = SUBLANE + r * pitch
                pad_ref[sub, j, base:base + row_len, :] = xbc[r * row_len:(r + 1) * row_len,
                                                              j * LANE:(j + 1) * LANE]
                pad_ref[sub, j, base + row_len:base + pitch, :] = zrow
            for z0 in range(data_end, alloc_end, SUBLANE):
                pad_ref[sub, j, z0:z0 + SUBLANE, :] = zrow

        for j in range(XBC_PAD // LANE):
            wk = [jnp.broadcast_to(cw_ref[k:k + 1, j * LANE:(j + 1) * LANE], (SUBLANE, LANE))
                  for k in range(D_CONV)]
            bias = jnp.broadcast_to(cb_ref[:, j * LANE:(j + 1) * LANE], (SUBLANE, LANE))
            for p in range(npiece):
                base = SUBLANE + p * CONV_PIECE
                wins = {v: pad_ref[sub, j, pl.ds(base + v, SUBLANE, stride=CONV_STRIDE), :]
                        for v in range(-HALF_CONV, CONV_STRIDE + HALF_CONV)}
                for a in range(CONV_STRIDE):
                    acc = bias
                    for k in range(D_CONV):
                        acc = acc + wins[a + k - HALF_CONV] * wk[k]
                    cv_ref[sub, j, pl.ds(base + a, SUBLANE, stride=CONV_STRIDE), :] = acc

        def conv_rows(j, t0, n):
            parts = []
            for t in range(t0, t0 + n, min(n, row_len)):
                src = SUBLANE + (t // row_len) * pitch + t % row_len
                parts.append(cv_ref[sub, j, src:src + min(n, row_len), :])
            return _silu(parts[0] if len(parts) == 1 else jnp.concatenate(parts, axis=0))

        for j in range(XBC_PAD // LANE):
            c0 = j * LANE
            for ci in range(tms // CHUNK):
                val = conv_rows(j, ci * CHUNK, CHUNK)
                rows = slice(r0 + ci * CHUNK, r0 + (ci + 1) * CHUNK)
                if c0 < SSD_PAD:
                    xs_ref[0, rows, c0:c0 + LANE] = val.astype(BF16)
                elif c0 < SSD_PAD + BC_WIDTH:
                    cc = c0 - SSD_PAD
                    bt_ref[0, r0 // CHUNK + ci, cc:cc + LANE, :] = val.T.astype(BF16)
                else:
                    cc = c0 - SSD_PAD - BC_WIDTH
                    cm_ref[0, rows, cc:cc + LANE] = val.astype(BF16)


def _inproj_call(xin, mod, g, w, wdt, cw, cb, wc, *, ctx, tm, row_len):
    bsz, seq, _ = xin.shape
    nt = seq // tm
    const2 = lambda b, j: (0, 0)
    tile3 = lambda b, j: (b, j, 0)
    in_specs = [
        pl.BlockSpec((1, tm, D_MODEL), tile3),
        pl.BlockSpec((MOD_ROWS, N_MOD * D_MODEL), const2),
        pl.BlockSpec((1, D_MODEL), const2),
        pl.BlockSpec((D_MODEL, W_COLS), const2),
        pl.BlockSpec((DT_ROWS, D_MODEL), const2),
        pl.BlockSpec((D_CONV, XBC_PAD), const2),
        pl.BlockSpec((1, XBC_PAD), const2),
    ]
    args = [xin, mod, g, w, wdt, cw, cb]
    out_specs = []
    out_shape = []
    if not ctx:
        in_specs.append(pl.BlockSpec((F_WIDTH, 2 * F_WIDTH), const2))
        args.append(wc)
        out_specs += [pl.BlockSpec((1, tm, 2 * F_WIDTH), tile3),
                      pl.BlockSpec((1, tm, SSD_PAD), tile3)]
        out_shape += [jax.ShapeDtypeStruct((bsz, seq, 2 * F_WIDTH), BF16),
                      jax.ShapeDtypeStruct((bsz, seq, SSD_PAD), BF16)]
    out_specs += [pl.BlockSpec((1, tm, SSD_PAD), tile3),
                  pl.BlockSpec((1, tm // CHUNK, BC_WIDTH, CHUNK), lambda b, j: (b, j, 0, 0)),
                  pl.BlockSpec((1, tm, BC_WIDTH), tile3),
                  pl.BlockSpec((1, DT_ROWS, tm), lambda b, j: (b, 0, j))]
    out_shape += [jax.ShapeDtypeStruct((bsz, seq, SSD_PAD), BF16),
                  jax.ShapeDtypeStruct((bsz, seq // CHUNK, BC_WIDTH, CHUNK), BF16),
                  jax.ShapeDtypeStruct((bsz, seq, BC_WIDTH), BF16),
                  jax.ShapeDtypeStruct((bsz, DT_ROWS, seq), F32)]
    nsub = INPROJ_SPLIT if tm % (INPROJ_SPLIT * max(row_len, CHUNK)) == 0 else 1
    pad_rows = 2 * SUBLANE + _conv_pieces(tm // nsub, row_len) * CONV_PIECE
    conv_scratch = pltpu.VMEM((nsub, XBC_PAD // LANE, pad_rows, LANE), F32)
    return pl.pallas_call(
        functools.partial(_inproj_kernel, ctx=ctx, tm=tm, row_len=row_len),
        grid=(bsz, nt),
        in_specs=in_specs,
        out_specs=out_specs,
        out_shape=out_shape,
        scratch_shapes=[conv_scratch, conv_scratch],
        compiler_params=pltpu.CompilerParams(
            dimension_semantics=("arbitrary", "arbitrary"),
            vmem_limit_bytes=VMEM_LIMIT),
        name="inproj_ctx" if ctx else "inproj_lat",
    )(*args)


def _ssd_kernel(xs_ref, bt_ref, cm_ref, z_ref, dt_ref,
                xsc_ref, btc_ref, cmc_ref, dtc_ref,
                dtb_ref, alog_ref, dskip_ref, gssd_ref,
                out_ref, h_ref, y_ref, r2_ref, w2_ref, ee_ref, qt_ref, *, seq, ctx_len):
    nchunk = seq // CHUNK
    nchunk_ctx = ctx_len // CHUNK

    h_ref[...] = jnp.zeros(h_ref.shape, F32)

    bias = dtb_ref[...]
    nega = -jnp.exp(alog_ref[...])
    sub_i = lax.broadcasted_iota(jnp.int32, (CHUNK, CHUNK), 0)
    lane_i = lax.broadcasted_iota(jnp.int32, (CHUNK, CHUNK), 1)
    tri = (lane_i <= sub_i, lane_i >= sub_i)
    lane_lo = lane_i < HEAD_DIM
    lane_row = lax.broadcasted_iota(jnp.int32, (1, CHUNK), 1)

    def colb(mat_t, idx):
        return jnp.broadcast_to(mat_t[:, idx:idx + 1], (CHUNK, CHUNK))

    def prepare(d, raws):
        r0 = d * DT_DIR_ROWS
        n = len(raws)
        v = jnp.concatenate(raws, axis=0) + jnp.concatenate([bias[r0:r0 + DT_DIR_ROWS]] * n, axis=0)
        dt = jnp.maximum(v, 0.0) + jnp.log1p(jnp.exp(-jnp.abs(v)))
        da = dt * jnp.concatenate([nega[r0:r0 + DT_DIR_ROWS]] * n, axis=0)
        lane_n = lax.broadcasted_iota(jnp.int32, da.shape, 1)
        cs = da
        sh = 1
        while sh < CHUNK:
            if d == 0:
                cs = cs + jnp.where(lane_n >= sh, pltpu.roll(cs, sh, axis=1), 0.0)
            else:
                cs = cs + jnp.where(lane_n < CHUNK - sh, pltpu.roll(cs, CHUNK - sh, axis=1), 0.0)
            sh *= 2
        a_end = jnp.sum(da, axis=1, keepdims=True)
        cs2 = cs * LOG2E
        r2 = cs2 - jnp.log2(dt)
        w2 = jnp.exp2(a_end * LOG2E - r2)
        ee = jnp.broadcast_to(jnp.exp(a_end), da.shape)
        return cs2, r2, w2, ee

    def store_prep(d, k0, raws):
        cs2, r2, w2, ee = prepare(d, raws)
        for i in range(len(raws)):
            rows = slice(i * DT_DIR_ROWS, (i + 1) * DT_DIR_ROWS)
            r2_ref[d, k0 + i] = r2[rows]
            w2_ref[d, k0 + i] = w2[rows]
            ee_ref[d, k0 + i] = ee[rows]
            qt_ref[d, k0 + i] = jnp.concatenate(
                [cs2[rows], jnp.zeros((CHUNK - DT_DIR_ROWS, CHUNK), F32)], axis=0).T

    def direction(d, x, btc, cc, k, row0, mode):
        with_output = mode is not None
        r2 = r2_ref[d, k]
        w2 = w2_ref[d, k]
        ee = ee_ref[d, k]
        q_t = qt_ref[d, k]

        for g in range(SSD_GROUPS):
            cg = cc[:, g * D_STATE:(g + 1) * D_STATE]
            bt = btc[g * D_STATE:(g + 1) * D_STATE, :]
            hg = h_ref[d, g]
            xg = x[:, g * GROUP_PAD:(g + 1) * GROUP_PAD]
            t0 = xg[:, 0:CHUNK]
            t1 = xg[:, CHUNK:2 * CHUNK]
            heads = [g * HEADS_PER_GROUP + r for r in range(HEADS_PER_GROUP)]
            if with_output:
                scores = _dot(cg, bt)
                hb = hg.astype(BF16)
                lhs = []
                cgf = cg.astype(F32)
                for hd in heads:
                    col = colb(q_t, hd)
                    seg = col - r2[hd:hd + 1, :]
                    m = scores * jnp.exp2(jnp.where(tri[d], seg, -jnp.inf))
                    ce = cgf * jnp.exp2(col)
                    lhs.append(jnp.concatenate([m, ce], axis=1))
                rhs0 = jnp.concatenate([t0, hb[:, 0:CHUNK]], axis=0)
                rhs1 = jnp.concatenate([t1, hb[:, CHUNK:2 * CHUNK]], axis=0)
                y01 = _dot_mixed(jnp.concatenate([lhs[0], lhs[1]], axis=0), rhs0)
                y0 = jnp.where(lane_lo, y01[0:CHUNK], y01[CHUNK:2 * CHUNK])
                y1 = _dot_mixed(lhs[2], rhs1)
                y_g = jnp.concatenate([y0, y1], axis=1)
                if mode:
                    y_g = y_g + y_ref[pl.ds(row0, CHUNK), g * GROUP_PAD:(g + 1) * GROUP_PAD]
                y_ref[pl.ds(row0, CHUNK), g * GROUP_PAD:(g + 1) * GROUP_PAD] = y_g
            btf = bt.astype(F32)
            btw = [(btf * w2[hd:hd + 1, :]).astype(BF16) for hd in heads]
            s01 = _dot(jnp.concatenate([btw[0], btw[1]], axis=0), t0)
            s0 = jnp.where(lane_lo, s01[0:D_STATE], s01[D_STATE:2 * D_STATE])
            s1 = _dot(btw[2], t1)
            er = [ee[hd:hd + 1, :] for hd in heads]
            e3 = jnp.concatenate([jnp.where(lane_row < HEAD_DIM, er[0], er[1]), er[2]], axis=1)
            h_ref[d, g] = e3 * hg + jnp.concatenate([s0, s1], axis=1)

    for d in range(2):
        rows_d = slice(d * DT_DIR_ROWS, (d + 1) * DT_DIR_ROWS)
        store_prep(d, 0, [dtc_ref[0, rows_d, c * CHUNK:(c + 1) * CHUNK] for c in range(nchunk_ctx)])
        store_prep(d, nchunk_ctx, [dt_ref[0, rows_d, c * CHUNK:(c + 1) * CHUNK] for c in range(nchunk)])

    for i in range(nchunk_ctx):
        for d in range(2):
            ci = i if d == 0 else nchunk_ctx - 1 - i
            rows = slice(ci * CHUNK, (ci + 1) * CHUNK)
            direction(d, xsc_ref[0, rows, :], btc_ref[0, ci], cmc_ref[0, rows, :], ci, 0, None)

    def finish(row0):
        xr = xs_ref[0, pl.ds(row0, CHUNK), :].astype(F32)
        zr = z_ref[0, pl.ds(row0, CHUNK), :].astype(F32)
        y = y_ref[pl.ds(row0, CHUNK), :] + xr * dskip_ref[...]
        y = y * _silu(zr)
        out_ref[0, pl.ds(row0, CHUNK), :] = _rms(y, gssd_ref[...], n=SSD_WIDTH).astype(BF16)

    def step(i, carry, *, accumulate):
        row0s = []
        for d in range(2):
            ci = i if d == 0 else nchunk - 1 - i
            row0 = pl.multiple_of(ci * CHUNK, CHUNK)
            row0s.append(row0)
            direction(d, xs_ref[0, pl.ds(row0, CHUNK), :], bt_ref[0, ci],
                      cm_ref[0, pl.ds(row0, CHUNK), :], nchunk_ctx + ci, row0, accumulate)
        if accumulate:
            for row0 in row0s:
                finish(row0)
        return carry

    half = nchunk // 2
    lax.fori_loop(0, half, functools.partial(step, accumulate=False), 0, unroll=2)
    lax.fori_loop(half, nchunk, functools.partial(step, accumulate=True), 0, unroll=2)


def _ssd_call(xs, bm, cm, z, dt, xsc, bmc, cmc, dtc, dtb, alog, dskip, gssd):
    bsz, seq, _ = xs.shape
    ctx_len = xsc.shape[1]
    nck = (seq + ctx_len) // CHUNK
    per_b = lambda b: (b, 0, 0)
    const2 = lambda b: (0, 0)
    return pl.pallas_call(
        functools.partial(_ssd_kernel, seq=seq, ctx_len=ctx_len),
        grid=(bsz,),
        in_specs=[
            pl.BlockSpec((1, seq, SSD_PAD), per_b),
            pl.BlockSpec((1, seq // CHUNK, BC_WIDTH, CHUNK), lambda b: (b, 0, 0, 0)),
            pl.BlockSpec((1, seq, BC_WIDTH), per_b),
            pl.BlockSpec((1, seq, SSD_PAD), per_b),
            pl.BlockSpec((1, DT_ROWS, seq), per_b),
            pl.BlockSpec((1, ctx_len, SSD_PAD), per_b),
            pl.BlockSpec((1, ctx_len // CHUNK, BC_WIDTH, CHUNK), lambda b: (b, 0, 0, 0)),
            pl.BlockSpec((1, ctx_len, BC_WIDTH), per_b),
            pl.BlockSpec((1, DT_ROWS, ctx_len), per_b),
            pl.BlockSpec((DT_ROWS, 1), const2),
            pl.BlockSpec((DT_ROWS, 1), const2),
            pl.BlockSpec((1, SSD_PAD), const2),
            pl.BlockSpec((1, SSD_PAD), const2),
        ],
        out_specs=pl.BlockSpec((1, seq, SSD_PAD), per_b),
        out_shape=jax.ShapeDtypeStruct((bsz, seq, SSD_PAD), BF16),
        scratch_shapes=[
            pltpu.VMEM((2, SSD_GROUPS, D_STATE, GROUP_PAD), F32),
            pltpu.VMEM((seq, SSD_PAD), F32),
            pltpu.VMEM((2, nck, DT_DIR_ROWS, CHUNK), F32),
            pltpu.VMEM((2, nck, DT_DIR_ROWS, CHUNK), F32),
            pltpu.VMEM((2, nck, DT_DIR_ROWS, CHUNK), F32),
            pltpu.VMEM((2, nck, CHUNK, CHUNK), F32),
        ],
        compiler_params=pltpu.CompilerParams(
            dimension_semantics=("arbitrary",), vmem_limit_bytes=VMEM_LIMIT),
        name="ssd_scan",
    )(xs, bm, cm, z, dt, xsc, bmc, cmc, dtc, dtb, alog, dskip, gssd)


FFN_CHUNK = 256
MIX_SPLIT = 2


def _out_ffn_kernel(x_ref, yn_ref, uu_ref, cl_ref, sl_ref, mod_ref,
                    wof_ref, woy_ref, gpm_ref, gpf_ref, gpo_ref,
                    wg_ref, wu_ref, wd_ref, out_ref):
    b = pl.program_id(1)

    def mod(k):
        return mod_ref[pl.ds(b, 1), k * D_MODEL:(k + 1) * D_MODEL]

    half = x_ref.shape[1] // MIX_SPLIT
    x1_parts, h2_parts = [], []
    for r0 in range(0, x_ref.shape[1], half):
        rows = slice(r0, r0 + half)
        yf = (_dot(cl_ref[rows, :], uu_ref[0, :, 0:F_WIDTH])
              + _dot(sl_ref[rows, :], uu_ref[0, :, F_WIDTH:2 * F_WIDTH]))
        mix = _dot(yf.astype(BF16), wof_ref[...]) + _dot(yn_ref[0, rows, :], woy_ref[...])
        x1_h = x_ref[0, rows, :] + mod(2) * _rms(mix, gpm_ref[...])
        x1_parts.append(x1_h)
        h2_parts.append((_rms(x1_h, gpf_ref[...]) * (1.0 + mod(4)) + mod(3)).astype(BF16))
    x1 = jnp.concatenate(x1_parts, axis=0)
    h2 = jnp.concatenate(h2_parts, axis=0)
    ffn = None
    for c0 in range(0, wg_ref.shape[1], FFN_CHUNK):
        gate = _dot(h2, wg_ref[:, c0:c0 + FFN_CHUNK])
        up = _dot(h2, wu_ref[:, c0:c0 + FFN_CHUNK])
        act = (_silu(gate) * up).astype(BF16)
        part = _dot(act, wd_ref[c0:c0 + FFN_CHUNK, :])
        ffn = part if ffn is None else ffn + part
    out_ref[0] = x1 + mod(5) * _rms(ffn, gpo_ref[...])


def _out_ffn_call(x, yn, uu, cl, sl, mod, wof, woy, gpm, gpf, gpo, wg, wu, wd, *, tm):
    bsz, seq, _ = x.shape
    d_ff = wg.shape[1]
    nt = seq // tm
    const2 = lambda j, b: (0, 0)
    tile3 = lambda j, b: (b, j, 0)
    single = dict(pipeline_mode=pl.Buffered(1))
    return pl.pallas_call(
        _out_ffn_kernel,
        grid=(nt, bsz),
        in_specs=[
            pl.BlockSpec((1, tm, D_MODEL), tile3),
            pl.BlockSpec((1, tm, SSD_PAD), tile3),
            pl.BlockSpec((1, seq, 2 * F_WIDTH), lambda j, b: (b, 0, 0)),
            pl.BlockSpec((tm, seq), lambda j, b: (j, 0)),
            pl.BlockSpec((tm, seq), lambda j, b: (j, 0)),
            pl.BlockSpec((MOD_ROWS, N_MOD * D_MODEL), const2),
            pl.BlockSpec((F_WIDTH, D_MODEL), const2, **single),
            pl.BlockSpec((SSD_PAD, D_MODEL), const2, **single),
            pl.BlockSpec((1, D_MODEL), const2),
            pl.BlockSpec((1, D_MODEL), const2),
            pl.BlockSpec((1, D_MODEL), const2),
            pl.BlockSpec((D_MODEL, d_ff), const2, **single),
            pl.BlockSpec((D_MODEL, d_ff), const2, **single),
            pl.BlockSpec((d_ff, D_MODEL), const2, **single),
        ],
        out_specs=pl.BlockSpec((1, tm, D_MODEL), tile3),
        out_shape=jax.ShapeDtypeStruct((bsz, seq, D_MODEL), F32),
        compiler_params=pltpu.CompilerParams(
            dimension_semantics=("arbitrary", "arbitrary"),
            vmem_limit_bytes=VMEM_LIMIT),
        name="out_ffn",
    )(x, yn, uu, cl, sl, mod, wof, woy, gpm, gpf, gpo, wg, wu, wd)


@functools.lru_cache(maxsize=None)
def _dft_tables(seq):
    k = np.arange(seq, dtype=np.int64)
    ang = 2.0 * np.pi * ((k[:, None] * k[None, :]) % seq).astype(np.float64) / seq
    scale = 1.0 / math.sqrt(seq)
    cl = (np.cos(ang) * scale).astype(np.float32)
    sl = (np.sin(ang) * scale).astype(np.float32)
    c = np.arange(FGROUP_DIM, dtype=np.int64)
    angc = 2.0 * np.pi * ((c[:, None] * c[None, :]) % FGROUP_DIM).astype(np.float64) / FGROUP_DIM
    sc = 1.0 / math.sqrt(FGROUP_DIM)
    wc = np.zeros((F_WIDTH, 2 * F_WIDTH), np.float32)
    for g in range(N_FGROUPS):
        s = slice(g * FGROUP_DIM, (g + 1) * FGROUP_DIM)
        wc[s, g * FGROUP_DIM:(g + 1) * FGROUP_DIM] = np.cos(angc) * sc
        wc[s, F_WIDTH + g * FGROUP_DIM:F_WIDTH + (g + 1) * FGROUP_DIM] = -np.sin(angc) * sc
    return cl, sl, wc


def _pad_groups(a, axis):
    a = jnp.moveaxis(a, axis, -1)
    lead = a.shape[:-1]
    a = a.reshape(lead + (SSD_GROUPS, HEADS_PER_GROUP * HEAD_DIM))
    a = jnp.pad(a, [(0, 0)] * len(lead) + [(0, 0), (0, GROUP_PAD - HEADS_PER_GROUP * HEAD_DIM)])
    a = a.reshape(lead + (SSD_PAD,))
    return jnp.moveaxis(a, -1, axis)


def _pad_dirs(a):
    a = jnp.pad(a, ((0, 0), (0, DT_DIR_ROWS - SSD_HEADS)))
    return a.reshape(DT_ROWS, 1)


def kernel(x, c, ctx, c_ctx, w_ada, b_ada, g_pre_mix, g_post_mix, g_pre_ffn, g_post_ffn,
           w_in, conv_w, conv_b, dt_bias, a_log, d_skip, g_ssd, w_out, w_gate, w_up, w_down):
    bsz, seq, _ = x.shape
    ctx_len = ctx.shape[1]
    l = 0
    cl_np, sl_np, wc_np = _dft_tables(seq)
    cl = jnp.asarray(cl_np).astype(BF16)
    sl = jnp.asarray(sl_np).astype(BF16)
    wc = jnp.asarray(wc_np)

    cc = jnp.concatenate(
        [c, c_ctx[None, :], jnp.zeros((MOD_ROWS - bsz - 1, D_MODEL), F32)], axis=0)
    mod = _ada_call(cc, w_ada, b_ada[l][None, :], l)

    wi = w_in[l]
    w_cat = jnp.concatenate([
        wi[:, :F_WIDTH],
        _pad_groups(wi[:, F_WIDTH:XBC_OFF], 1),
        _pad_groups(wi[:, XBC_OFF:XBC_OFF + SSD_WIDTH], 1),
        wi[:, XBC_OFF + SSD_WIDTH:XBC_OFF + CONV_DIM],
    ], axis=1).astype(BF16)
    w_dt = wi[:, XBC_OFF + CONV_DIM:].T.reshape(2, SSD_HEADS, D_MODEL)
    w_dt = jnp.pad(w_dt, ((0, 0), (0, DT_DIR_ROWS - SSD_HEADS), (0, 0))).reshape(DT_ROWS, D_MODEL)
    w_dt = w_dt.astype(BF16)
    cw = conv_w[l].T
    cw = jnp.concatenate([_pad_groups(cw[:, :SSD_WIDTH], 1), cw[:, SSD_WIDTH:]], axis=1)
    cb = conv_b[l][None, :]
    cb = jnp.concatenate([_pad_groups(cb[:, :SSD_WIDTH], 1), cb[:, SSD_WIDTH:]], axis=1)
    g_pre = g_pre_mix[l][None, :]

    xs_c, bm_c, cm_c, dt_c = _inproj_call(
        ctx, mod, g_pre, w_cat, w_dt, cw, cb, None, ctx=True, tm=ctx_len, row_len=ctx_len)
    uu, z, xs, bm, cm, dt = _inproj_call(
        x, mod, g_pre, w_cat, w_dt, cw, cb, wc, ctx=False, tm=512, row_len=GRID_W)

    dskip = _pad_groups(jnp.repeat(d_skip[l], HEAD_DIM)[None, :], 1)
    gssd = _pad_groups(g_ssd[l][None, :], 1)
    yn = _ssd_call(xs, bm, cm, z, dt, xs_c, bm_c, cm_c, dt_c,
                   _pad_dirs(dt_bias[l]), _pad_dirs(a_log[l]), dskip, gssd)

    wo = w_out[l]
    wof = wo[:F_WIDTH].astype(BF16)
    woy = _pad_groups(wo[F_WIDTH:], 0).astype(BF16)
    return _out_ffn_call(
        x, yn, uu, cl, sl, mod, wof, woy,
        g_post_mix[l][None, :], g_pre_ffn[l][None, :], g_post_ffn[l][None, :],
        w_gate[l].astype(BF16), w_up[l].astype(BF16), w_down[l].astype(BF16), tm=512)
```

```python
import functools
import math

import jax
import jax.numpy as jnp
import numpy as np
from jax import lax
from jax.experimental import pallas as pl
from jax.experimental.pallas import tpu as pltpu

F32 = jnp.float32
BF16 = jnp.bfloat16

D_MODEL = 1024
GRID_W = 64
F_WIDTH = 256
N_FGROUPS = 4
FGROUP_DIM = F_WIDTH // N_FGROUPS
SSD_WIDTH = 768
HEAD_DIM = 64
SSD_HEADS = 12
SSD_GROUPS = 4
HEADS_PER_GROUP = 3
D_STATE = 128
BC_WIDTH = SSD_GROUPS * D_STATE
D_CONV = 7
HALF_CONV = D_CONV // 2
CHUNK = 128
XBC_OFF = F_WIDTH + SSD_WIDTH
CONV_DIM = SSD_WIDTH + 2 * BC_WIDTH
N_MOD = 6
EPS = 1e-6
LOG2E = math.log2(math.e)

GROUP_PAD = 256
SSD_PAD = SSD_GROUPS * GROUP_PAD
XBC_PAD = SSD_PAD + 2 * BC_WIDTH
W_COLS = F_WIDTH + SSD_PAD + XBC_PAD
CTX_COL0 = F_WIDTH + SSD_PAD
DT_ROWS = 32
DT_DIR_ROWS = 16
MOD_ROWS = 16
MOD_CTX_ROW = 8

SUBLANE = 8
LANE = 128
VMEM_LIMIT = 56 * 1024 * 1024


def _dot(a, b):
    return jnp.dot(a, b, preferred_element_type=F32)


def _dot_mixed(a_f32, b_bf16):
    return lax.dot_general(a_f32, b_bf16, (((1,), (0,)), ((), ())), preferred_element_type=F32)


def _silu(v):
    return v * jax.nn.sigmoid(v)


def _rms(v, g, n=None):
    n = v.shape[-1] if n is None else n
    ms = jnp.sum(v * v, axis=-1, keepdims=True) * (1.0 / n)
    return v * lax.rsqrt(ms + EPS) * g


def _ada_kernel(c_ref, w_ref, b_ref, o_ref):
    s = _silu(c_ref[...]).astype(BF16)
    o_ref[...] = _dot(s, w_ref[...].astype(BF16)) + b_ref[...]


def _ada_call(cc, w_ada, b_ada, layer):
    n = w_ada.shape[2]
    tn = 768
    return pl.pallas_call(
        _ada_kernel,
        grid=(n // tn,),
        in_specs=[
            pl.BlockSpec((MOD_ROWS, D_MODEL), lambda j: (0, 0)),
            pl.BlockSpec((None, D_MODEL, tn), lambda j: (layer, 0, j)),
            pl.BlockSpec((1, tn), lambda j: (0, j)),
        ],
        out_specs=pl.BlockSpec((MOD_ROWS, tn), lambda j: (0, j)),
        out_shape=jax.ShapeDtypeStruct((MOD_ROWS, n), F32),
        compiler_params=pltpu.CompilerParams(dimension_semantics=("arbitrary",)),
        name="ada_mod",
    )(cc, w_ada, b_ada)


CONV_STRIDE = SUBLANE + 1
CONV_PIECE = SUBLANE * CONV_STRIDE
INPROJ_SPLIT = 1


def _conv_pieces(tm, row_len):
    padded = (tm // row_len) * (row_len + SUBLANE)
    return -(-padded // CONV_PIECE)


def _inproj_kernel(*refs, ctx, tm, row_len):
    if ctx:
        (x_ref, mod_ref, g_ref, w_ref, wdt_ref, cw_ref, cb_ref,
         xs_ref, bt_ref, cm_ref, dt_ref, pad_ref, cv_ref) = refs
    else:
        (x_ref, mod_ref, g_ref, w_ref, wdt_ref, cw_ref, cb_ref, wc_ref,
         uu_ref, z_ref, xs_ref, bt_ref, cm_ref, dt_ref, pad_ref, cv_ref) = refs

    if ctx:
        shift = mod_ref[MOD_CTX_ROW:MOD_CTX_ROW + 1, 0:D_MODEL]
        scale = mod_ref[MOD_CTX_ROW:MOD_CTX_ROW + 1, D_MODEL:2 * D_MODEL]
    else:
        b = pl.program_id(0)
        shift = mod_ref[pl.ds(b, 1), 0:D_MODEL]
        scale = mod_ref[pl.ds(b, 1), D_MODEL:2 * D_MODEL]

    nsub = pad_ref.shape[0]
    tms = tm // nsub
    pitch = row_len + SUBLANE
    nrow = tms // row_len
    npiece = _conv_pieces(tms, row_len)
    data_end = SUBLANE + nrow * pitch
    alloc_end = pad_ref.shape[2]
    zrow = jnp.zeros((SUBLANE, LANE), F32)

    for sub in range(nsub):
        r0 = sub * tms
        xt = x_ref[0, r0:r0 + tms, :]
        h = _rms(xt, g_ref[...]) * (1.0 + scale) + shift
        hb = h.astype(BF16)

        if not ctx:
            uf = _dot(hb, w_ref[:, 0:F_WIDTH])
            uu_ref[0, r0:r0 + tms, :] = _dot(uf.astype(BF16), wc_ref[...].astype(BF16)).astype(BF16)
            z_ref[0, r0:r0 + tms, :] = _dot(hb, w_ref[:, F_WIDTH:CTX_COL0]).astype(BF16)
        xbc = _dot(hb, w_ref[:, CTX_COL0:W_COLS])
        dt_ref[0, :, r0:r0 + tms] = lax.dot_general(wdt_ref[...], hb, (((1,), (1,)), ((), ())),
                                                    preferred_element_type=F32)

        for j in range(XBC_PAD // LANE):
            pad_ref[sub, j, 0:SUBLANE, :] = zrow
            for r in range(nrow):
                base = SUBLANE + r * pitch
                pad_ref[sub, j, base:base + row_len, :] = xbc[r * row_len:(r + 1) * row_len,
                                                              j * LANE:(j + 1) * LANE]
                pad_ref[sub, j, base + row_len:base + pitch, :] = zrow
            for z0 in range(data_end, alloc_end, SUBLANE):
                pad_ref[sub, j, z0:z0 + SUBLANE, :] = zrow

        for j in range(XBC_PAD // LANE):
            wk = [jnp.broadcast_to(cw_ref[k:k + 1, j * LANE:(j + 1) * LANE], (SUBLANE, LANE))
                  for k in range(D_CONV)]
            bias = jnp.broadcast_to(cb_ref[:, j * LANE:(j + 1) * LANE], (SUBLANE, LANE))
            for p in range(npiece):
                base = SUBLANE + p * CONV_PIECE
                wins = {v: pad_ref[sub, j, pl.ds(base + v, SUBLANE, stride=CONV_STRIDE), :]
                        for v in range(-HALF_CONV, CONV_STRIDE + HALF_CONV)}
                for a in range(CONV_STRIDE):
                    acc = bias
                    for k in range(D_CONV):
                        acc = acc + wins[a + k - HALF_CONV] * wk[k]
                    cv_ref[sub, j, pl.ds(base + a, SUBLANE, stride=CONV_STRIDE), :] = acc

        def conv_rows(j, t0, n):
            parts = []
            for t in range(t0, t0 + n, min(n, row_len)):
                src = SUBLANE + (t // row_len) * pitch + t % row_len
                parts.append(cv_ref[sub, j, src:src + min(n, row_len), :])
            return _silu(parts[0] if len(parts) == 1 else jnp.concatenate(parts, axis=0))

        for j in range(XBC_PAD // LANE):
            c0 = j * LANE
            for ci in range(tms // CHUNK):
                val = conv_rows(j, ci * CHUNK, CHUNK)
                rows = slice(r0 + ci * CHUNK, r0 + (ci + 1) * CHUNK)
                if c0 < SSD_PAD:
                    xs_ref[0, rows, c0:c0 + LANE] = val.astype(BF16)
                elif c0 < SSD_PAD + BC_WIDTH:
                    cc = c0 - SSD_PAD
                    bt_ref[0, r0 // CHUNK + ci, cc:cc + LANE, :] = val.T.astype(BF16)
                else:
                    cc = c0 - SSD_PAD - BC_WIDTH
                    cm_ref[0, rows, cc:cc + LANE] = val.astype(BF16)


def _inproj_call(xin, mod, g, w, wdt, cw, cb, wc, *, ctx, tm, row_len):
    bsz, seq, _ = xin.shape
    nt = seq // tm
    const2 = lambda b, j: (0, 0)
    tile3 = lambda b, j: (b, j, 0)
    in_specs = [
        pl.BlockSpec((1, tm, D_MODEL), tile3),
        pl.BlockSpec((MOD_ROWS, N_MOD * D_MODEL), const2),
        pl.BlockSpec((1, D_MODEL), const2),
        pl.BlockSpec((D_MODEL, W_COLS), const2),
        pl.BlockSpec((DT_ROWS, D_MODEL), const2),
        pl.BlockSpec((D_CONV, XBC_PAD), const2),
        pl.BlockSpec((1, XBC_PAD), const2),
    ]
    args = [xin, mod, g, w, wdt, cw, cb]
    out_specs = []
    out_shape = []
    if not ctx:
        in_specs.append(pl.BlockSpec((F_WIDTH, 2 * F_WIDTH), const2))
        args.append(wc)
        out_specs += [pl.BlockSpec((1, tm, 2 * F_WIDTH), tile3),
                      pl.BlockSpec((1, tm, SSD_PAD), tile3)]
        out_shape += [jax.ShapeDtypeStruct((bsz, seq, 2 * F_WIDTH), BF16),
                      jax.ShapeDtypeStruct((bsz, seq, SSD_PAD), BF16)]
    out_specs += [pl.BlockSpec((1, tm, SSD_PAD), tile3),
                  pl.BlockSpec((1, tm // CHUNK, BC_WIDTH, CHUNK), lambda b, j: (b, j, 0, 0)),
                  pl.BlockSpec((1, tm, BC_WIDTH), tile3),
                  pl.BlockSpec((1, DT_ROWS, tm), lambda b, j: (b, 0, j))]
    out_shape += [jax.ShapeDtypeStruct((bsz, seq, SSD_PAD), BF16),
                  jax.ShapeDtypeStruct((bsz, seq // CHUNK, BC_WIDTH, CHUNK), BF16),
                  jax.ShapeDtypeStruct((bsz, seq, BC_WIDTH), BF16),
                  jax.ShapeDtypeStruct((bsz, DT_ROWS, seq), F32)]
    nsub = INPROJ_SPLIT if tm % (INPROJ_SPLIT * max(row_len, CHUNK)) == 0 else 1
    pad_rows = 2 * SUBLANE + _conv_pieces(tm // nsub, row_len) * CONV_PIECE
    conv_scratch = pltpu.VMEM((nsub, XBC_PAD // LANE, pad_rows, LANE), F32)
    return pl.pallas_call(
        functools.partial(_inproj_kernel, ctx=ctx, tm=tm, row_len=row_len),
        grid=(bsz, nt),
        in_specs=in_specs,
        out_specs=out_specs,
        out_shape=out_shape,
        scratch_shapes=[conv_scratch, conv_scratch],
        compiler_params=pltpu.CompilerParams(
            dimension_semantics=("arbitrary", "arbitrary"),
            vmem_limit_bytes=VMEM_LIMIT),
        name="inproj_ctx" if ctx else "inproj_lat",
    )(*args)


def _ssd_kernel(xs_ref, bt_ref, cm_ref, z_ref, dt_ref,
                xsc_ref, btc_ref, cmc_ref, dtc_ref,
                dtb_ref, alog_ref, dskip_ref, gssd_ref,
                out_ref, h_ref, y_ref, r2_ref, w2_ref, ee_ref, qt_ref, *, seq, ctx_len):
    nchunk = seq // CHUNK
    nchunk_ctx = ctx_len // CHUNK

    h_ref[...] = jnp.zeros(h_ref.shape, F32)

    bias = dtb_ref[...]
    nega = -jnp.exp(alog_ref[...])
    sub_i = lax.broadcasted_iota(jnp.int32, (CHUNK, CHUNK), 0)
    lane_i = lax.broadcasted_iota(jnp.int32, (CHUNK, CHUNK), 1)
    tri = (lane_i <= sub_i, lane_i >= sub_i)
    lane_lo = lane_i < HEAD_DIM
    lane_row = lax.broadcasted_iota(jnp.int32, (1, CHUNK), 1)

    def colb(mat_t, idx):
        return jnp.broadcast_to(mat_t[:, idx:idx + 1], (CHUNK, CHUNK))

    def prepare(d, raws):
        r0 = d * DT_DIR_ROWS
        n = len(raws)
        v = jnp.concatenate(raws, axis=0) + jnp.concatenate([bias[r0:r0 + DT_DIR_ROWS]] * n, axis=0)
        dt = jnp.maximum(v, 0.0) + jnp.log1p(jnp.exp(-jnp.abs(v)))
        da = dt * jnp.concatenate([nega[r0:r0 + DT_DIR_ROWS]] * n, axis=0)
        lane_n = lax.broadcasted_iota(jnp.int32, da.shape, 1)
        cs = da
        sh = 1
        while sh < CHUNK:
            if d == 0:
                cs = cs + jnp.where(lane_n >= sh, pltpu.roll(cs, sh, axis=1), 0.0)
            else:
                cs = cs + jnp.where(lane_n < CHUNK - sh, pltpu.roll(cs, CHUNK - sh, axis=1), 0.0)
            sh *= 2
        a_end = jnp.sum(da, axis=1, keepdims=True)
        cs2 = cs * LOG2E
        r2 = cs2 - jnp.log2(dt)
        w2 = jnp.exp2(a_end * LOG2E - r2)
        ee = jnp.broadcast_to(jnp.exp(a_end), da.shape)
        return cs2, r2, w2, ee

    def store_prep(d, k0, raws):
        cs2, r2, w2, ee = prepare(d, raws)
        for i in range(len(raws)):
            rows = slice(i * DT_DIR_ROWS, (i + 1) * DT_DIR_ROWS)
            r2_ref[d, k0 + i] = r2[rows]
            w2_ref[d, k0 + i] = w2[rows]
            ee_ref[d, k0 + i] = ee[rows]
            qt_ref[d, k0 + i] = jnp.concatenate(
                [cs2[rows], jnp.zeros((CHUNK - DT_DIR_ROWS, CHUNK), F32)], axis=0).T

    def direction(d, x, btc, cc, k, row0, mode):
        with_output = mode is not None
        r2 = r2_ref[d, k]
        w2 = w2_ref[d, k]
        ee = ee_ref[d, k]
        q_t = qt_ref[d, k]
        if with_output:
            scores_all = [_dot(cc[:, g * D_STATE:(g + 1) * D_STATE],
                               btc[g * D_STATE:(g + 1) * D_STATE, :]) for g in range(SSD_GROUPS)]

        s_all = []
        for g in range(SSD_GROUPS):
            bt = btc[g * D_STATE:(g + 1) * D_STATE, :]
            xg = x[:, g * GROUP_PAD:(g + 1) * GROUP_PAD]
            heads = [g * HEADS_PER_GROUP + r for r in range(HEADS_PER_GROUP)]
            btf = bt.astype(F32)
            btw = [(btf * w2[hd:hd + 1, :]).astype(BF16) for hd in heads]
            s01 = _dot(jnp.concatenate([btw[0], btw[1]], axis=0), xg[:, 0:CHUNK])
            s0 = jnp.where(lane_lo, s01[0:D_STATE], s01[D_STATE:2 * D_STATE])
            s1 = _dot(btw[2], xg[:, CHUNK:2 * CHUNK])
            s_all.append(jnp.concatenate([s0, s1], axis=1))

        for g in range(SSD_GROUPS):
            cg = cc[:, g * D_STATE:(g + 1) * D_STATE]
            hg = h_ref[d, g]
            xg = x[:, g * GROUP_PAD:(g + 1) * GROUP_PAD]
            t0 = xg[:, 0:CHUNK]
            t1 = xg[:, CHUNK:2 * CHUNK]
            heads = [g * HEADS_PER_GROUP + r for r in range(HEADS_PER_GROUP)]
            if with_output:
                scores = scores_all[g]
                hb = hg.astype(BF16)
                lhs = []
                cgf = cg.astype(F32)
                for hd in heads:
                    col = colb(q_t, hd)
                    seg = col - r2[hd:hd + 1, :]
                    m = scores * jnp.exp2(jnp.where(tri[d], seg, -jnp.inf))
                    ce = cgf * jnp.exp2(col)
                    lhs.append(jnp.concatenate([m, ce], axis=1))
                rhs0 = jnp.concatenate([t0, hb[:, 0:CHUNK]], axis=0)
                rhs1 = jnp.concatenate([t1, hb[:, CHUNK:2 * CHUNK]], axis=0)
                y01 = _dot_mixed(jnp.concatenate([lhs[0], lhs[1]], axis=0), rhs0)
                y0 = jnp.where(lane_lo, y01[0:CHUNK], y01[CHUNK:2 * CHUNK])
                y1 = _dot_mixed(lhs[2], rhs1)
                y_g = jnp.concatenate([y0, y1], axis=1)
                if mode:
                    y_g = y_g + y_ref[pl.ds(row0, CHUNK), g * GROUP_PAD:(g + 1) * GROUP_PAD]
                y_ref[pl.ds(row0, CHUNK), g * GROUP_PAD:(g + 1) * GROUP_PAD] = y_g
            er = [ee[hd:hd + 1, :] for hd in heads]
            e3 = jnp.concatenate([jnp.where(lane_row < HEAD_DIM, er[0], er[1]), er[2]], axis=1)
            h_ref[d, g] = e3 * hg + s_all[g]

    for d in range(2):
        rows_d = slice(d * DT_DIR_ROWS, (d + 1) * DT_DIR_ROWS)
        store_prep(d, 0, [dtc_ref[0, rows_d, c * CHUNK:(c + 1) * CHUNK] for c in range(nchunk_ctx)])
        store_prep(d, nchunk_ctx, [dt_ref[0, rows_d, c * CHUNK:(c + 1) * CHUNK] for c in range(nchunk)])

    for i in range(nchunk_ctx):
        for d in range(2):
            ci = i if d == 0 else nchunk_ctx - 1 - i
            rows = slice(ci * CHUNK, (ci + 1) * CHUNK)
            direction(d, xsc_ref[0, rows, :], btc_ref[0, ci], cmc_ref[0, rows, :], ci, 0, None)

    def finish(row0):
        xr = xs_ref[0, pl.ds(row0, CHUNK), :].astype(F32)
        zr = z_ref[0, pl.ds(row0, CHUNK), :].astype(F32)
        y = y_ref[pl.ds(row0, CHUNK), :] + xr * dskip_ref[...]
        y = y * _silu(zr)
        out_ref[0, pl.ds(row0, CHUNK), :] = _rms(y, gssd_ref[...], n=SSD_WIDTH).astype(BF16)

    def step(i, carry, *, accumulate):
        row0s = []
        for d in range(2):
            ci = i if d == 0 else nchunk - 1 - i
            row0 = pl.multiple_of(ci * CHUNK, CHUNK)
            row0s.append(row0)
            direction(d, xs_ref[0, pl.ds(row0, CHUNK), :], bt_ref[0, ci],
                      cm_ref[0, pl.ds(row0, CHUNK), :], nchunk_ctx + ci, row0, accumulate)
        if accumulate:
            for row0 in row0s:
                finish(row0)
        return carry

    half = nchunk // 2
    lax.fori_loop(0, half, functools.partial(step, accumulate=False), 0, unroll=2)
    lax.fori_loop(half, nchunk, functools.partial(step, accumulate=True), 0, unroll=2)


def _ssd_call(xs, bm, cm, z, dt, xsc, bmc, cmc, dtc, dtb, alog, dskip, gssd):
    bsz, seq, _ = xs.shape
    ctx_len = xsc.shape[1]
    nck = (seq + ctx_len) // CHUNK
    per_b = lambda b: (b, 0, 0)
    const2 = lambda b: (0, 0)
    return pl.pallas_call(
        functools.partial(_ssd_kernel, seq=seq, ctx_len=ctx_len),
        grid=(bsz,),
        in_specs=[
            pl.BlockSpec((1, seq, SSD_PAD), per_b),
            pl.BlockSpec((1, seq // CHUNK, BC_WIDTH, CHUNK), lambda b: (b, 0, 0, 0)),
            pl.BlockSpec((1, seq, BC_WIDTH), per_b),
            pl.BlockSpec((1, seq, SSD_PAD), per_b),
            pl.BlockSpec((1, DT_ROWS, seq), per_b),
            pl.BlockSpec((1, ctx_len, SSD_PAD), per_b),
            pl.BlockSpec((1, ctx_len // CHUNK, BC_WIDTH, CHUNK), lambda b: (b, 0, 0, 0)),
            pl.BlockSpec((1, ctx_len, BC_WIDTH), per_b),
            pl.BlockSpec((1, DT_ROWS, ctx_len), per_b),
            pl.BlockSpec((DT_ROWS, 1), const2),
            pl.BlockSpec((DT_ROWS, 1), const2),
            pl.BlockSpec((1, SSD_PAD), const2),
            pl.BlockSpec((1, SSD_PAD), const2),
        ],
        out_specs=pl.BlockSpec((1, seq, SSD_PAD), per_b),
        out_shape=jax.ShapeDtypeStruct((bsz, seq, SSD_PAD), BF16),
        scratch_shapes=[
            pltpu.VMEM((2, SSD_GROUPS, D_STATE, GROUP_PAD), F32),
            pltpu.VMEM((seq, SSD_PAD), F32),
            pltpu.VMEM((2, nck, DT_DIR_ROWS, CHUNK), F32),
            pltpu.VMEM((2, nck, DT_DIR_ROWS, CHUNK), F32),
            pltpu.VMEM((2, nck, DT_DIR_ROWS, CHUNK), F32),
            pltpu.VMEM((2, nck, CHUNK, CHUNK), F32),
        ],
        compiler_params=pltpu.CompilerParams(
            dimension_semantics=("arbitrary",), vmem_limit_bytes=VMEM_LIMIT),
        name="ssd_scan",
    )(xs, bm, cm, z, dt, xsc, bmc, cmc, dtc, dtb, alog, dskip, gssd)


FFN_CHUNK = 256
MIX_SPLIT = 2


def _out_ffn_kernel(x_ref, yn_ref, uu_ref, cl_ref, sl_ref, mod_ref,
                    wof_ref, woy_ref, gpm_ref, gpf_ref, gpo_ref,
                    wg_ref, wu_ref, wd_ref, out_ref):
    b = pl.program_id(1)

    def mod(k):
        return mod_ref[pl.ds(b, 1), k * D_MODEL:(k + 1) * D_MODEL]

    half = x_ref.shape[1] // MIX_SPLIT
    x1_parts, h2_parts = [], []
    for r0 in range(0, x_ref.shape[1], half):
        rows = slice(r0, r0 + half)
        yf = (_dot(cl_ref[rows, :], uu_ref[0, :, 0:F_WIDTH])
              + _dot(sl_ref[rows, :], uu_ref[0, :, F_WIDTH:2 * F_WIDTH]))
        mix = _dot(yf.astype(BF16), wof_ref[...]) + _dot(yn_ref[0, rows, :], woy_ref[...])
        x1_h = x_ref[0, rows, :] + mod(2) * _rms(mix, gpm_ref[...])
        x1_parts.append(x1_h)
        h2_parts.append((_rms(x1_h, gpf_ref[...]) * (1.0 + mod(4)) + mod(3)).astype(BF16))
    x1 = jnp.concatenate(x1_parts, axis=0)
    h2 = jnp.concatenate(h2_parts, axis=0)
    ffn = None
    for c0 in range(0, wg_ref.shape[1], FFN_CHUNK):
        gate = _dot(h2, wg_ref[:, c0:c0 + FFN_CHUNK])
        up = _dot(h2, wu_ref[:, c0:c0 + FFN_CHUNK])
        act = (_silu(gate) * up).astype(BF16)
        part = _dot(act, wd_ref[c0:c0 + FFN_CHUNK, :])
        ffn = part if ffn is None else ffn + part
    out_ref[0] = x1 + mod(5) * _rms(ffn, gpo_ref[...])


def _out_ffn_call(x, yn, uu, cl, sl, mod, wof, woy, gpm, gpf, gpo, wg, wu, wd, *, tm):
    bsz, seq, _ = x.shape
    d_ff = wg.shape[1]
    nt = seq // tm
    const2 = lambda j, b: (0, 0)
    tile3 = lambda j, b: (b, j, 0)
    single = dict(pipeline_mode=pl.Buffered(1))
    return pl.pallas_call(
        _out_ffn_kernel,
        grid=(nt, bsz),
        in_specs=[
            pl.BlockSpec((1, tm, D_MODEL), tile3),
            pl.BlockSpec((1, tm, SSD_PAD), tile3),
            pl.BlockSpec((1, seq, 2 * F_WIDTH), lambda j, b: (b, 0, 0)),
            pl.BlockSpec((tm, seq), lambda j, b: (j, 0)),
            pl.BlockSpec((tm, seq), lambda j, b: (j, 0)),
            pl.BlockSpec((MOD_ROWS, N_MOD * D_MODEL), const2),
            pl.BlockSpec((F_WIDTH, D_MODEL), const2, **single),
            pl.BlockSpec((SSD_PAD, D_MODEL), const2, **single),
            pl.BlockSpec((1, D_MODEL), const2),
            pl.BlockSpec((1, D_MODEL), const2),
            pl.BlockSpec((1, D_MODEL), const2),
            pl.BlockSpec((D_MODEL, d_ff), const2, **single),
            pl.BlockSpec((D_MODEL, d_ff), const2, **single),
            pl.BlockSpec((d_ff, D_MODEL), const2, **single),
        ],
        out_specs=pl.BlockSpec((1, tm, D_MODEL), tile3),
        out_shape=jax.ShapeDtypeStruct((bsz, seq, D_MODEL), F32),
        compiler_params=pltpu.CompilerParams(
            dimension_semantics=("arbitrary", "arbitrary"),
            vmem_limit_bytes=VMEM_LIMIT),
        name="out_ffn",
    )(x, yn, uu, cl, sl, mod, wof, woy, gpm, gpf, gpo, wg, wu, wd)


@functools.lru_cache(maxsize=None)
def _dft_tables(seq):
    k = np.arange(seq, dtype=np.int64)
    ang = 2.0 * np.pi * ((k[:, None] * k[None, :]) % seq).astype(np.float64) / seq
    scale = 1.0 / math.sqrt(seq)
    cl = (np.cos(ang) * scale).astype(np.float32)
    sl = (np.sin(ang) * scale).astype(np.float32)
    c = np.arange(FGROUP_DIM, dtype=np.int64)
    angc = 2.0 * np.pi * ((c[:, None] * c[None, :]) % FGROUP_DIM).astype(np.float64) / FGROUP_DIM
    sc = 1.0 / math.sqrt(FGROUP_DIM)
    wc = np.zeros((F_WIDTH, 2 * F_WIDTH), np.float32)
    for g in range(N_FGROUPS):
        s = slice(g * FGROUP_DIM, (g + 1) * FGROUP_DIM)
        wc[s, g * FGROUP_DIM:(g + 1) * FGROUP_DIM] = np.cos(angc) * sc
        wc[s, F_WIDTH + g * FGROUP_DIM:F_WIDTH + (g + 1) * FGROUP_DIM] = -np.sin(angc) * sc
    return cl, sl, wc


def _pad_groups(a, axis):
    a = jnp.moveaxis(a, axis, -1)
    lead = a.shape[:-1]
    a = a.reshape(lead + (SSD_GROUPS, HEADS_PER_GROUP * HEAD_DIM))
    a = jnp.pad(a, [(0, 0)] * len(lead) + [(0, 0), (0, GROUP_PAD - HEADS_PER_GROUP * HEAD_DIM)])
    a = a.reshape(lead + (SSD_PAD,))
    return jnp.moveaxis(a, -1, axis)


def _pad_dirs(a):
    a = jnp.pad(a, ((0, 0), (0, DT_DIR_ROWS - SSD_HEADS)))
    return a.reshape(DT_ROWS, 1)


def kernel(x, c, ctx, c_ctx, w_ada, b_ada, g_pre_mix, g_post_mix, g_pre_ffn, g_post_ffn,
           w_in, conv_w, conv_b, dt_bias, a_log, d_skip, g_ssd, w_out, w_gate, w_up, w_down):
    bsz, seq, _ = x.shape
    ctx_len = ctx.shape[1]
    l = 0
    cl_np, sl_np, wc_np = _dft_tables(seq)
    cl = jnp.asarray(cl_np).astype(BF16)
    sl = jnp.asarray(sl_np).astype(BF16)
    wc = jnp.asarray(wc_np)

    cc = jnp.concatenate(
        [c, c_ctx[None, :], jnp.zeros((MOD_ROWS - bsz - 1, D_MODEL), F32)], axis=0)
    mod = _ada_call(cc, w_ada, b_ada[l][None, :], l)

    wi = w_in[l]
    w_cat = jnp.concatenate([
        wi[:, :F_WIDTH],
        _pad_groups(wi[:, F_WIDTH:XBC_OFF], 1),
        _pad_groups(wi[:, XBC_OFF:XBC_OFF + SSD_WIDTH], 1),
        wi[:, XBC_OFF + SSD_WIDTH:XBC_OFF + CONV_DIM],
    ], axis=1).astype(BF16)
    w_dt = wi[:, XBC_OFF + CONV_DIM:].T.reshape(2, SSD_HEADS, D_MODEL)
    w_dt = jnp.pad(w_dt, ((0, 0), (0, DT_DIR_ROWS - SSD_HEADS), (0, 0))).reshape(DT_ROWS, D_MODEL)
    w_dt = w_dt.astype(BF16)
    cw = conv_w[l].T
    cw = jnp.concatenate([_pad_groups(cw[:, :SSD_WIDTH], 1), cw[:, SSD_WIDTH:]], axis=1)
    cb = conv_b[l][None, :]
    cb = jnp.concatenate([_pad_groups(cb[:, :SSD_WIDTH], 1), cb[:, SSD_WIDTH:]], axis=1)
    g_pre = g_pre_mix[l][None, :]

    xs_c, bm_c, cm_c, dt_c = _inproj_call(
        ctx, mod, g_pre, w_cat, w_dt, cw, cb, None, ctx=True, tm=ctx_len, row_len=ctx_len)
    uu, z, xs, bm, cm, dt = _inproj_call(
        x, mod, g_pre, w_cat, w_dt, cw, cb, wc, ctx=False, tm=512, row_len=GRID_W)

    dskip = _pad_groups(jnp.repeat(d_skip[l], HEAD_DIM)[None, :], 1)
    gssd = _pad_groups(g_ssd[l][None, :], 1)
    yn = _ssd_call(xs, bm, cm, z, dt, xs_c, bm_c, cm_c, dt_c,
                   _pad_dirs(dt_bias[l]), _pad_dirs(a_log[l]), dskip, gssd)

    wo = w_out[l]
    wof = wo[:F_WIDTH].astype(BF16)
    woy = _pad_groups(wo[F_WIDTH:], 0).astype(BF16)
    return _out_ffn_call(
        x, yn, uu, cl, sl, mod, wof, woy,
        g_post_mix[l][None, :], g_pre_ffn[l][None, :], g_post_ffn[l][None, :],
        w_gate[l].astype(BF16), w_up[l].astype(BF16), w_down[l].astype(BF16), tm=512)
```

```python
import functools
import math

import jax
import jax.numpy as jnp
import numpy as np
from jax import lax
from jax.experimental import pallas as pl
from jax.experimental.pallas import tpu as pltpu

F32 = jnp.float32
BF16 = jnp.bfloat16

D_MODEL = 1024
GRID_W = 64
F_WIDTH = 256
N_FGROUPS = 4
FGROUP_DIM = F_WIDTH // N_FGROUPS
SSD_WIDTH = 768
HEAD_DIM = 64
SSD_HEADS = 12
SSD_GROUPS = 4
HEADS_PER_GROUP = 3
D_STATE = 128
BC_WIDTH = SSD_GROUPS * D_STATE
D_CONV = 7
HALF_CONV = D_CONV // 2
CHUNK = 128
XBC_OFF = F_WIDTH + SSD_WIDTH
CONV_DIM = SSD_WIDTH + 2 * BC_WIDTH
N_MOD = 6
EPS = 1e-6
LOG2E = math.log2(math.e)

GROUP_PAD = 256
SSD_PAD = SSD_GROUPS * GROUP_PAD
XBC_PAD = SSD_PAD + 2 * BC_WIDTH
W_COLS = F_WIDTH + SSD_PAD + XBC_PAD
CTX_COL0 = F_WIDTH + SSD_PAD
DT_ROWS = 32
DT_DIR_ROWS = 16
MOD_ROWS = 16
MOD_CTX_ROW = 8

SUBLANE = 8
LANE = 128
VMEM_LIMIT = 56 * 1024 * 1024


def _dot(a, b):
    return jnp.dot(a, b, preferred_element_type=F32)


def _dot_mixed(a_f32, b_bf16):
    return lax.dot_general(a_f32, b_bf16, (((1,), (0,)), ((), ())), preferred_element_type=F32)


def _silu(v):
    return v * jax.nn.sigmoid(v)


def _rms(v, g, n=None):
    n = v.shape[-1] if n is None else n
    ms = jnp.sum(v * v, axis=-1, keepdims=True) * (1.0 / n)
    return v * lax.rsqrt(ms + EPS) * g


def _ada_kernel(c_ref, w_ref, b_ref, o_ref):
    s = _silu(c_ref[...]).astype(BF16)
    o_ref[...] = _dot(s, w_ref[...].astype(BF16)) + b_ref[...]


def _ada_call(cc, w_ada, b_ada, layer):
    n = w_ada.shape[2]
    tn = 768
    return pl.pallas_call(
        _ada_kernel,
        grid=(n // tn,),
        in_specs=[
            pl.BlockSpec((MOD_ROWS, D_MODEL), lambda j: (0, 0)),
            pl.BlockSpec((None, D_MODEL, tn), lambda j: (layer, 0, j)),
            pl.BlockSpec((1, tn), lambda j: (0, j)),
        ],
        out_specs=pl.BlockSpec((MOD_ROWS, tn), lambda j: (0, j)),
        out_shape=jax.ShapeDtypeStruct((MOD_ROWS, n), F32),
        compiler_params=pltpu.CompilerParams(dimension_semantics=("arbitrary",)),
        name="ada_mod",
    )(cc, w_ada, b_ada)


CONV_STRIDE = SUBLANE + 1
CONV_PIECE = SUBLANE * CONV_STRIDE


def _conv_pieces(tm, row_len):
    padded = (tm // row_len) * (row_len + SUBLANE)
    return -(-padded // CONV_PIECE)


def _inproj_kernel(*refs, ctx, tm, row_len):
    if ctx:
        (x_ref, mod_ref, g_ref, w_ref, wdt_ref, cw_ref, cb_ref,
         xs_ref, bt_ref, cm_ref, dt_ref, pad_ref, cv_ref) = refs
    else:
        (x_ref, mod_ref, g_ref, w_ref, wdt_ref, cw_ref, cb_ref, wc_ref,
         uu_ref, z_ref, xs_ref, bt_ref, cm_ref, dt_ref, pad_ref, cv_ref) = refs

    xt = x_ref[0]
    if ctx:
        shift = mod_ref[MOD_CTX_ROW:MOD_CTX_ROW + 1, 0:D_MODEL]
        scale = mod_ref[MOD_CTX_ROW:MOD_CTX_ROW + 1, D_MODEL:2 * D_MODEL]
    else:
        b = pl.program_id(0)
        shift = mod_ref[pl.ds(b, 1), 0:D_MODEL]
        scale = mod_ref[pl.ds(b, 1), D_MODEL:2 * D_MODEL]
    h = _rms(xt, g_ref[...]) * (1.0 + scale) + shift
    hb = h.astype(BF16)

    if not ctx:
        uf = _dot(hb, w_ref[:, 0:F_WIDTH])
        uu_ref[0] = _dot(uf.astype(BF16), wc_ref[...].astype(BF16)).astype(BF16)
        z_ref[0] = _dot(hb, w_ref[:, F_WIDTH:CTX_COL0]).astype(BF16)
        xbc = _dot(hb, w_ref[:, CTX_COL0:W_COLS])
    else:
        xbc = _dot(hb, w_ref[:, CTX_COL0:W_COLS])
    dt_ref[0] = lax.dot_general(wdt_ref[...], hb, (((1,), (1,)), ((), ())),
                                preferred_element_type=F32)

    pitch = row_len + SUBLANE
    nrow = tm // row_len
    npiece = _conv_pieces(tm, row_len)
    data_end = SUBLANE + nrow * pitch
    alloc_end = pad_ref.shape[1]
    zrow = jnp.zeros((SUBLANE, LANE), F32)
    for j in range(XBC_PAD // LANE):
        pad_ref[j, 0:SUBLANE, :] = zrow
        for r in range(nrow):
            base = SUBLANE + r * pitch
            pad_ref[j, base:base + row_len, :] = xbc[r * row_len:(r + 1) * row_len,
                                                     j * LANE:(j + 1) * LANE]
            pad_ref[j, base + row_len:base + pitch, :] = zrow
        for z0 in range(data_end, alloc_end, SUBLANE):
            pad_ref[j, z0:z0 + SUBLANE, :] = zrow

    for j in range(XBC_PAD // LANE):
        wk = [jnp.broadcast_to(cw_ref[k:k + 1, j * LANE:(j + 1) * LANE], (SUBLANE, LANE))
              for k in range(D_CONV)]
        bias = jnp.broadcast_to(cb_ref[:, j * LANE:(j + 1) * LANE], (SUBLANE, LANE))
        for p in range(npiece):
            base = SUBLANE + p * CONV_PIECE
            wins = {v: pad_ref[j, pl.ds(base + v, SUBLANE, stride=CONV_STRIDE), :]
                    for v in range(-HALF_CONV, CONV_STRIDE + HALF_CONV)}
            for a in range(CONV_STRIDE):
                acc = bias
                for k in range(D_CONV):
                    acc = acc + wins[a + k - HALF_CONV] * wk[k]
                cv_ref[j, pl.ds(base + a, SUBLANE, stride=CONV_STRIDE), :] = acc

    def conv_rows(j, t0, n):
        parts = []
        for t in range(t0, t0 + n, min(n, row_len)):
            src = SUBLANE + (t // row_len) * pitch + t % row_len
            parts.append(cv_ref[j, src:src + min(n, row_len), :])
        return _silu(parts[0] if len(parts) == 1 else jnp.concatenate(parts, axis=0))

    for j in range(XBC_PAD // LANE):
        c0 = j * LANE
        for ci in range(tm // CHUNK):
            val = conv_rows(j, ci * CHUNK, CHUNK)
            rows = slice(ci * CHUNK, (ci + 1) * CHUNK)
            if c0 < SSD_PAD:
                xs_ref[0, rows, c0:c0 + LANE] = val.astype(BF16)
            elif c0 < SSD_PAD + BC_WIDTH:
                cc = c0 - SSD_PAD
                bt_ref[0, ci, cc:cc + LANE, :] = val.T.astype(BF16)
            else:
                cc = c0 - SSD_PAD - BC_WIDTH
                cm_ref[0, rows, cc:cc + LANE] = val.astype(BF16)


def _inproj_call(xin, mod, g, w, wdt, cw, cb, wc, *, ctx, tm, row_len):
    bsz, seq, _ = xin.shape
    nt = seq // tm
    const2 = lambda b, j: (0, 0)
    tile3 = lambda b, j: (b, j, 0)
    in_specs = [
        pl.BlockSpec((1, tm, D_MODEL), tile3),
        pl.BlockSpec((MOD_ROWS, N_MOD * D_MODEL), const2),
        pl.BlockSpec((1, D_MODEL), const2),
        pl.BlockSpec((D_MODEL, W_COLS), const2),
        pl.BlockSpec((DT_ROWS, D_MODEL), const2),
        pl.BlockSpec((D_CONV, XBC_PAD), const2),
        pl.BlockSpec((1, XBC_PAD), const2),
    ]
    args = [xin, mod, g, w, wdt, cw, cb]
    out_specs = []
    out_shape = []
    if not ctx:
        in_specs.append(pl.BlockSpec((F_WIDTH, 2 * F_WIDTH), const2))
        args.append(wc)
        out_specs += [pl.BlockSpec((1, tm, 2 * F_WIDTH), tile3),
                      pl.BlockSpec((1, tm, SSD_PAD), tile3)]
        out_shape += [jax.ShapeDtypeStruct((bsz, seq, 2 * F_WIDTH), BF16),
                      jax.ShapeDtypeStruct((bsz, seq, SSD_PAD), BF16)]
    out_specs += [pl.BlockSpec((1, tm, SSD_PAD), tile3),
                  pl.BlockSpec((1, tm // CHUNK, BC_WIDTH, CHUNK), lambda b, j: (b, j, 0, 0)),
                  pl.BlockSpec((1, tm, BC_WIDTH), tile3),
                  pl.BlockSpec((1, DT_ROWS, tm), lambda b, j: (b, 0, j))]
    out_shape += [jax.ShapeDtypeStruct((bsz, seq, SSD_PAD), BF16),
                  jax.ShapeDtypeStruct((bsz, seq // CHUNK, BC_WIDTH, CHUNK), BF16),
                  jax.ShapeDtypeStruct((bsz, seq, BC_WIDTH), BF16),
                  jax.ShapeDtypeStruct((bsz, DT_ROWS, seq), F32)]
    pad_rows = 2 * SUBLANE + _conv_pieces(tm, row_len) * CONV_PIECE
    conv_scratch = pltpu.VMEM((XBC_PAD // LANE, pad_rows, LANE), F32)
    return pl.pallas_call(
        functools.partial(_inproj_kernel, ctx=ctx, tm=tm, row_len=row_len),
        grid=(bsz, nt),
        in_specs=in_specs,
        out_specs=out_specs,
        out_shape=out_shape,
        scratch_shapes=[conv_scratch, conv_scratch],
        compiler_params=pltpu.CompilerParams(
            dimension_semantics=("arbitrary", "arbitrary"),
            vmem_limit_bytes=VMEM_LIMIT),
        name="inproj_ctx" if ctx else "inproj_lat",
    )(*args)


def _ssd_kernel(xs_ref, bt_ref, cm_ref, z_ref, dt_ref,
                xsc_ref, btc_ref, cmc_ref, dtc_ref,
                dtb_ref, alog_ref, dskip_ref, gssd_ref,
                out_ref, h_ref, y_ref, r2_ref, w2_ref, ee_ref, qt_ref, *, seq, ctx_len):
    nchunk = seq // CHUNK
    nchunk_ctx = ctx_len // CHUNK

    h_ref[...] = jnp.zeros(h_ref.shape, F32)

    bias = dtb_ref[...]
    nega = -jnp.exp(alog_ref[...])
    sub_i = lax.broadcasted_iota(jnp.int32, (CHUNK, CHUNK), 0)
    lane_i = lax.broadcasted_iota(jnp.int32, (CHUNK, CHUNK), 1)
    tri = (lane_i <= sub_i, lane_i >= sub_i)
    lane_lo = lane_i < HEAD_DIM
    lane_row = lax.broadcasted_iota(jnp.int32, (1, CHUNK), 1)

    def colb(mat_t, idx):
        return jnp.broadcast_to(mat_t[:, idx:idx + 1], (CHUNK, CHUNK))

    def prepare(d, raws):
        r0 = d * DT_DIR_ROWS
        n = len(raws)
        v = jnp.concatenate(raws, axis=0) + jnp.concatenate([bias[r0:r0 + DT_DIR_ROWS]] * n, axis=0)
        dt = jnp.maximum(v, 0.0) + jnp.log1p(jnp.exp(-jnp.abs(v)))
        da = dt * jnp.concatenate([nega[r0:r0 + DT_DIR_ROWS]] * n, axis=0)
        lane_n = lax.broadcasted_iota(jnp.int32, da.shape, 1)
        cs = da
        sh = 1
        while sh < CHUNK:
            if d == 0:
                cs = cs + jnp.where(lane_n >= sh, pltpu.roll(cs, sh, axis=1), 0.0)
            else:
                cs = cs + jnp.where(lane_n < CHUNK - sh, pltpu.roll(cs, CHUNK - sh, axis=1), 0.0)
            sh *= 2
        a_end = jnp.sum(da, axis=1, keepdims=True)
        cs2 = cs * LOG2E
        r2 = cs2 - jnp.log2(dt)
        w2 = jnp.exp2(a_end * LOG2E - r2)
        ee = jnp.broadcast_to(jnp.exp(a_end), da.shape)
        return cs2, r2, w2, ee

    def store_prep(d, k0, raws):
        cs2, r2, w2, ee = prepare(d, raws)
        for i in range(len(raws)):
            rows = slice(i * DT_DIR_ROWS, (i + 1) * DT_DIR_ROWS)
            r2_ref[d, k0 + i] = r2[rows]
            w2_ref[d, k0 + i] = w2[rows]
            ee_ref[d, k0 + i] = ee[rows]
            qt_ref[d, k0 + i] = jnp.concatenate(
                [cs2[rows], jnp.zeros((CHUNK - DT_DIR_ROWS, CHUNK), F32)], axis=0).T

    def direction(d, x, btc, cc, k, row0, mode):
        with_output = mode is not None
        r2 = r2_ref[d, k]
        w2 = w2_ref[d, k]
        ee = ee_ref[d, k]
        q_t = qt_ref[d, k]
        if with_output:
            scores_all = [_dot(cc[:, g * D_STATE:(g + 1) * D_STATE],
                               btc[g * D_STATE:(g + 1) * D_STATE, :]) for g in range(SSD_GROUPS)]

        s_all = []
        for g in range(SSD_GROUPS):
            bt = btc[g * D_STATE:(g + 1) * D_STATE, :]
            xg = x[:, g * GROUP_PAD:(g + 1) * GROUP_PAD]
            heads = [g * HEADS_PER_GROUP + r for r in range(HEADS_PER_GROUP)]
            btf = bt.astype(F32)
            btw = [(btf * w2[hd:hd + 1, :]).astype(BF16) for hd in heads]
            s01 = _dot(jnp.concatenate([btw[0], btw[1]], axis=0), xg[:, 0:CHUNK])
            s0 = jnp.where(lane_lo, s01[0:D_STATE], s01[D_STATE:2 * D_STATE])
            s1 = _dot(btw[2], xg[:, CHUNK:2 * CHUNK])
            s_all.append(jnp.concatenate([s0, s1], axis=1))

        for g in range(SSD_GROUPS):
            cg = cc[:, g * D_STATE:(g + 1) * D_STATE]
            hg = h_ref[d, g]
            xg = x[:, g * GROUP_PAD:(g + 1) * GROUP_PAD]
            t0 = xg[:, 0:CHUNK]
            t1 = xg[:, CHUNK:2 * CHUNK]
            heads = [g * HEADS_PER_GROUP + r for r in range(HEADS_PER_GROUP)]
            if with_output:
                scores = scores_all[g]
                hb = hg.astype(BF16)
                lhs = []
                cgf = cg.astype(F32)
                for hd in heads:
                    col = colb(q_t, hd)
                    seg = col - r2[hd:hd + 1, :]
                    m = scores * jnp.exp2(jnp.where(tri[d], seg, -jnp.inf))
                    ce = cgf * jnp.exp2(col)
                    lhs.append(jnp.concatenate([m, ce], axis=1))
                rhs0 = jnp.concatenate([t0, hb[:, 0:CHUNK]], axis=0)
                rhs1 = jnp.concatenate([t1, hb[:, CHUNK:2 * CHUNK]], axis=0)
                y01 = _dot_mixed(jnp.concatenate([lhs[0], lhs[1]], axis=0), rhs0)
                y0 = jnp.where(lane_lo, y01[0:CHUNK], y01[CHUNK:2 * CHUNK])
                y1 = _dot_mixed(lhs[2], rhs1)
                y_g = jnp.concatenate([y0, y1], axis=1)
                if mode:
                    y_g = y_g + y_ref[pl.ds(row0, CHUNK), g * GROUP_PAD:(g + 1) * GROUP_PAD]
                y_ref[pl.ds(row0, CHUNK), g * GROUP_PAD:(g + 1) * GROUP_PAD] = y_g
            er = [ee[hd:hd + 1, :] for hd in heads]
            e3 = jnp.concatenate([jnp.where(lane_row < HEAD_DIM, er[0], er[1]), er[2]], axis=1)
            h_ref[d, g] = e3 * hg + s_all[g]

    for d in range(2):
        rows_d = slice(d * DT_DIR_ROWS, (d + 1) * DT_DIR_ROWS)
        store_prep(d, 0, [dtc_ref[0, rows_d, c * CHUNK:(c + 1) * CHUNK] for c in range(nchunk_ctx)])
        store_prep(d, nchunk_ctx, [dt_ref[0, rows_d, c * CHUNK:(c + 1) * CHUNK] for c in range(nchunk)])

    for i in range(nchunk_ctx):
        for d in range(2):
            ci = i if d == 0 else nchunk_ctx - 1 - i
            rows = slice(ci * CHUNK, (ci + 1) * CHUNK)
            direction(d, xsc_ref[0, rows, :], btc_ref[0, ci], cmc_ref[0, rows, :], ci, 0, None)

    def finish(row0):
        xr = xs_ref[0, pl.ds(row0, CHUNK), :].astype(F32)
        zr = z_ref[0, pl.ds(row0, CHUNK), :].astype(F32)
        y = y_ref[pl.ds(row0, CHUNK), :] + xr * dskip_ref[...]
        y = y * _silu(zr)
        out_ref[0, pl.ds(row0, CHUNK), :] = _rms(y, gssd_ref[...], n=SSD_WIDTH).astype(BF16)

    def step(i, carry, *, accumulate):
        row0s = []
        for d in range(2):
            ci = i if d == 0 else nchunk - 1 - i
            row0 = pl.multiple_of(ci * CHUNK, CHUNK)
            row0s.append(row0)
            direction(d, xs_ref[0, pl.ds(row0, CHUNK), :], bt_ref[0, ci],
                      cm_ref[0, pl.ds(row0, CHUNK), :], nchunk_ctx + ci, row0, accumulate)
        if accumulate:
            for row0 in row0s:
                finish(row0)
        return carry

    half = nchunk // 2
    lax.fori_loop(0, half, functools.partial(step, accumulate=False), 0, unroll=2)
    lax.fori_loop(half, nchunk, functools.partial(step, accumulate=True), 0, unroll=2)


def _ssd_call(xs, bm, cm, z, dt, xsc, bmc, cmc, dtc, dtb, alog, dskip, gssd):
    bsz, seq, _ = xs.shape
    ctx_len = xsc.shape[1]
    nck = (seq + ctx_len) // CHUNK
    per_b = lambda b: (b, 0, 0)
    const2 = lambda b: (0, 0)
    return pl.pallas_call(
        functools.partial(_ssd_kernel, seq=seq, ctx_len=ctx_len),
        grid=(bsz,),
        in_specs=[
            pl.BlockSpec((1, seq, SSD_PAD), per_b),
            pl.BlockSpec((1, seq // CHUNK, BC_WIDTH, CHUNK), lambda b: (b, 0, 0, 0)),
            pl.BlockSpec((1, seq, BC_WIDTH), per_b),
            pl.BlockSpec((1, seq, SSD_PAD), per_b),
            pl.BlockSpec((1, DT_ROWS, seq), per_b),
            pl.BlockSpec((1, ctx_len, SSD_PAD), per_b),
            pl.BlockSpec((1, ctx_len // CHUNK, BC_WIDTH, CHUNK), lambda b: (b, 0, 0, 0)),
            pl.BlockSpec((1, ctx_len, BC_WIDTH), per_b),
            pl.BlockSpec((1, DT_ROWS, ctx_len), per_b),
            pl.BlockSpec((DT_ROWS, 1), const2),
            pl.BlockSpec((DT_ROWS, 1), const2),
            pl.BlockSpec((1, SSD_PAD), const2),
            pl.BlockSpec((1, SSD_PAD), const2),
        ],
        out_specs=pl.BlockSpec((1, seq, SSD_PAD), per_b),
        out_shape=jax.ShapeDtypeStruct((bsz, seq, SSD_PAD), BF16),
        scratch_shapes=[
            pltpu.VMEM((2, SSD_GROUPS, D_STATE, GROUP_PAD), F32),
            pltpu.VMEM((seq, SSD_PAD), F32),
            pltpu.VMEM((2, nck, DT_DIR_ROWS, CHUNK), F32),
            pltpu.VMEM((2, nck, DT_DIR_ROWS, CHUNK), F32),
            pltpu.VMEM((2, nck, DT_DIR_ROWS, CHUNK), F32),
            pltpu.VMEM((2, nck, CHUNK, CHUNK), F32),
        ],
        compiler_params=pltpu.CompilerParams(
            dimension_semantics=("arbitrary",), vmem_limit_bytes=VMEM_LIMIT),
        name="ssd_scan",
    )(xs, bm, cm, z, dt, xsc, bmc, cmc, dtc, dtb, alog, dskip, gssd)


FFN_CHUNK = 256
MIX_SPLIT = 2


def _out_ffn_kernel(x_ref, yn_ref, uu_ref, cl_ref, sl_ref, mod_ref,
                    wof_ref, woy_ref, gpm_ref, gpf_ref, gpo_ref,
                    wg_ref, wu_ref, wd_ref, out_ref):
    b = pl.program_id(1)

    def mod(k):
        return mod_ref[pl.ds(b, 1), k * D_MODEL:(k + 1) * D_MODEL]

    half = x_ref.shape[1] // MIX_SPLIT
    x1_parts, h2_parts = [], []
    for r0 in range(0, x_ref.shape[1], half):
        rows = slice(r0, r0 + half)
        yf = (_dot(cl_ref[rows, :], uu_ref[0, :, 0:F_WIDTH])
              + _dot(sl_ref[rows, :], uu_ref[0, :, F_WIDTH:2 * F_WIDTH]))
        mix = _dot(yf.astype(BF16), wof_ref[...]) + _dot(yn_ref[0, rows, :], woy_ref[...])
        x1_h = x_ref[0, rows, :] + mod(2) * _rms(mix, gpm_ref[...])
        x1_parts.append(x1_h)
        h2_parts.append((_rms(x1_h, gpf_ref[...]) * (1.0 + mod(4)) + mod(3)).astype(BF16))
    x1 = jnp.concatenate(x1_parts, axis=0)
    h2 = jnp.concatenate(h2_parts, axis=0)
    ffn = None
    for c0 in range(0, wg_ref.shape[1], FFN_CHUNK):
        gate = _dot(h2, wg_ref[:, c0:c0 + FFN_CHUNK])
        up = _dot(h2, wu_ref[:, c0:c0 + FFN_CHUNK])
        act = (_silu(gate) * up).astype(BF16)
        part = _dot(act, wd_ref[c0:c0 + FFN_CHUNK, :])
        ffn = part if ffn is None else ffn + part
    out_ref[0] = x1 + mod(5) * _rms(ffn, gpo_ref[...])


def _out_ffn_call(x, yn, uu, cl, sl, mod, wof, woy, gpm, gpf, gpo, wg, wu, wd, *, tm):
    bsz, seq, _ = x.shape
    d_ff = wg.shape[1]
    nt = seq // tm
    const2 = lambda j, b: (0, 0)
    tile3 = lambda j, b: (b, j, 0)
    single = dict(pipeline_mode=pl.Buffered(1))
    return pl.pallas_call(
        _out_ffn_kernel,
        grid=(nt, bsz),
        in_specs=[
            pl.BlockSpec((1, tm, D_MODEL), tile3),
            pl.BlockSpec((1, tm, SSD_PAD), tile3),
            pl.BlockSpec((1, seq, 2 * F_WIDTH), lambda j, b: (b, 0, 0)),
            pl.BlockSpec((tm, seq), lambda j, b: (j, 0)),
            pl.BlockSpec((tm, seq), lambda j, b: (j, 0)),
            pl.BlockSpec((MOD_ROWS, N_MOD * D_MODEL), const2),
            pl.BlockSpec((F_WIDTH, D_MODEL), const2, **single),
            pl.BlockSpec((SSD_PAD, D_MODEL), const2, **single),
            pl.BlockSpec((1, D_MODEL), const2),
            pl.BlockSpec((1, D_MODEL), const2),
            pl.BlockSpec((1, D_MODEL), const2),
            pl.BlockSpec((D_MODEL, d_ff), const2, **single),
            pl.BlockSpec((D_MODEL, d_ff), const2, **single),
            pl.BlockSpec((d_ff, D_MODEL), const2, **single),
        ],
        out_specs=pl.BlockSpec((1, tm, D_MODEL), tile3),
        out_shape=jax.ShapeDtypeStruct((bsz, seq, D_MODEL), F32),
        compiler_params=pltpu.CompilerParams(
            dimension_semantics=("arbitrary", "arbitrary"),
            vmem_limit_bytes=VMEM_LIMIT),
        name="out_ffn",
    )(x, yn, uu, cl, sl, mod, wof, woy, gpm, gpf, gpo, wg, wu, wd)


@functools.lru_cache(maxsize=None)
def _dft_tables(seq):
    k = np.arange(seq, dtype=np.int64)
    ang = 2.0 * np.pi * ((k[:, None] * k[None, :]) % seq).astype(np.float64) / seq
    scale = 1.0 / math.sqrt(seq)
    cl = (np.cos(ang) * scale).astype(np.float32)
    sl = (np.sin(ang) * scale).astype(np.float32)
    c = np.arange(FGROUP_DIM, dtype=np.int64)
    angc = 2.0 * np.pi * ((c[:, None] * c[None, :]) % FGROUP_DIM).astype(np.float64) / FGROUP_DIM
    sc = 1.0 / math.sqrt(FGROUP_DIM)
    wc = np.zeros((F_WIDTH, 2 * F_WIDTH), np.float32)
    for g in range(N_FGROUPS):
        s = slice(g * FGROUP_DIM, (g + 1) * FGROUP_DIM)
        wc[s, g * FGROUP_DIM:(g + 1) * FGROUP_DIM] = np.cos(angc) * sc
        wc[s, F_WIDTH + g * FGROUP_DIM:F_WIDTH + (g + 1) * FGROUP_DIM] = -np.sin(angc) * sc
    return cl, sl, wc


def _pad_groups(a, axis):
    a = jnp.moveaxis(a, axis, -1)
    lead = a.shape[:-1]
    a = a.reshape(lead + (SSD_GROUPS, HEADS_PER_GROUP * HEAD_DIM))
    a = jnp.pad(a, [(0, 0)] * len(lead) + [(0, 0), (0, GROUP_PAD - HEADS_PER_GROUP * HEAD_DIM)])
    a = a.reshape(lead + (SSD_PAD,))
    return jnp.moveaxis(a, -1, axis)


def _pad_dirs(a):
    a = jnp.pad(a, ((0, 0), (0, DT_DIR_ROWS - SSD_HEADS)))
    return a.reshape(DT_ROWS, 1)


def kernel(x, c, ctx, c_ctx, w_ada, b_ada, g_pre_mix, g_post_mix, g_pre_ffn, g_post_ffn,
           w_in, conv_w, conv_b, dt_bias, a_log, d_skip, g_ssd, w_out, w_gate, w_up, w_down):
    bsz, seq, _ = x.shape
    ctx_len = ctx.shape[1]
    l = 0
    cl_np, sl_np, wc_np = _dft_tables(seq)
    cl = jnp.asarray(cl_np).astype(BF16)
    sl = jnp.asarray(sl_np).astype(BF16)
    wc = jnp.asarray(wc_np)

    cc = jnp.concatenate(
        [c, c_ctx[None, :], jnp.zeros((MOD_ROWS - bsz - 1, D_MODEL), F32)], axis=0)
    mod = _ada_call(cc, w_ada, b_ada[l][None, :], l)

    wi = w_in[l][:, :XBC_OFF + CONV_DIM].astype(BF16)
    zx_end = XBC_OFF + SSD_WIDTH
    zx = wi[:, F_WIDTH:zx_end].reshape(D_MODEL, 2 * SSD_GROUPS, HEADS_PER_GROUP * HEAD_DIM)
    zx = jnp.pad(zx, ((0, 0), (0, 0), (0, GROUP_PAD - HEADS_PER_GROUP * HEAD_DIM)))
    w_cat = jnp.concatenate(
        [wi[:, :F_WIDTH], zx.reshape(D_MODEL, 2 * SSD_PAD), wi[:, zx_end:]], axis=1)
    w_dt = w_in[l][:, XBC_OFF + CONV_DIM:].T.reshape(2, SSD_HEADS, D_MODEL)
    w_dt = jnp.pad(w_dt, ((0, 0), (0, DT_DIR_ROWS - SSD_HEADS), (0, 0))).reshape(DT_ROWS, D_MODEL)
    w_dt = w_dt.astype(BF16)
    cw = conv_w[l].T
    cw = jnp.concatenate([_pad_groups(cw[:, :SSD_WIDTH], 1), cw[:, SSD_WIDTH:]], axis=1)
    cb = conv_b[l][None, :]
    cb = jnp.concatenate([_pad_groups(cb[:, :SSD_WIDTH], 1), cb[:, SSD_WIDTH:]], axis=1)
    g_pre = g_pre_mix[l][None, :]

    xs_c, bm_c, cm_c, dt_c = _inproj_call(
        ctx, mod, g_pre, w_cat, w_dt, cw, cb, None, ctx=True, tm=ctx_len, row_len=ctx_len)
    uu, z, xs, bm, cm, dt = _inproj_call(
        x, mod, g_pre, w_cat, w_dt, cw, cb, wc, ctx=False, tm=512, row_len=GRID_W)

    dskip = _pad_groups(jnp.repeat(d_skip[l], HEAD_DIM)[None, :], 1)
    gssd = _pad_groups(g_ssd[l][None, :], 1)
    yn = _ssd_call(xs, bm, cm, z, dt, xs_c, bm_c, cm_c, dt_c,
                   _pad_dirs(dt_bias[l]), _pad_dirs(a_log[l]), dskip, gssd)

    wo = w_out[l]
    wof = wo[:F_WIDTH].astype(BF16)
    woy = _pad_groups(wo[F_WIDTH:].astype(BF16), 0)
    return _out_ffn_call(
        x, yn, uu, cl, sl, mod, wof, woy,
        g_post_mix[l][None, :], g_pre_ffn[l][None, :], g_post_ffn[l][None, :],
        w_gate[l].astype(BF16), w_up[l].astype(BF16), w_down[l].astype(BF16), tm=512)
```

```python
import functools
import math

import jax
import jax.numpy as jnp
import numpy as np
from jax import lax
from jax.experimental import pallas as pl
from jax.experimental.pallas import tpu as pltpu

F32 = jnp.float32
BF16 = jnp.bfloat16

D_MODEL = 1024
GRID_W = 64
F_WIDTH = 256
N_FGROUPS = 4
FGROUP_DIM = F_WIDTH // N_FGROUPS
SSD_WIDTH = 768
HEAD_DIM = 64
SSD_HEADS = 12
SSD_GROUPS = 4
HEADS_PER_GROUP = 3
D_STATE = 128
BC_WIDTH = SSD_GROUPS * D_STATE
D_CONV = 7
HALF_CONV = D_CONV // 2
CHUNK = 128
XBC_OFF = F_WIDTH + SSD_WIDTH
CONV_DIM = SSD_WIDTH + 2 * BC_WIDTH
N_MOD = 6
EPS = 1e-6
LOG2E = math.log2(math.e)

GROUP_PAD = 256
SSD_PAD = SSD_GROUPS * GROUP_PAD
XBC_PAD = SSD_PAD + 2 * BC_WIDTH
W_COLS = F_WIDTH + SSD_PAD + XBC_PAD
CTX_COL0 = F_WIDTH + SSD_PAD
DT_ROWS = 32
DT_DIR_ROWS = 16
MOD_ROWS = 16
MOD_CTX_ROW = 8

SUBLANE = 8
LANE = 128
VMEM_LIMIT = 56 * 1024 * 1024


def _dot(a, b):
    return jnp.dot(a, b, preferred_element_type=F32)


def _dot_mixed(a_f32, b_bf16):
    return lax.dot_general(a_f32, b_bf16, (((1,), (0,)), ((), ())), preferred_element_type=F32)


def _silu(v):
    return v * jax.nn.sigmoid(v)


def _rms(v, g, n=None):
    n = v.shape[-1] if n is None else n
    ms = jnp.sum(v * v, axis=-1, keepdims=True) * (1.0 / n)
    return v * lax.rsqrt(ms + EPS) * g


def _ada_kernel(c_ref, w_ref, b_ref, o_ref):
    s = _silu(c_ref[...]).astype(BF16)
    o_ref[...] = _dot(s, w_ref[...].astype(BF16)) + b_ref[...]


def _ada_call(cc, w_ada, b_ada, layer):
    n = w_ada.shape[2]
    tn = 768
    return pl.pallas_call(
        _ada_kernel,
        grid=(n // tn,),
        in_specs=[
            pl.BlockSpec((MOD_ROWS, D_MODEL), lambda j: (0, 0)),
            pl.BlockSpec((None, D_MODEL, tn), lambda j: (layer, 0, j)),
            pl.BlockSpec((1, tn), lambda j: (0, j)),
        ],
        out_specs=pl.BlockSpec((MOD_ROWS, tn), lambda j: (0, j)),
        out_shape=jax.ShapeDtypeStruct((MOD_ROWS, n), F32),
        compiler_params=pltpu.CompilerParams(dimension_semantics=("arbitrary",)),
        name="ada_mod",
    )(cc, w_ada, b_ada)


CONV_STRIDE = SUBLANE + 1
CONV_PIECE = SUBLANE * CONV_STRIDE


def _conv_pieces(tm, row_len):
    padded = (tm // row_len) * (row_len + SUBLANE)
    return -(-padded // CONV_PIECE)


def _inproj_kernel(*refs, ctx, tm, row_len):
    if ctx:
        (x_ref, mod_ref, g_ref, w_ref, wdt_ref, cw_ref, cb_ref,
         xs_ref, bt_ref, cm_ref, dt_ref, pad_ref, cv_ref) = refs
    else:
        (x_ref, mod_ref, g_ref, w_ref, wdt_ref, cw_ref, cb_ref, wc_ref,
         uu_ref, z_ref, xs_ref, bt_ref, cm_ref, dt_ref, pad_ref, cv_ref) = refs

    xt = x_ref[0]
    if ctx:
        shift = mod_ref[MOD_CTX_ROW:MOD_CTX_ROW + 1, 0:D_MODEL]
        scale = mod_ref[MOD_CTX_ROW:MOD_CTX_ROW + 1, D_MODEL:2 * D_MODEL]
    else:
        b = pl.program_id(0)
        shift = mod_ref[pl.ds(b, 1), 0:D_MODEL]
        scale = mod_ref[pl.ds(b, 1), D_MODEL:2 * D_MODEL]
    h = _rms(xt, g_ref[...]) * (1.0 + scale) + shift
    hb = h.astype(BF16)

    if not ctx:
        uf = _dot(hb, w_ref[:, 0:F_WIDTH])
        uu_ref[0] = _dot(uf.astype(BF16), wc_ref[...].astype(BF16)).astype(BF16)
        z_ref[0] = _silu(_dot(hb, w_ref[:, F_WIDTH:CTX_COL0])).astype(BF16)
        xbc = _dot(hb, w_ref[:, CTX_COL0:W_COLS])
    else:
        xbc = _dot(hb, w_ref[:, CTX_COL0:W_COLS])
    dt_ref[0] = lax.dot_general(wdt_ref[...], hb, (((1,), (1,)), ((), ())),
                                preferred_element_type=F32)

    pitch = row_len + SUBLANE
    nrow = tm // row_len
    npiece = _conv_pieces(tm, row_len)
    data_end = SUBLANE + nrow * pitch
    alloc_end = pad_ref.shape[1]
    zrow = jnp.zeros((SUBLANE, LANE), F32)
    for j in range(XBC_PAD // LANE):
        pad_ref[j, 0:SUBLANE, :] = zrow
        for r in range(nrow):
            base = SUBLANE + r * pitch
            pad_ref[j, base:base + row_len, :] = xbc[r * row_len:(r + 1) * row_len,
                                                     j * LANE:(j + 1) * LANE]
            pad_ref[j, base + row_len:base + pitch, :] = zrow
        for z0 in range(data_end, alloc_end, SUBLANE):
            pad_ref[j, z0:z0 + SUBLANE, :] = zrow

    for j in range(XBC_PAD // LANE):
        wk = [jnp.broadcast_to(cw_ref[k:k + 1, j * LANE:(j + 1) * LANE], (SUBLANE, LANE))
              for k in range(D_CONV)]
        bias = jnp.broadcast_to(cb_ref[:, j * LANE:(j + 1) * LANE], (SUBLANE, LANE))
        for p in range(npiece):
            base = SUBLANE + p * CONV_PIECE
            wins = {v: pad_ref[j, pl.ds(base + v, SUBLANE, stride=CONV_STRIDE), :]
                    for v in range(-HALF_CONV, CONV_STRIDE + HALF_CONV)}
            for a in range(CONV_STRIDE):
                acc = bias
                for k in range(D_CONV):
                    acc = acc + wins[a + k - HALF_CONV] * wk[k]
                cv_ref[j, pl.ds(base + a, SUBLANE, stride=CONV_STRIDE), :] = acc

    def conv_rows(j, t0, n):
        parts = []
        for t in range(t0, t0 + n, min(n, row_len)):
            src = SUBLANE + (t // row_len) * pitch + t % row_len
            parts.append(cv_ref[j, src:src + min(n, row_len), :])
        return _silu(parts[0] if len(parts) == 1 else jnp.concatenate(parts, axis=0))

    for j in range(XBC_PAD // LANE):
        c0 = j * LANE
        for ci in range(tm // CHUNK):
            val = conv_rows(j, ci * CHUNK, CHUNK)
            rows = slice(ci * CHUNK, (ci + 1) * CHUNK)
            if c0 < SSD_PAD:
                xs_ref[0, rows, c0:c0 + LANE] = val.astype(BF16)
            elif c0 < SSD_PAD + BC_WIDTH:
                cc = c0 - SSD_PAD
                bt_ref[0, ci, cc:cc + LANE, :] = val.T.astype(BF16)
            else:
                cc = c0 - SSD_PAD - BC_WIDTH
                cm_ref[0, rows, cc:cc + LANE] = val.astype(BF16)


def _inproj_call(xin, mod, g, w, wdt, cw, cb, wc, *, ctx, tm, row_len):
    bsz, seq, _ = xin.shape
    nt = seq // tm
    const2 = lambda b, j: (0, 0)
    tile3 = lambda b, j: (b, j, 0)
    in_specs = [
        pl.BlockSpec((1, tm, D_MODEL), tile3),
        pl.BlockSpec((MOD_ROWS, N_MOD * D_MODEL), const2),
        pl.BlockSpec((1, D_MODEL), const2),
        pl.BlockSpec((D_MODEL, W_COLS), const2),
        pl.BlockSpec((DT_ROWS, D_MODEL), const2),
        pl.BlockSpec((D_CONV, XBC_PAD), const2),
        pl.BlockSpec((1, XBC_PAD), const2),
    ]
    args = [xin, mod, g, w, wdt, cw, cb]
    out_specs = []
    out_shape = []
    if not ctx:
        in_specs.append(pl.BlockSpec((F_WIDTH, 2 * F_WIDTH), const2))
        args.append(wc)
        out_specs += [pl.BlockSpec((1, tm, 2 * F_WIDTH), tile3),
                      pl.BlockSpec((1, tm, SSD_PAD), tile3)]
        out_shape += [jax.ShapeDtypeStruct((bsz, seq, 2 * F_WIDTH), BF16),
                      jax.ShapeDtypeStruct((bsz, seq, SSD_PAD), BF16)]
    out_specs += [pl.BlockSpec((1, tm, SSD_PAD), tile3),
                  pl.BlockSpec((1, tm // CHUNK, BC_WIDTH, CHUNK), lambda b, j: (b, j, 0, 0)),
                  pl.BlockSpec((1, tm, BC_WIDTH), tile3),
                  pl.BlockSpec((1, DT_ROWS, tm), lambda b, j: (b, 0, j))]
    out_shape += [jax.ShapeDtypeStruct((bsz, seq, SSD_PAD), BF16),
                  jax.ShapeDtypeStruct((bsz, seq // CHUNK, BC_WIDTH, CHUNK), BF16),
                  jax.ShapeDtypeStruct((bsz, seq, BC_WIDTH), BF16),
                  jax.ShapeDtypeStruct((bsz, DT_ROWS, seq), F32)]
    pad_rows = 2 * SUBLANE + _conv_pieces(tm, row_len) * CONV_PIECE
    conv_scratch = pltpu.VMEM((XBC_PAD // LANE, pad_rows, LANE), F32)
    return pl.pallas_call(
        functools.partial(_inproj_kernel, ctx=ctx, tm=tm, row_len=row_len),
        grid=(bsz, nt),
        in_specs=in_specs,
        out_specs=out_specs,
        out_shape=out_shape,
        scratch_shapes=[conv_scratch, conv_scratch],
        compiler_params=pltpu.CompilerParams(
            dimension_semantics=("arbitrary", "arbitrary"),
            vmem_limit_bytes=VMEM_LIMIT),
        name="inproj_ctx" if ctx else "inproj_lat",
    )(*args)


def _ssd_kernel(xs_ref, bt_ref, cm_ref, z_ref, dt_ref,
                xsc_ref, btc_ref, cmc_ref, dtc_ref,
                dtb_ref, alog_ref, dskip_ref, gssd_ref,
                out_ref, h_ref, y_ref, r2_ref, w2_ref, ee_ref, qt_ref, *, seq, ctx_len):
    nchunk = seq // CHUNK
    nchunk_ctx = ctx_len // CHUNK

    h_ref[...] = jnp.zeros(h_ref.shape, F32)

    bias = dtb_ref[...]
    nega = -jnp.exp(alog_ref[...])
    sub_i = lax.broadcasted_iota(jnp.int32, (CHUNK, CHUNK), 0)
    lane_i = lax.broadcasted_iota(jnp.int32, (CHUNK, CHUNK), 1)
    tri = (lane_i <= sub_i, lane_i >= sub_i)
    lane_lo = lane_i < HEAD_DIM
    lane_row = lax.broadcasted_iota(jnp.int32, (1, CHUNK), 1)

    def colb(mat_t, idx):
        return jnp.broadcast_to(mat_t[:, idx:idx + 1], (CHUNK, CHUNK))

    def prepare(d, raws):
        r0 = d * DT_DIR_ROWS
        n = len(raws)
        v = jnp.concatenate(raws, axis=0) + jnp.concatenate([bias[r0:r0 + DT_DIR_ROWS]] * n, axis=0)
        dt = jnp.maximum(v, 0.0) + jnp.log1p(jnp.exp(-jnp.abs(v)))
        da = dt * jnp.concatenate([nega[r0:r0 + DT_DIR_ROWS]] * n, axis=0)
        lane_n = lax.broadcasted_iota(jnp.int32, da.shape, 1)
        cs = da
        sh = 1
        while sh < CHUNK:
            if d == 0:
                cs = cs + jnp.where(lane_n >= sh, pltpu.roll(cs, sh, axis=1), 0.0)
            else:
                cs = cs + jnp.where(lane_n < CHUNK - sh, pltpu.roll(cs, CHUNK - sh, axis=1), 0.0)
            sh *= 2
        a_end = jnp.sum(da, axis=1, keepdims=True)
        cs2 = cs * LOG2E
        r2 = cs2 - jnp.log2(dt)
        w2 = jnp.exp2(a_end * LOG2E - r2)
        ee = jnp.broadcast_to(jnp.exp(a_end), da.shape)
        return cs2, r2, w2, ee

    def store_prep(d, k0, raws):
        cs2, r2, w2, ee = prepare(d, raws)
        for i in range(len(raws)):
            rows = slice(i * DT_DIR_ROWS, (i + 1) * DT_DIR_ROWS)
            r2_ref[d, k0 + i] = r2[rows]
            w2_ref[d, k0 + i] = w2[rows]
            ee_ref[d, k0 + i] = ee[rows]
            qt_ref[d, k0 + i] = jnp.concatenate(
                [cs2[rows], jnp.zeros((CHUNK - DT_DIR_ROWS, CHUNK), F32)], axis=0).T

    def direction(d, x, btc, cc, k, row0, mode):
        with_output = mode is not None
        r2 = r2_ref[d, k]
        w2 = w2_ref[d, k]
        ee = ee_ref[d, k]
        q_t = qt_ref[d, k]
        if with_output:
            scores_all = [_dot(cc[:, g * D_STATE:(g + 1) * D_STATE],
                               btc[g * D_STATE:(g + 1) * D_STATE, :]) for g in range(SSD_GROUPS)]

        s_all = []
        for g in range(SSD_GROUPS):
            bt = btc[g * D_STATE:(g + 1) * D_STATE, :]
            xg = x[:, g * GROUP_PAD:(g + 1) * GROUP_PAD]
            heads = [g * HEADS_PER_GROUP + r for r in range(HEADS_PER_GROUP)]
            btf = bt.astype(F32)
            btw = [(btf * w2[hd:hd + 1, :]).astype(BF16) for hd in heads]
            s01 = _dot(jnp.concatenate([btw[0], btw[1]], axis=0), xg[:, 0:CHUNK])
            s0 = jnp.where(lane_lo, s01[0:D_STATE], s01[D_STATE:2 * D_STATE])
            s1 = _dot(btw[2], xg[:, CHUNK:2 * CHUNK])
            s_all.append(jnp.concatenate([s0, s1], axis=1))

        for g in range(SSD_GROUPS):
            cg = cc[:, g * D_STATE:(g + 1) * D_STATE]
            hg = h_ref[d, g]
            xg = x[:, g * GROUP_PAD:(g + 1) * GROUP_PAD]
            t0 = xg[:, 0:CHUNK]
            t1 = xg[:, CHUNK:2 * CHUNK]
            heads = [g * HEADS_PER_GROUP + r for r in range(HEADS_PER_GROUP)]
            if with_output:
                scores = scores_all[g]
                hb = hg.astype(BF16)
                lhs = []
                cgf = cg.astype(F32)
                for hd in heads:
                    col = colb(q_t, hd)
                    seg = col - r2[hd:hd + 1, :]
                    m = scores * jnp.exp2(jnp.where(tri[d], seg, -jnp.inf))
                    ce = cgf * jnp.exp2(col)
                    lhs.append(jnp.concatenate([m, ce], axis=1))
                rhs0 = jnp.concatenate([t0, hb[:, 0:CHUNK]], axis=0)
                rhs1 = jnp.concatenate([t1, hb[:, CHUNK:2 * CHUNK]], axis=0)
                y01 = _dot_mixed(jnp.concatenate([lhs[0], lhs[1]], axis=0), rhs0)
                y0 = jnp.where(lane_lo, y01[0:CHUNK], y01[CHUNK:2 * CHUNK])
                y1 = _dot_mixed(lhs[2], rhs1)
                y_g = jnp.concatenate([y0, y1], axis=1)
                if mode:
                    y_g = y_g + y_ref[pl.ds(row0, CHUNK), g * GROUP_PAD:(g + 1) * GROUP_PAD]
                y_ref[pl.ds(row0, CHUNK), g * GROUP_PAD:(g + 1) * GROUP_PAD] = y_g
            er = [ee[hd:hd + 1, :] for hd in heads]
            e3 = jnp.concatenate([jnp.where(lane_row < HEAD_DIM, er[0], er[1]), er[2]], axis=1)
            h_ref[d, g] = e3 * hg + s_all[g]

    for d in range(2):
        rows_d = slice(d * DT_DIR_ROWS, (d + 1) * DT_DIR_ROWS)
        store_prep(d, 0, [dtc_ref[0, rows_d, c * CHUNK:(c + 1) * CHUNK] for c in range(nchunk_ctx)])
        store_prep(d, nchunk_ctx, [dt_ref[0, rows_d, c * CHUNK:(c + 1) * CHUNK] for c in range(nchunk)])

    for i in range(nchunk_ctx):
        for d in range(2):
            ci = i if d == 0 else nchunk_ctx - 1 - i
            rows = slice(ci * CHUNK, (ci + 1) * CHUNK)
            direction(d, xsc_ref[0, rows, :], btc_ref[0, ci], cmc_ref[0, rows, :], ci, 0, None)

    def finish(row0):
        xr = xs_ref[0, pl.ds(row0, CHUNK), :].astype(F32)
        zr = z_ref[0, pl.ds(row0, CHUNK), :].astype(F32)
        y = y_ref[pl.ds(row0, CHUNK), :] + xr * dskip_ref[...]
        y = y * zr
        out_ref[0, pl.ds(row0, CHUNK), :] = _rms(y, gssd_ref[...], n=SSD_WIDTH).astype(BF16)

    def step(i, carry, *, accumulate):
        row0s = []
        for d in range(2):
            ci = i if d == 0 else nchunk - 1 - i
            row0 = pl.multiple_of(ci * CHUNK, CHUNK)
            row0s.append(row0)
            direction(d, xs_ref[0, pl.ds(row0, CHUNK), :], bt_ref[0, ci],
                      cm_ref[0, pl.ds(row0, CHUNK), :], nchunk_ctx + ci, row0, accumulate)
        if accumulate:
            for row0 in row0s:
                finish(row0)
        return carry

    half = nchunk // 2
    lax.fori_loop(0, half, functools.partial(step, accumulate=False), 0, unroll=2)
    lax.fori_loop(half, nchunk, functools.partial(step, accumulate=True), 0, unroll=2)


def _ssd_call(xs, bm, cm, z, dt, xsc, bmc, cmc, dtc, dtb, alog, dskip, gssd):
    bsz, seq, _ = xs.shape
    ctx_len = xsc.shape[1]
    nck = (seq + ctx_len) // CHUNK
    per_b = lambda b: (b, 0, 0)
    const2 = lambda b: (0, 0)
    return pl.pallas_call(
        functools.partial(_ssd_kernel, seq=seq, ctx_len=ctx_len),
        grid=(bsz,),
        in_specs=[
            pl.BlockSpec((1, seq, SSD_PAD), per_b),
            pl.BlockSpec((1, seq // CHUNK, BC_WIDTH, CHUNK), lambda b: (b, 0, 0, 0)),
            pl.BlockSpec((1, seq, BC_WIDTH), per_b),
            pl.BlockSpec((1, seq, SSD_PAD), per_b),
            pl.BlockSpec((1, DT_ROWS, seq), per_b),
            pl.BlockSpec((1, ctx_len, SSD_PAD), per_b),
            pl.BlockSpec((1, ctx_len // CHUNK, BC_WIDTH, CHUNK), lambda b: (b, 0, 0, 0)),
            pl.BlockSpec((1, ctx_len, BC_WIDTH), per_b),
            pl.BlockSpec((1, DT_ROWS, ctx_len), per_b),
            pl.BlockSpec((DT_ROWS, 1), const2),
            pl.BlockSpec((DT_ROWS, 1), const2),
            pl.BlockSpec((1, SSD_PAD), const2),
            pl.BlockSpec((1, SSD_PAD), const2),
        ],
        out_specs=pl.BlockSpec((1, seq, SSD_PAD), per_b),
        out_shape=jax.ShapeDtypeStruct((bsz, seq, SSD_PAD), BF16),
        scratch_shapes=[
            pltpu.VMEM((2, SSD_GROUPS, D_STATE, GROUP_PAD), F32),
            pltpu.VMEM((seq, SSD_PAD), F32),
            pltpu.VMEM((2, nck, DT_DIR_ROWS, CHUNK), F32),
            pltpu.VMEM((2, nck, DT_DIR_ROWS, CHUNK), F32),
            pltpu.VMEM((2, nck, DT_DIR_ROWS, CHUNK), F32),
            pltpu.VMEM((2, nck, CHUNK, CHUNK), F32),
        ],
        compiler_params=pltpu.CompilerParams(
            dimension_semantics=("arbitrary",), vmem_limit_bytes=VMEM_LIMIT),
        name="ssd_scan",
    )(xs, bm, cm, z, dt, xsc, bmc, cmc, dtc, dtb, alog, dskip, gssd)


FFN_CHUNK = 256
MIX_SPLIT = 2


def _out_ffn_kernel(x_ref, yn_ref, uu_ref, cl_ref, sl_ref, mod_ref,
                    wof_ref, woy_ref, gpm_ref, gpf_ref, gpo_ref,
                    wg_ref, wu_ref, wd_ref, out_ref):
    b = pl.program_id(1)

    def mod(k):
        return mod_ref[pl.ds(b, 1), k * D_MODEL:(k + 1) * D_MODEL]

    half = x_ref.shape[1] // MIX_SPLIT
    x1_parts, h2_parts = [], []
    for r0 in range(0, x_ref.shape[1], half):
        rows = slice(r0, r0 + half)
        yf = (_dot(cl_ref[rows, :], uu_ref[0, :, 0:F_WIDTH])
              + _dot(sl_ref[rows, :], uu_ref[0, :, F_WIDTH:2 * F_WIDTH]))
        mix = _dot(yf.astype(BF16), wof_ref[...]) + _dot(yn_ref[0, rows, :], woy_ref[...])
        x1_h = x_ref[0, rows, :] + mod(2) * _rms(mix, gpm_ref[...])
        x1_parts.append(x1_h)
        h2_parts.append((_rms(x1_h, gpf_ref[...]) * (1.0 + mod(4)) + mod(3)).astype(BF16))
    x1 = jnp.concatenate(x1_parts, axis=0)
    h2 = jnp.concatenate(h2_parts, axis=0)
    ffn = None
    for c0 in range(0, wg_ref.shape[1], FFN_CHUNK):
        gate = _dot(h2, wg_ref[:, c0:c0 + FFN_CHUNK])
        up = _dot(h2, wu_ref[:, c0:c0 + FFN_CHUNK])
        act = (_silu(gate) * up).astype(BF16)
        part = _dot(act, wd_ref[c0:c0 + FFN_CHUNK, :])
        ffn = part if ffn is None else ffn + part
    out_ref[0] = x1 + mod(5) * _rms(ffn, gpo_ref[...])


def _out_ffn_call(x, yn, uu, cl, sl, mod, wof, woy, gpm, gpf, gpo, wg, wu, wd, *, tm):
    bsz, seq, _ = x.shape
    d_ff = wg.shape[1]
    nt = seq // tm
    const2 = lambda j, b: (0, 0)
    tile3 = lambda j, b: (b, j, 0)
    single = dict(pipeline_mode=pl.Buffered(1))
    return pl.pallas_call(
        _out_ffn_kernel,
        grid=(nt, bsz),
        in_specs=[
            pl.BlockSpec((1, tm, D_MODEL), tile3),
            pl.BlockSpec((1, tm, SSD_PAD), tile3),
            pl.BlockSpec((1, seq, 2 * F_WIDTH), lambda j, b: (b, 0, 0)),
            pl.BlockSpec((tm, seq), lambda j, b: (j, 0)),
            pl.BlockSpec((tm, seq), lambda j, b: (j, 0)),
            pl.BlockSpec((MOD_ROWS, N_MOD * D_MODEL), const2),
            pl.BlockSpec((F_WIDTH, D_MODEL), const2, **single),
            pl.BlockSpec((SSD_PAD, D_MODEL), const2, **single),
            pl.BlockSpec((1, D_MODEL), const2),
            pl.BlockSpec((1, D_MODEL), const2),
            pl.BlockSpec((1, D_MODEL), const2),
            pl.BlockSpec((D_MODEL, d_ff), const2, **single),
            pl.BlockSpec((D_MODEL, d_ff), const2, **single),
            pl.BlockSpec((d_ff, D_MODEL), const2, **single),
        ],
        out_specs=pl.BlockSpec((1, tm, D_MODEL), tile3),
        out_shape=jax.ShapeDtypeStruct((bsz, seq, D_MODEL), F32),
        compiler_params=pltpu.CompilerParams(
            dimension_semantics=("arbitrary", "arbitrary"),
            vmem_limit_bytes=VMEM_LIMIT),
        name="out_ffn",
    )(x, yn, uu, cl, sl, mod, wof, woy, gpm, gpf, gpo, wg, wu, wd)


@functools.lru_cache(maxsize=None)
def _dft_tables(seq):
    k = np.arange(seq, dtype=np.int64)
    ang = 2.0 * np.pi * ((k[:, None] * k[None, :]) % seq).astype(np.float64) / seq
    scale = 1.0 / math.sqrt(seq)
    cl = (np.cos(ang) * scale).astype(np.float32)
    sl = (np.sin(ang) * scale).astype(np.float32)
    c = np.arange(FGROUP_DIM, dtype=np.int64)
    angc = 2.0 * np.pi * ((c[:, None] * c[None, :]) % FGROUP_DIM).astype(np.float64) / FGROUP_DIM
    sc = 1.0 / math.sqrt(FGROUP_DIM)
    wc = np.zeros((F_WIDTH, 2 * F_WIDTH), np.float32)
    for g in range(N_FGROUPS):
        s = slice(g * FGROUP_DIM, (g + 1) * FGROUP_DIM)
        wc[s, g * FGROUP_DIM:(g + 1) * FGROUP_DIM] = np.cos(angc) * sc
        wc[s, F_WIDTH + g * FGROUP_DIM:F_WIDTH + (g + 1) * FGROUP_DIM] = -np.sin(angc) * sc
    return cl, sl, wc


def _pad_groups(a, axis):
    a = jnp.moveaxis(a, axis, -1)
    lead = a.shape[:-1]
    a = a.reshape(lead + (SSD_GROUPS, HEADS_PER_GROUP * HEAD_DIM))
    a = jnp.pad(a, [(0, 0)] * len(lead) + [(0, 0), (0, GROUP_PAD - HEADS_PER_GROUP * HEAD_DIM)])
    a = a.reshape(lead + (SSD_PAD,))
    return jnp.moveaxis(a, -1, axis)


def _pad_dirs(a):
    a = jnp.pad(a, ((0, 0), (0, DT_DIR_ROWS - SSD_HEADS)))
    return a.reshape(DT_ROWS, 1)


def kernel(x, c, ctx, c_ctx, w_ada, b_ada, g_pre_mix, g_post_mix, g_pre_ffn, g_post_ffn,
           w_in, conv_w, conv_b, dt_bias, a_log, d_skip, g_ssd, w_out, w_gate, w_up, w_down):
    bsz, seq, _ = x.shape
    ctx_len = ctx.shape[1]
    l = 0
    cl_np, sl_np, wc_np = _dft_tables(seq)
    cl = jnp.asarray(cl_np).astype(BF16)
    sl = jnp.asarray(sl_np).astype(BF16)
    wc = jnp.asarray(wc_np)

    cc = jnp.concatenate(
        [c, c_ctx[None, :], jnp.zeros((MOD_ROWS - bsz - 1, D_MODEL), F32)], axis=0)
    mod = _ada_call(cc, w_ada, b_ada[l][None, :], l)

    wi = w_in[l]
    w_cat = jnp.concatenate([
        wi[:, :F_WIDTH],
        _pad_groups(wi[:, F_WIDTH:XBC_OFF], 1),
        _pad_groups(wi[:, XBC_OFF:XBC_OFF + SSD_WIDTH], 1),
        wi[:, XBC_OFF + SSD_WIDTH:XBC_OFF + CONV_DIM],
    ], axis=1).astype(BF16)
    w_dt = wi[:, XBC_OFF + CONV_DIM:].T.reshape(2, SSD_HEADS, D_MODEL)
    w_dt = jnp.pad(w_dt, ((0, 0), (0, DT_DIR_ROWS - SSD_HEADS), (0, 0))).reshape(DT_ROWS, D_MODEL)
    w_dt = w_dt.astype(BF16)
    cw = conv_w[l].T
    cw = jnp.concatenate([_pad_groups(cw[:, :SSD_WIDTH], 1), cw[:, SSD_WIDTH:]], axis=1)
    cb = conv_b[l][None, :]
    cb = jnp.concatenate([_pad_groups(cb[:, :SSD_WIDTH], 1), cb[:, SSD_WIDTH:]], axis=1)
    g_pre = g_pre_mix[l][None, :]

    xs_c, bm_c, cm_c, dt_c = _inproj_call(
        ctx, mod, g_pre, w_cat, w_dt, cw, cb, None, ctx=True, tm=ctx_len, row_len=ctx_len)
    uu, z, xs, bm, cm, dt = _inproj_call(
        x, mod, g_pre, w_cat, w_dt, cw, cb, wc, ctx=False, tm=512, row_len=GRID_W)

    dskip = _pad_groups(jnp.repeat(d_skip[l], HEAD_DIM)[None, :], 1)
    gssd = _pad_groups(g_ssd[l][None, :], 1)
    yn = _ssd_call(xs, bm, cm, z, dt, xs_c, bm_c, cm_c, dt_c,
                   _pad_dirs(dt_bias[l]), _pad_dirs(a_log[l]), dskip, gssd)

    wo = w_out[l]
    wof = wo[:F_WIDTH].astype(BF16)
    woy = _pad_groups(wo[F_WIDTH:], 0).astype(BF16)
    return _out_ffn_call(
        x, yn, uu, cl, sl, mod, wof, woy,
        g_post_mix[l][None, :], g_pre_ffn[l][None, :], g_post_ffn[l][None, :],
        w_gate[l].astype(BF16), w_up[l].astype(BF16), w_down[l].astype(BF16), tm=512)
```

```python
import functools
import math

import jax
import jax.numpy as jnp
import numpy as np
from jax import lax
from jax.experimental import pallas as pl
from jax.experimental.pallas import tpu as pltpu

F32 = jnp.float32
BF16 = jnp.bfloat16

D_MODEL = 1024
GRID_W = 64
F_WIDTH = 256
N_FGROUPS = 4
FGROUP_DIM = F_WIDTH // N_FGROUPS
SSD_WIDTH = 768
HEAD_DIM = 64
SSD_HEADS = 12
SSD_GROUPS = 4
HEADS_PER_GROUP = 3
D_STATE = 128
BC_WIDTH = SSD_GROUPS * D_STATE
D_CONV = 7
HALF_CONV = D_CONV // 2
CHUNK = 128
XBC_OFF = F_WIDTH + SSD_WIDTH
CONV_DIM = SSD_WIDTH + 2 * BC_WIDTH
N_MOD = 6
EPS = 1e-6
LOG2E = math.log2(math.e)

GROUP_PAD = 256
SSD_PAD = SSD_GROUPS * GROUP_PAD
XBC_PAD = SSD_PAD + 2 * BC_WIDTH
W_COLS = F_WIDTH + SSD_PAD + XBC_PAD
CTX_COL0 = F_WIDTH + SSD_PAD
DT_ROWS = 32
DT_DIR_ROWS = 16
MOD_ROWS = 16
MOD_CTX_ROW = 8

SUBLANE = 8
LANE = 128
VMEM_LIMIT = 56 * 1024 * 1024


def _dot(a, b):
    return jnp.dot(a, b, preferred_element_type=F32)


def _dot_mixed(a_f32, b_bf16):
    return lax.dot_general(a_f32, b_bf16, (((1,), (0,)), ((), ())), preferred_element_type=F32)


def _silu(v):
    return v * jax.nn.sigmoid(v)


def _rms(v, g, n=None):
    n = v.shape[-1] if n is None else n
    ms = jnp.sum(v * v, axis=-1, keepdims=True) * (1.0 / n)
    return v * lax.rsqrt(ms + EPS) * g


def _ada_kernel(c_ref, w_ref, b_ref, o_ref):
    s = _silu(c_ref[...]).astype(BF16)
    o_ref[...] = _dot(s, w_ref[...].astype(BF16)) + b_ref[...]


def _ada_call(cc, w_ada, b_ada, layer):
    n = w_ada.shape[2]
    tn = 768
    return pl.pallas_call(
        _ada_kernel,
        grid=(n // tn,),
        in_specs=[
            pl.BlockSpec((MOD_ROWS, D_MODEL), lambda j: (0, 0)),
            pl.BlockSpec((None, D_MODEL, tn), lambda j: (layer, 0, j)),
            pl.BlockSpec((1, tn), lambda j: (0, j)),
        ],
        out_specs=pl.BlockSpec((MOD_ROWS, tn), lambda j: (0, j)),
        out_shape=jax.ShapeDtypeStruct((MOD_ROWS, n), F32),
        compiler_params=pltpu.CompilerParams(dimension_semantics=("arbitrary",)),
        name="ada_mod",
    )(cc, w_ada, b_ada)


def _wprep_kernel(w_ref, o_ref):
    w = w_ref[...]
    rows = w.shape[0]
    group_w = HEADS_PER_GROUP * HEAD_DIM
    o_ref[:, 0:F_WIDTH] = w[:, 0:F_WIDTH].astype(BF16)
    zslot = jnp.zeros((rows, GROUP_PAD - group_w), BF16)
    for seg in range(2):
        for g in range(SSD_GROUPS):
            src = F_WIDTH + seg * SSD_WIDTH + g * group_w
            dst = F_WIDTH + seg * SSD_PAD + g * GROUP_PAD
            o_ref[:, dst:dst + group_w] = w[:, src:src + group_w].astype(BF16)
            o_ref[:, dst + group_w:dst + GROUP_PAD] = zslot
    o_ref[:, CTX_COL0 + SSD_PAD:W_COLS] = w[:, XBC_OFF + SSD_WIDTH:XBC_OFF + CONV_DIM].astype(BF16)


def _wprep_call(w_in, layer):
    _, d, n_in = w_in.shape
    tr = 256
    return pl.pallas_call(
        _wprep_kernel,
        grid=(d // tr,),
        in_specs=[pl.BlockSpec((None, tr, n_in), lambda i: (layer, i, 0))],
        out_specs=pl.BlockSpec((tr, W_COLS), lambda i: (i, 0)),
        out_shape=jax.ShapeDtypeStruct((d, W_COLS), BF16),
        compiler_params=pltpu.CompilerParams(dimension_semantics=("arbitrary",)),
        name="w_in_layout",
    )(w_in)


CONV_STRIDE = SUBLANE + 1
CONV_PIECE = SUBLANE * CONV_STRIDE


def _conv_pieces(tm, row_len):
    padded = (tm // row_len) * (row_len + SUBLANE)
    return -(-padded // CONV_PIECE)


def _inproj_kernel(*refs, ctx, tm, row_len):
    if ctx:
        (x_ref, mod_ref, g_ref, w_ref, wdt_ref, cw_ref, cb_ref,
         xs_ref, bt_ref, cm_ref, dt_ref, pad_ref, cv_ref) = refs
    else:
        (x_ref, mod_ref, g_ref, w_ref, wdt_ref, cw_ref, cb_ref, wc_ref,
         uu_ref, z_ref, xs_ref, bt_ref, cm_ref, dt_ref, pad_ref, cv_ref) = refs

    xt = x_ref[0]
    if ctx:
        shift = mod_ref[MOD_CTX_ROW:MOD_CTX_ROW + 1, 0:D_MODEL]
        scale = mod_ref[MOD_CTX_ROW:MOD_CTX_ROW + 1, D_MODEL:2 * D_MODEL]
    else:
        b = pl.program_id(0)
        shift = mod_ref[pl.ds(b, 1), 0:D_MODEL]
        scale = mod_ref[pl.ds(b, 1), D_MODEL:2 * D_MODEL]
    h = _rms(xt, g_ref[...]) * (1.0 + scale) + shift
    hb = h.astype(BF16)

    if not ctx:
        uf = _dot(hb, w_ref[:, 0:F_WIDTH])
        uu_ref[0] = _dot(uf.astype(BF16), wc_ref[...].astype(BF16)).astype(BF16)
        z_ref[0] = _silu(_dot(hb, w_ref[:, F_WIDTH:CTX_COL0])).astype(BF16)
        xbc = _dot(hb, w_ref[:, CTX_COL0:W_COLS])
    else:
        xbc = _dot(hb, w_ref[:, CTX_COL0:W_COLS])
    dt_ref[0] = lax.dot_general(wdt_ref[...], hb, (((1,), (1,)), ((), ())),
                                preferred_element_type=F32)

    pitch = row_len + SUBLANE
    nrow = tm // row_len
    npiece = _conv_pieces(tm, row_len)
    data_end = SUBLANE + nrow * pitch
    alloc_end = pad_ref.shape[1]
    zrow = jnp.zeros((SUBLANE, LANE), F32)
    for j in range(XBC_PAD // LANE):
        pad_ref[j, 0:SUBLANE, :] = zrow
        for r in range(nrow):
            base = SUBLANE + r * pitch
            pad_ref[j, base:base + row_len, :] = xbc[r * row_len:(r + 1) * row_len,
                                                     j * LANE:(j + 1) * LANE]
            pad_ref[j, base + row_len:base + pitch, :] = zrow
        for z0 in range(data_end, alloc_end, SUBLANE):
            pad_ref[j, z0:z0 + SUBLANE, :] = zrow

    for j in range(XBC_PAD // LANE):
        wk = [jnp.broadcast_to(cw_ref[k:k + 1, j * LANE:(j + 1) * LANE], (SUBLANE, LANE))
              for k in range(D_CONV)]
        bias = jnp.broadcast_to(cb_ref[:, j * LANE:(j + 1) * LANE], (SUBLANE, LANE))
        for p in range(npiece):
            base = SUBLANE + p * CONV_PIECE
            wins = {v: pad_ref[j, pl.ds(base + v, SUBLANE, stride=CONV_STRIDE), :]
                    for v in range(-HALF_CONV, CONV_STRIDE + HALF_CONV)}
            for a in range(CONV_STRIDE):
                acc = bias
                for k in range(D_CONV):
                    acc = acc + wins[a + k - HALF_CONV] * wk[k]
                cv_ref[j, pl.ds(base + a, SUBLANE, stride=CONV_STRIDE), :] = acc

    def conv_rows(j, t0, n):
        parts = []
        for t in range(t0, t0 + n, min(n, row_len)):
            src = SUBLANE + (t // row_len) * pitch + t % row_len
            parts.append(cv_ref[j, src:src + min(n, row_len), :])
        return _silu(parts[0] if len(parts) == 1 else jnp.concatenate(parts, axis=0))

    for j in range(XBC_PAD // LANE):
        c0 = j * LANE
        for ci in range(tm // CHUNK):
            val = conv_rows(j, ci * CHUNK, CHUNK)
            rows = slice(ci * CHUNK, (ci + 1) * CHUNK)
            if c0 < SSD_PAD:
                xs_ref[0, rows, c0:c0 + LANE] = val.astype(BF16)
            elif c0 < SSD_PAD + BC_WIDTH:
                cc = c0 - SSD_PAD
                bt_ref[0, ci, cc:cc + LANE, :] = val.T.astype(BF16)
            else:
                cc = c0 - SSD_PAD - BC_WIDTH
                cm_ref[0, rows, cc:cc + LANE] = val.astype(BF16)


def _inproj_call(xin, mod, g, w, wdt, cw, cb, wc, *, ctx, tm, row_len):
    bsz, seq, _ = xin.shape
    nt = seq // tm
    const2 = lambda b, j: (0, 0)
    tile3 = lambda b, j: (b, j, 0)
    in_specs = [
        pl.BlockSpec((1, tm, D_MODEL), tile3),
        pl.BlockSpec((MOD_ROWS, N_MOD * D_MODEL), const2),
        pl.BlockSpec((1, D_MODEL), const2),
        pl.BlockSpec((D_MODEL, W_COLS), const2),
        pl.BlockSpec((DT_ROWS, D_MODEL), const2),
        pl.BlockSpec((D_CONV, XBC_PAD), const2),
        pl.BlockSpec((1, XBC_PAD), const2),
    ]
    args = [xin, mod, g, w, wdt, cw, cb]
    out_specs = []
    out_shape = []
    if not ctx:
        in_specs.append(pl.BlockSpec((F_WIDTH, 2 * F_WIDTH), const2))
        args.append(wc)
        out_specs += [pl.BlockSpec((1, tm, 2 * F_WIDTH), tile3),
                      pl.BlockSpec((1, tm, SSD_PAD), tile3)]
        out_shape += [jax.ShapeDtypeStruct((bsz, seq, 2 * F_WIDTH), BF16),
                      jax.ShapeDtypeStruct((bsz, seq, SSD_PAD), BF16)]
    out_specs += [pl.BlockSpec((1, tm, SSD_PAD), tile3),
                  pl.BlockSpec((1, tm // CHUNK, BC_WIDTH, CHUNK), lambda b, j: (b, j, 0, 0)),
                  pl.BlockSpec((1, tm, BC_WIDTH), tile3),
                  pl.BlockSpec((1, DT_ROWS, tm), lambda b, j: (b, 0, j))]
    out_shape += [jax.ShapeDtypeStruct((bsz, seq, SSD_PAD), BF16),
                  jax.ShapeDtypeStruct((bsz, seq // CHUNK, BC_WIDTH, CHUNK), BF16),
                  jax.ShapeDtypeStruct((bsz, seq, BC_WIDTH), BF16),
                  jax.ShapeDtypeStruct((bsz, DT_ROWS, seq), F32)]
    pad_rows = 2 * SUBLANE + _conv_pieces(tm, row_len) * CONV_PIECE
    conv_scratch = pltpu.VMEM((XBC_PAD // LANE, pad_rows, LANE), F32)
    return pl.pallas_call(
        functools.partial(_inproj_kernel, ctx=ctx, tm=tm, row_len=row_len),
        grid=(bsz, nt),
        in_specs=in_specs,
        out_specs=out_specs,
        out_shape=out_shape,
        scratch_shapes=[conv_scratch, conv_scratch],
        compiler_params=pltpu.CompilerParams(
            dimension_semantics=("arbitrary", "arbitrary"),
            vmem_limit_bytes=VMEM_LIMIT),
        name="inproj_ctx" if ctx else "inproj_lat",
    )(*args)


def _ssd_kernel(xs_ref, bt_ref, cm_ref, z_ref, dt_ref,
                xsc_ref, btc_ref, cmc_ref, dtc_ref,
                dtb_ref, alog_ref, dskip_ref, gssd_ref,
                out_ref, h_ref, y_ref, r2_ref, w2_ref, ee_ref, qt_ref, *, seq, ctx_len):
    nchunk = seq // CHUNK
    nchunk_ctx = ctx_len // CHUNK

    h_ref[...] = jnp.zeros(h_ref.shape, F32)

    bias = dtb_ref[...]
    nega = -jnp.exp(alog_ref[...])
    sub_i = lax.broadcasted_iota(jnp.int32, (CHUNK, CHUNK), 0)
    lane_i = lax.broadcasted_iota(jnp.int32, (CHUNK, CHUNK), 1)
    tri = (lane_i <= sub_i, lane_i >= sub_i)
    lane_lo = lane_i < HEAD_DIM
    lane_row = lax.broadcasted_iota(jnp.int32, (1, CHUNK), 1)

    def colb(mat_t, idx):
        return jnp.broadcast_to(mat_t[:, idx:idx + 1], (CHUNK, CHUNK))

    def prepare(d, raws):
        r0 = d * DT_DIR_ROWS
        n = len(raws)
        v = jnp.concatenate(raws, axis=0) + jnp.concatenate([bias[r0:r0 + DT_DIR_ROWS]] * n, axis=0)
        dt = jnp.maximum(v, 0.0) + jnp.log1p(jnp.exp(-jnp.abs(v)))
        da = dt * jnp.concatenate([nega[r0:r0 + DT_DIR_ROWS]] * n, axis=0)
        lane_n = lax.broadcasted_iota(jnp.int32, da.shape, 1)
        cs = da
        sh = 1
        while sh < CHUNK:
            if d == 0:
                cs = cs + jnp.where(lane_n >= sh, pltpu.roll(cs, sh, axis=1), 0.0)
            else:
                cs = cs + jnp.where(lane_n < CHUNK - sh, pltpu.roll(cs, CHUNK - sh, axis=1), 0.0)
            sh *= 2
        a_end = jnp.sum(da, axis=1, keepdims=True)
        cs2 = cs * LOG2E
        r2 = cs2 - jnp.log2(dt)
        w2 = jnp.exp2(a_end * LOG2E - r2)
        ee = jnp.broadcast_to(jnp.exp(a_end), da.shape)
        return cs2, r2, w2, ee

    def store_prep(d, k0, raws):
        cs2, r2, w2, ee = prepare(d, raws)
        for i in range(len(raws)):
            rows = slice(i * DT_DIR_ROWS, (i + 1) * DT_DIR_ROWS)
            r2_ref[d, k0 + i] = r2[rows]
            w2_ref[d, k0 + i] = w2[rows]
            ee_ref[d, k0 + i] = ee[rows]
            qt_ref[d, k0 + i] = jnp.concatenate(
                [cs2[rows], jnp.zeros((CHUNK - DT_DIR_ROWS, CHUNK), F32)], axis=0).T

    def direction(d, x, btc, cc, k, row0, mode):
        with_output = mode is not None
        r2 = r2_ref[d, k]
        w2 = w2_ref[d, k]
        ee = ee_ref[d, k]
        q_t = qt_ref[d, k]
        if with_output:
            scores_all = [_dot(cc[:, g * D_STATE:(g + 1) * D_STATE],
                               btc[g * D_STATE:(g + 1) * D_STATE, :]) for g in range(SSD_GROUPS)]

        s_all = []
        for g in range(SSD_GROUPS):
            bt = btc[g * D_STATE:(g + 1) * D_STATE, :]
            xg = x[:, g * GROUP_PAD:(g + 1) * GROUP_PAD]
            heads = [g * HEADS_PER_GROUP + r for r in range(HEADS_PER_GROUP)]
            btf = bt.astype(F32)
            btw = [(btf * w2[hd:hd + 1, :]).astype(BF16) for hd in heads]
            s01 = _dot(jnp.concatenate([btw[0], btw[1]], axis=0), xg[:, 0:CHUNK])
            s0 = jnp.where(lane_lo, s01[0:D_STATE], s01[D_STATE:2 * D_STATE])
            s1 = _dot(btw[2], xg[:, CHUNK:2 * CHUNK])
            s_all.append(jnp.concatenate([s0, s1], axis=1))

        for g in range(SSD_GROUPS):
            cg = cc[:, g * D_STATE:(g + 1) * D_STATE]
            hg = h_ref[d, g]
            xg = x[:, g * GROUP_PAD:(g + 1) * GROUP_PAD]
            t0 = xg[:, 0:CHUNK]
            t1 = xg[:, CHUNK:2 * CHUNK]
            heads = [g * HEADS_PER_GROUP + r for r in range(HEADS_PER_GROUP)]
            if with_output:
                scores = scores_all[g]
                hb = hg.astype(BF16)
                lhs = []
                cgf = cg.astype(F32)
                for hd in heads:
                    col = colb(q_t, hd)
                    seg = col - r2[hd:hd + 1, :]
                    m = scores * jnp.exp2(jnp.where(tri[d], seg, -jnp.inf))
                    ce = cgf * jnp.exp2(col)
                    lhs.append(jnp.concatenate([m, ce], axis=1))
                rhs0 = jnp.concatenate([t0, hb[:, 0:CHUNK]], axis=0)
                rhs1 = jnp.concatenate([t1, hb[:, CHUNK:2 * CHUNK]], axis=0)
                y01 = _dot_mixed(jnp.concatenate([lhs[0], lhs[1]], axis=0), rhs0)
                y0 = jnp.where(lane_lo, y01[0:CHUNK], y01[CHUNK:2 * CHUNK])
                y1 = _dot_mixed(lhs[2], rhs1)
                y_g = jnp.concatenate([y0, y1], axis=1)
                if mode:
                    y_g = y_g + y_ref[pl.ds(row0, CHUNK), g * GROUP_PAD:(g + 1) * GROUP_PAD]
                y_ref[pl.ds(row0, CHUNK), g * GROUP_PAD:(g + 1) * GROUP_PAD] = y_g
            er = [ee[hd:hd + 1, :] for hd in heads]
            e3 = jnp.concatenate([jnp.where(lane_row < HEAD_DIM, er[0], er[1]), er[2]], axis=1)
            h_ref[d, g] = e3 * hg + s_all[g]

    for d in range(2):
        rows_d = slice(d * DT_DIR_ROWS, (d + 1) * DT_DIR_ROWS)
        store_prep(d, 0, [dtc_ref[0, rows_d, c * CHUNK:(c + 1) * CHUNK] for c in range(nchunk_ctx)])
        store_prep(d, nchunk_ctx, [dt_ref[0, rows_d, c * CHUNK:(c + 1) * CHUNK] for c in range(nchunk)])

    for i in range(nchunk_ctx):
        for d in range(2):
            ci = i if d == 0 else nchunk_ctx - 1 - i
            rows = slice(ci * CHUNK, (ci + 1) * CHUNK)
            direction(d, xsc_ref[0, rows, :], btc_ref[0, ci], cmc_ref[0, rows, :], ci, 0, None)

    def finish(row0):
        xr = xs_ref[0, pl.ds(row0, CHUNK), :].astype(F32)
        zr = z_ref[0, pl.ds(row0, CHUNK), :].astype(F32)
        y = y_ref[pl.ds(row0, CHUNK), :] + xr * dskip_ref[...]
        y = y * zr
        out_ref[0, pl.ds(row0, CHUNK), :] = _rms(y, gssd_ref[...], n=SSD_WIDTH).astype(BF16)

    def step(i, carry, *, accumulate):
        row0s = []
        for d in range(2):
            ci = i if d == 0 else nchunk - 1 - i
            row0 = pl.multiple_of(ci * CHUNK, CHUNK)
            row0s.append(row0)
            direction(d, xs_ref[0, pl.ds(row0, CHUNK), :], bt_ref[0, ci],
                      cm_ref[0, pl.ds(row0, CHUNK), :], nchunk_ctx + ci, row0, accumulate)
        if accumulate:
            for row0 in row0s:
                finish(row0)
        return carry

    half = nchunk // 2
    lax.fori_loop(0, half, functools.partial(step, accumulate=False), 0, unroll=2)
    lax.fori_loop(half, nchunk, functools.partial(step, accumulate=True), 0, unroll=2)


def _ssd_call(xs, bm, cm, z, dt, xsc, bmc, cmc, dtc, dtb, alog, dskip, gssd):
    bsz, seq, _ = xs.shape
    ctx_len = xsc.shape[1]
    nck = (seq + ctx_len) // CHUNK
    per_b = lambda b: (b, 0, 0)
    const2 = lambda b: (0, 0)
    return pl.pallas_call(
        functools.partial(_ssd_kernel, seq=seq, ctx_len=ctx_len),
        grid=(bsz,),
        in_specs=[
            pl.BlockSpec((1, seq, SSD_PAD), per_b),
            pl.BlockSpec((1, seq // CHUNK, BC_WIDTH, CHUNK), lambda b: (b, 0, 0, 0)),
            pl.BlockSpec((1, seq, BC_WIDTH), per_b),
            pl.BlockSpec((1, seq, SSD_PAD), per_b),
            pl.BlockSpec((1, DT_ROWS, seq), per_b),
            pl.BlockSpec((1, ctx_len, SSD_PAD), per_b),
            pl.BlockSpec((1, ctx_len // CHUNK, BC_WIDTH, CHUNK), lambda b: (b, 0, 0, 0)),
            pl.BlockSpec((1, ctx_len, BC_WIDTH), per_b),
            pl.BlockSpec((1, DT_ROWS, ctx_len), per_b),
            pl.BlockSpec((DT_ROWS, 1), const2),
            pl.BlockSpec((DT_ROWS, 1), const2),
            pl.BlockSpec((1, SSD_PAD), const2),
            pl.BlockSpec((1, SSD_PAD), const2),
        ],
        out_specs=pl.BlockSpec((1, seq, SSD_PAD), per_b),
        out_shape=jax.ShapeDtypeStruct((bsz, seq, SSD_PAD), BF16),
        scratch_shapes=[
            pltpu.VMEM((2, SSD_GROUPS, D_STATE, GROUP_PAD), F32),
            pltpu.VMEM((seq, SSD_PAD), F32),
            pltpu.VMEM((2, nck, DT_DIR_ROWS, CHUNK), F32),
            pltpu.VMEM((2, nck, DT_DIR_ROWS, CHUNK), F32),
            pltpu.VMEM((2, nck, DT_DIR_ROWS, CHUNK), F32),
            pltpu.VMEM((2, nck, CHUNK, CHUNK), F32),
        ],
        compiler_params=pltpu.CompilerParams(
            dimension_semantics=("arbitrary",), vmem_limit_bytes=VMEM_LIMIT),
        name="ssd_scan",
    )(xs, bm, cm, z, dt, xsc, bmc, cmc, dtc, dtb, alog, dskip, gssd)


FFN_CHUNK = 256
MIX_SPLIT = 2


def _out_ffn_kernel(x_ref, yn_ref, uu_ref, cl_ref, sl_ref, mod_ref,
                    wof_ref, woy_ref, gpm_ref, gpf_ref, gpo_ref,
                    wg_ref, wu_ref, wd_ref, out_ref):
    b = pl.program_id(1)

    def mod(k):
        return mod_ref[pl.ds(b, 1), k * D_MODEL:(k + 1) * D_MODEL]

    half = x_ref.shape[1] // MIX_SPLIT
    x1_parts, h2_parts = [], []
    for r0 in range(0, x_ref.shape[1], half):
        rows = slice(r0, r0 + half)
        yf = (_dot(cl_ref[rows, :], uu_ref[0, :, 0:F_WIDTH])
              + _dot(sl_ref[rows, :], uu_ref[0, :, F_WIDTH:2 * F_WIDTH]))
        mix = _dot(yf.astype(BF16), wof_ref[...]) + _dot(yn_ref[0, rows, :], woy_ref[...])
        x1_h = x_ref[0, rows, :] + mod(2) * _rms(mix, gpm_ref[...])
        x1_parts.append(x1_h)
        h2_parts.append((_rms(x1_h, gpf_ref[...]) * (1.0 + mod(4)) + mod(3)).astype(BF16))
    x1 = jnp.concatenate(x1_parts, axis=0)
    h2 = jnp.concatenate(h2_parts, axis=0)
    ffn = None
    for c0 in range(0, wg_ref.shape[1], FFN_CHUNK):
        gate = _dot(h2, wg_ref[:, c0:c0 + FFN_CHUNK])
        up = _dot(h2, wu_ref[:, c0:c0 + FFN_CHUNK])
        act = (_silu(gate) * up).astype(BF16)
        part = _dot(act, wd_ref[c0:c0 + FFN_CHUNK, :])
        ffn = part if ffn is None else ffn + part
    out_ref[0] = x1 + mod(5) * _rms(ffn, gpo_ref[...])


def _out_ffn_call(x, yn, uu, cl, sl, mod, wof, woy, gpm, gpf, gpo, wg, wu, wd, *, tm):
    bsz, seq, _ = x.shape
    d_ff = wg.shape[1]
    nt = seq // tm
    const2 = lambda j, b: (0, 0)
    tile3 = lambda j, b: (b, j, 0)
    single = dict(pipeline_mode=pl.Buffered(1))
    return pl.pallas_call(
        _out_ffn_kernel,
        grid=(nt, bsz),
        in_specs=[
            pl.BlockSpec((1, tm, D_MODEL), tile3),
            pl.BlockSpec((1, tm, SSD_PAD), tile3),
            pl.BlockSpec((1, seq, 2 * F_WIDTH), lambda j, b: (b, 0, 0)),
            pl.BlockSpec((tm, seq), lambda j, b: (j, 0)),
            pl.BlockSpec((tm, seq), lambda j, b: (j, 0)),
            pl.BlockSpec((MOD_ROWS, N_MOD * D_MODEL), const2),
            pl.BlockSpec((F_WIDTH, D_MODEL), const2, **single),
            pl.BlockSpec((SSD_PAD, D_MODEL), const2, **single),
            pl.BlockSpec((1, D_MODEL), const2),
            pl.BlockSpec((1, D_MODEL), const2),
            pl.BlockSpec((1, D_MODEL), const2),
            pl.BlockSpec((D_MODEL, d_ff), const2, **single),
            pl.BlockSpec((D_MODEL, d_ff), const2, **single),
            pl.BlockSpec((d_ff, D_MODEL), const2, **single),
        ],
        out_specs=pl.BlockSpec((1, tm, D_MODEL), tile3),
        out_shape=jax.ShapeDtypeStruct((bsz, seq, D_MODEL), F32),
        compiler_params=pltpu.CompilerParams(
            dimension_semantics=("arbitrary", "arbitrary"),
            vmem_limit_bytes=VMEM_LIMIT),
        name="out_ffn",
    )(x, yn, uu, cl, sl, mod, wof, woy, gpm, gpf, gpo, wg, wu, wd)


@functools.lru_cache(maxsize=None)
def _dft_tables(seq):
    k = np.arange(seq, dtype=np.int64)
    ang = 2.0 * np.pi * ((k[:, None] * k[None, :]) % seq).astype(np.float64) / seq
    scale = 1.0 / math.sqrt(seq)
    cl = (np.cos(ang) * scale).astype(np.float32)
    sl = (np.sin(ang) * scale).astype(np.float32)
    c = np.arange(FGROUP_DIM, dtype=np.int64)
    angc = 2.0 * np.pi * ((c[:, None] * c[None, :]) % FGROUP_DIM).astype(np.float64) / FGROUP_DIM
    sc = 1.0 / math.sqrt(FGROUP_DIM)
    wc = np.zeros((F_WIDTH, 2 * F_WIDTH), np.float32)
    for g in range(N_FGROUPS):
        s = slice(g * FGROUP_DIM, (g + 1) * FGROUP_DIM)
        wc[s, g * FGROUP_DIM:(g + 1) * FGROUP_DIM] = np.cos(angc) * sc
        wc[s, F_WIDTH + g * FGROUP_DIM:F_WIDTH + (g + 1) * FGROUP_DIM] = -np.sin(angc) * sc
    return cl, sl, wc


def _pad_groups(a, axis):
    a = jnp.moveaxis(a, axis, -1)
    lead = a.shape[:-1]
    a = a.reshape(lead + (SSD_GROUPS, HEADS_PER_GROUP * HEAD_DIM))
    a = jnp.pad(a, [(0, 0)] * len(lead) + [(0, 0), (0, GROUP_PAD - HEADS_PER_GROUP * HEAD_DIM)])
    a = a.reshape(lead + (SSD_PAD,))
    return jnp.moveaxis(a, -1, axis)


def _pad_dirs(a):
    a = jnp.pad(a, ((0, 0), (0, DT_DIR_ROWS - SSD_HEADS)))
    return a.reshape(DT_ROWS, 1)


def kernel(x, c, ctx, c_ctx, w_ada, b_ada, g_pre_mix, g_post_mix, g_pre_ffn, g_post_ffn,
           w_in, conv_w, conv_b, dt_bias, a_log, d_skip, g_ssd, w_out, w_gate, w_up, w_down):
    bsz, seq, _ = x.shape
    ctx_len = ctx.shape[1]
    l = 0
    cl_np, sl_np, wc_np = _dft_tables(seq)
    cl = jnp.asarray(cl_np).astype(BF16)
    sl = jnp.asarray(sl_np).astype(BF16)
    wc = jnp.asarray(wc_np)

    cc = jnp.concatenate(
        [c, c_ctx[None, :], jnp.zeros((MOD_ROWS - bsz - 1, D_MODEL), F32)], axis=0)
    mod = _ada_call(cc, w_ada, b_ada[l][None, :], l)

    wi = w_in[l]
    w_cat = _wprep_call(w_in, l)
    w_dt = wi[:, XBC_OFF + CONV_DIM:].T.reshape(2, SSD_HEADS, D_MODEL)
    w_dt = jnp.pad(w_dt, ((0, 0), (0, DT_DIR_ROWS - SSD_HEADS), (0, 0))).reshape(DT_ROWS, D_MODEL)
    w_dt = w_dt.astype(BF16)
    cw = conv_w[l].T
    cw = jnp.concatenate([_pad_groups(cw[:, :SSD_WIDTH], 1), cw[:, SSD_WIDTH:]], axis=1)
    cb = conv_b[l][None, :]
    cb = jnp.concatenate([_pad_groups(cb[:, :SSD_WIDTH], 1), cb[:, SSD_WIDTH:]], axis=1)
    g_pre = g_pre_mix[l][None, :]

    xs_c, bm_c, cm_c, dt_c = _inproj_call(
        ctx, mod, g_pre, w_cat, w_dt, cw, cb, None, ctx=True, tm=ctx_len, row_len=ctx_len)
    uu, z, xs, bm, cm, dt = _inproj_call(
        x, mod, g_pre, w_cat, w_dt, cw, cb, wc, ctx=False, tm=512, row_len=GRID_W)

    dskip = _pad_groups(jnp.repeat(d_skip[l], HEAD_DIM)[None, :], 1)
    gssd = _pad_groups(g_ssd[l][None, :], 1)
    yn = _ssd_call(xs, bm, cm, z, dt, xs_c, bm_c, cm_c, dt_c,
                   _pad_dirs(dt_bias[l]), _pad_dirs(a_log[l]), dskip, gssd)

    wo = w_out[l]
    wof = wo[:F_WIDTH].astype(BF16)
    woy = _pad_groups(wo[F_WIDTH:], 0).astype(BF16)
    return _out_ffn_call(
        x, yn, uu, cl, sl, mod, wof, woy,
        g_post_mix[l][None, :], g_pre_ffn[l][None, :], g_post_ffn[l][None, :],
        w_gate[l].astype(BF16), w_up[l].astype(BF16), w_down[l].astype(BF16), tm=512)
```

```python
import functools
import math

import jax
import jax.numpy as jnp
import numpy as np
from jax import lax
from jax.experimental import pallas as pl
from jax.experimental.pallas import tpu as pltpu

F32 = jnp.float32
BF16 = jnp.bfloat16

D_MODEL = 1024
GRID_W = 64
F_WIDTH = 256
N_FGROUPS = 4
FGROUP_DIM = F_WIDTH // N_FGROUPS
SSD_WIDTH = 768
HEAD_DIM = 64
SSD_HEADS = 12
SSD_GROUPS = 4
HEADS_PER_GROUP = 3
D_STATE = 128
BC_WIDTH = SSD_GROUPS * D_STATE
D_CONV = 7
HALF_CONV = D_CONV // 2
CHUNK = 128
XBC_OFF = F_WIDTH + SSD_WIDTH
CONV_DIM = SSD_WIDTH + 2 * BC_WIDTH
N_MOD = 6
EPS = 1e-6
LOG2E = math.log2(math.e)

GROUP_PAD = 256
SSD_PAD = SSD_GROUPS * GROUP_PAD
XBC_PAD = SSD_PAD + 2 * BC_WIDTH
W_COLS = F_WIDTH + SSD_PAD + XBC_PAD
CTX_COL0 = F_WIDTH + SSD_PAD
DT_ROWS = 32
DT_DIR_ROWS = 16
MOD_ROWS = 16
MOD_CTX_ROW = 8

SUBLANE = 8
LANE = 128
VMEM_LIMIT = 56 * 1024 * 1024


def _dot(a, b):
    return jnp.dot(a, b, preferred_element_type=F32)


def _dot_mixed(a_f32, b_bf16):
    return lax.dot_general(a_f32, b_bf16, (((1,), (0,)), ((), ())), preferred_element_type=F32)


def _silu(v):
    return v * jax.nn.sigmoid(v)


def _rms(v, g, n=None):
    n = v.shape[-1] if n is None else n
    ms = jnp.sum(v * v, axis=-1, keepdims=True) * (1.0 / n)
    return v * lax.rsqrt(ms + EPS) * g


def _ada_kernel(c_ref, w_ref, b_ref, o_ref):
    s = _silu(c_ref[...]).astype(BF16)
    o_ref[...] = _dot(s, w_ref[...].astype(BF16)) + b_ref[...]


def _ada_call(cc, w_ada, b_ada, layer):
    n = w_ada.shape[2]
    tn = 2048
    return pl.pallas_call(
        _ada_kernel,
        grid=(n // tn,),
        in_specs=[
            pl.BlockSpec((MOD_ROWS, D_MODEL), lambda j: (0, 0)),
            pl.BlockSpec((None, D_MODEL, tn), lambda j: (layer, 0, j)),
            pl.BlockSpec((1, tn), lambda j: (0, j)),
        ],
        out_specs=pl.BlockSpec((MOD_ROWS, tn), lambda j: (0, j)),
        out_shape=jax.ShapeDtypeStruct((MOD_ROWS, n), F32),
        compiler_params=pltpu.CompilerParams(dimension_semantics=("arbitrary",)),
        name="ada_mod",
    )(cc, w_ada, b_ada)


CONV_STRIDE = SUBLANE + 1
CONV_PIECE = SUBLANE * CONV_STRIDE


def _conv_pieces(tm, row_len):
    padded = (tm // row_len) * (row_len + SUBLANE)
    return -(-padded // CONV_PIECE)


def _inproj_kernel(*refs, ctx, tm, row_len):
    if ctx:
        (x_ref, mod_ref, g_ref, w_ref, wdt_ref, cw_ref, cb_ref,
         xs_ref, bt_ref, cm_ref, dt_ref, pad_ref, cv_ref) = refs
    else:
        (x_ref, mod_ref, g_ref, w_ref, wdt_ref, cw_ref, cb_ref, wc_ref,
         uu_ref, z_ref, xs_ref, bt_ref, cm_ref, dt_ref, pad_ref, cv_ref) = refs

    xt = x_ref[0]
    if ctx:
        shift = mod_ref[MOD_CTX_ROW:MOD_CTX_ROW + 1, 0:D_MODEL]
        scale = mod_ref[MOD_CTX_ROW:MOD_CTX_ROW + 1, D_MODEL:2 * D_MODEL]
    else:
        b = pl.program_id(0)
        shift = mod_ref[pl.ds(b, 1), 0:D_MODEL]
        scale = mod_ref[pl.ds(b, 1), D_MODEL:2 * D_MODEL]
    h = _rms(xt, g_ref[...]) * (1.0 + scale) + shift
    hb = h.astype(BF16)

    if not ctx:
        uf = _dot(hb, w_ref[:, 0:F_WIDTH])
        uu_ref[0] = _dot(uf.astype(BF16), wc_ref[...].astype(BF16)).astype(BF16)
        z_ref[0] = _silu(_dot(hb, w_ref[:, F_WIDTH:CTX_COL0])).astype(BF16)
        xbc = _dot(hb, w_ref[:, CTX_COL0:W_COLS])
    else:
        xbc = _dot(hb, w_ref[:, CTX_COL0:W_COLS])
    dt_ref[0] = lax.dot_general(wdt_ref[...], hb, (((1,), (1,)), ((), ())),
                                preferred_element_type=F32)

    pitch = row_len + SUBLANE
    nrow = tm // row_len
    npiece = _conv_pieces(tm, row_len)
    data_end = SUBLANE + nrow * pitch
    alloc_end = pad_ref.shape[1]
    zrow = jnp.zeros((SUBLANE, LANE), F32)
    for j in range(XBC_PAD // LANE):
        pad_ref[j, 0:SUBLANE, :] = zrow
        for r in range(nrow):
            base = SUBLANE + r * pitch
            pad_ref[j, base:base + row_len, :] = xbc[r * row_len:(r + 1) * row_len,
                                                     j * LANE:(j + 1) * LANE]
            pad_ref[j, base + row_len:base + pitch, :] = zrow
        for z0 in range(data_end, alloc_end, SUBLANE):
            pad_ref[j, z0:z0 + SUBLANE, :] = zrow

    for j in range(XBC_PAD // LANE):
        wk = [jnp.broadcast_to(cw_ref[k:k + 1, j * LANE:(j + 1) * LANE], (SUBLANE, LANE))
              for k in range(D_CONV)]
        bias = jnp.broadcast_to(cb_ref[:, j * LANE:(j + 1) * LANE], (SUBLANE, LANE))
        for p in range(npiece):
            base = SUBLANE + p * CONV_PIECE
            wins = {v: pad_ref[j, pl.ds(base + v, SUBLANE, stride=CONV_STRIDE), :]
                    for v in range(-HALF_CONV, CONV_STRIDE + HALF_CONV)}
            for a in range(CONV_STRIDE):
                acc = bias
                for k in range(D_CONV):
                    acc = acc + wins[a + k - HALF_CONV] * wk[k]
                cv_ref[j, pl.ds(base + a, SUBLANE, stride=CONV_STRIDE), :] = acc

    def conv_rows(j, t0, n):
        parts = []
        for t in range(t0, t0 + n, min(n, row_len)):
            src = SUBLANE + (t // row_len) * pitch + t % row_len
            parts.append(cv_ref[j, src:src + min(n, row_len), :])
        return _silu(parts[0] if len(parts) == 1 else jnp.concatenate(parts, axis=0))

    for j in range(XBC_PAD // LANE):
        c0 = j * LANE
        for ci in range(tm // CHUNK):
            val = conv_rows(j, ci * CHUNK, CHUNK)
            rows = slice(ci * CHUNK, (ci + 1) * CHUNK)
            if c0 < SSD_PAD:
                xs_ref[0, rows, c0:c0 + LANE] = val.astype(BF16)
            elif c0 < SSD_PAD + BC_WIDTH:
                cc = c0 - SSD_PAD
                bt_ref[0, ci, cc:cc + LANE, :] = val.T.astype(BF16)
            else:
                cc = c0 - SSD_PAD - BC_WIDTH
                cm_ref[0, rows, cc:cc + LANE] = val.astype(BF16)


def _inproj_call(xin, mod, g, w, wdt, cw, cb, wc, *, ctx, tm, row_len):
    bsz, seq, _ = xin.shape
    nt = seq // tm
    const2 = lambda b, j: (0, 0)
    tile3 = lambda b, j: (b, j, 0)
    in_specs = [
        pl.BlockSpec((1, tm, D_MODEL), tile3),
        pl.BlockSpec((MOD_ROWS, N_MOD * D_MODEL), const2),
        pl.BlockSpec((1, D_MODEL), const2),
        pl.BlockSpec((D_MODEL, W_COLS), const2),
        pl.BlockSpec((DT_ROWS, D_MODEL), const2),
        pl.BlockSpec((D_CONV, XBC_PAD), const2),
        pl.BlockSpec((1, XBC_PAD), const2),
    ]
    args = [xin, mod, g, w, wdt, cw, cb]
    out_specs = []
    out_shape = []
    if not ctx:
        in_specs.append(pl.BlockSpec((F_WIDTH, 2 * F_WIDTH), const2))
        args.append(wc)
        out_specs += [pl.BlockSpec((1, tm, 2 * F_WIDTH), tile3),
                      pl.BlockSpec((1, tm, SSD_PAD), tile3)]
        out_shape += [jax.ShapeDtypeStruct((bsz, seq, 2 * F_WIDTH), BF16),
                      jax.ShapeDtypeStruct((bsz, seq, SSD_PAD), BF16)]
    out_specs += [pl.BlockSpec((1, tm, SSD_PAD), tile3),
                  pl.BlockSpec((1, tm // CHUNK, BC_WIDTH, CHUNK), lambda b, j: (b, j, 0, 0)),
                  pl.BlockSpec((1, tm, BC_WIDTH), tile3),
                  pl.BlockSpec((1, DT_ROWS, tm), lambda b, j: (b, 0, j))]
    out_shape += [jax.ShapeDtypeStruct((bsz, seq, SSD_PAD), BF16),
                  jax.ShapeDtypeStruct((bsz, seq // CHUNK, BC_WIDTH, CHUNK), BF16),
                  jax.ShapeDtypeStruct((bsz, seq, BC_WIDTH), BF16),
                  jax.ShapeDtypeStruct((bsz, DT_ROWS, seq), F32)]
    pad_rows = 2 * SUBLANE + _conv_pieces(tm, row_len) * CONV_PIECE
    conv_scratch = pltpu.VMEM((XBC_PAD // LANE, pad_rows, LANE), F32)
    return pl.pallas_call(
        functools.partial(_inproj_kernel, ctx=ctx, tm=tm, row_len=row_len),
        grid=(bsz, nt),
        in_specs=in_specs,
        out_specs=out_specs,
        out_shape=out_shape,
        scratch_shapes=[conv_scratch, conv_scratch],
        compiler_params=pltpu.CompilerParams(
            dimension_semantics=("arbitrary", "arbitrary"),
            vmem_limit_bytes=VMEM_LIMIT),
        name="inproj_ctx" if ctx else "inproj_lat",
    )(*args)


def _ssd_kernel(xs_ref, bt_ref, cm_ref, z_ref, dt_ref,
                xsc_ref, btc_ref, cmc_ref, dtc_ref,
                dtb_ref, alog_ref, dskip_ref, gssd_ref,
                out_ref, h_ref, y_ref, r2_ref, w2_ref, ee_ref, qt_ref, *, seq, ctx_len):
    nchunk = seq // CHUNK
    nchunk_ctx = ctx_len // CHUNK

    h_ref[...] = jnp.zeros(h_ref.shape, F32)

    bias = dtb_ref[...]
    nega = -jnp.exp(alog_ref[...])
    sub_i = lax.broadcasted_iota(jnp.int32, (CHUNK, CHUNK), 0)
    lane_i = lax.broadcasted_iota(jnp.int32, (CHUNK, CHUNK), 1)
    tri = (lane_i <= sub_i, lane_i >= sub_i)
    lane_lo = lane_i < HEAD_DIM
    lane_row = lax.broadcasted_iota(jnp.int32, (1, CHUNK), 1)

    def colb(mat_t, idx):
        return jnp.broadcast_to(mat_t[:, idx:idx + 1], (CHUNK, CHUNK))

    def prepare(d, raws):
        r0 = d * DT_DIR_ROWS
        n = len(raws)
        v = jnp.concatenate(raws, axis=0) + jnp.concatenate([bias[r0:r0 + DT_DIR_ROWS]] * n, axis=0)
        dt = jnp.maximum(v, 0.0) + jnp.log1p(jnp.exp(-jnp.abs(v)))
        da = dt * jnp.concatenate([nega[r0:r0 + DT_DIR_ROWS]] * n, axis=0)
        lane_n = lax.broadcasted_iota(jnp.int32, da.shape, 1)
        cs = da
        sh = 1
        while sh < CHUNK:
            if d == 0:
                cs = cs + jnp.where(lane_n >= sh, pltpu.roll(cs, sh, axis=1), 0.0)
            else:
                cs = cs + jnp.where(lane_n < CHUNK - sh, pltpu.roll(cs, CHUNK - sh, axis=1), 0.0)
            sh *= 2
        a_end = jnp.sum(da, axis=1, keepdims=True)
        cs2 = cs * LOG2E
        r2 = cs2 - jnp.log2(dt)
        w2 = jnp.exp2(a_end * LOG2E - r2)
        ee = jnp.broadcast_to(jnp.exp(a_end), da.shape)
        return cs2, r2, w2, ee

    def store_prep(d, k0, raws):
        cs2, r2, w2, ee = prepare(d, raws)
        for i in range(len(raws)):
            rows = slice(i * DT_DIR_ROWS, (i + 1) * DT_DIR_ROWS)
            r2_ref[d, k0 + i] = r2[rows]
            w2_ref[d, k0 + i] = w2[rows]
            ee_ref[d, k0 + i] = ee[rows]
            qt_ref[d, k0 + i] = jnp.concatenate(
                [cs2[rows], jnp.zeros((CHUNK - DT_DIR_ROWS, CHUNK), F32)], axis=0).T

    def direction(d, x, btc, cc, k, row0, mode):
        with_output = mode is not None
        r2 = r2_ref[d, k]
        w2 = w2_ref[d, k]
        ee = ee_ref[d, k]
        q_t = qt_ref[d, k]
        if with_output:
            scores_all = [_dot(cc[:, g * D_STATE:(g + 1) * D_STATE],
                               btc[g * D_STATE:(g + 1) * D_STATE, :]) for g in range(SSD_GROUPS)]

        s_all = []
        for g in range(SSD_GROUPS):
            bt = btc[g * D_STATE:(g + 1) * D_STATE, :]
            xg = x[:, g * GROUP_PAD:(g + 1) * GROUP_PAD]
            heads = [g * HEADS_PER_GROUP + r for r in range(HEADS_PER_GROUP)]
            btf = bt.astype(F32)
            btw = [(btf * w2[hd:hd + 1, :]).astype(BF16) for hd in heads]
            s01 = _dot(jnp.concatenate([btw[0], btw[1]], axis=0), xg[:, 0:CHUNK])
            s0 = jnp.where(lane_lo, s01[0:D_STATE], s01[D_STATE:2 * D_STATE])
            s1 = _dot(btw[2], xg[:, CHUNK:2 * CHUNK])
            s_all.append(jnp.concatenate([s0, s1], axis=1))

        for g in range(SSD_GROUPS):
            cg = cc[:, g * D_STATE:(g + 1) * D_STATE]
            hg = h_ref[d, g]
            xg = x[:, g * GROUP_PAD:(g + 1) * GROUP_PAD]
            t0 = xg[:, 0:CHUNK]
            t1 = xg[:, CHUNK:2 * CHUNK]
            heads = [g * HEADS_PER_GROUP + r for r in range(HEADS_PER_GROUP)]
            if with_output:
                scores = scores_all[g]
                hb = hg.astype(BF16)
                lhs = []
                cgf = cg.astype(F32)
                for hd in heads:
                    col = colb(q_t, hd)
                    seg = col - r2[hd:hd + 1, :]
                    m = scores * jnp.exp2(jnp.where(tri[d], seg, -jnp.inf))
                    ce = cgf * jnp.exp2(col)
                    lhs.append(jnp.concatenate([m, ce], axis=1))
                rhs0 = jnp.concatenate([t0, hb[:, 0:CHUNK]], axis=0)
                rhs1 = jnp.concatenate([t1, hb[:, CHUNK:2 * CHUNK]], axis=0)
                y01 = _dot_mixed(jnp.concatenate([lhs[0], lhs[1]], axis=0), rhs0)
                y0 = jnp.where(lane_lo, y01[0:CHUNK], y01[CHUNK:2 * CHUNK])
                y1 = _dot_mixed(lhs[2], rhs1)
                y_g = jnp.concatenate([y0, y1], axis=1)
                if mode:
                    y_g = y_g + y_ref[pl.ds(row0, CHUNK), g * GROUP_PAD:(g + 1) * GROUP_PAD]
                y_ref[pl.ds(row0, CHUNK), g * GROUP_PAD:(g + 1) * GROUP_PAD] = y_g
            er = [ee[hd:hd + 1, :] for hd in heads]
            e3 = jnp.concatenate([jnp.where(lane_row < HEAD_DIM, er[0], er[1]), er[2]], axis=1)
            h_ref[d, g] = e3 * hg + s_all[g]

    for d in range(2):
        rows_d = slice(d * DT_DIR_ROWS, (d + 1) * DT_DIR_ROWS)
        store_prep(d, 0, [dtc_ref[0, rows_d, c * CHUNK:(c + 1) * CHUNK] for c in range(nchunk_ctx)])
        store_prep(d, nchunk_ctx, [dt_ref[0, rows_d, c * CHUNK:(c + 1) * CHUNK] for c in range(nchunk)])

    for i in range(nchunk_ctx):
        for d in range(2):
            ci = i if d == 0 else nchunk_ctx - 1 - i
            rows = slice(ci * CHUNK, (ci + 1) * CHUNK)
            direction(d, xsc_ref[0, rows, :], btc_ref[0, ci], cmc_ref[0, rows, :], ci, 0, None)

    def finish(row0):
        xr = xs_ref[0, pl.ds(row0, CHUNK), :].astype(F32)
        zr = z_ref[0, pl.ds(row0, CHUNK), :].astype(F32)
        y = y_ref[pl.ds(row0, CHUNK), :] + xr * dskip_ref[...]
        y = y * zr
        out_ref[0, pl.ds(row0, CHUNK), :] = _rms(y, gssd_ref[...], n=SSD_WIDTH).astype(BF16)

    def step(i, carry, *, accumulate):
        row0s = []
        for d in range(2):
            ci = i if d == 0 else nchunk - 1 - i
            row0 = pl.multiple_of(ci * CHUNK, CHUNK)
            row0s.append(row0)
            direction(d, xs_ref[0, pl.ds(row0, CHUNK), :], bt_ref[0, ci],
                      cm_ref[0, pl.ds(row0, CHUNK), :], nchunk_ctx + ci, row0, accumulate)
        if accumulate:
            for row0 in row0s:
                finish(row0)
        return carry

    half = nchunk // 2
    lax.fori_loop(0, half, functools.partial(step, accumulate=False), 0, unroll=4)
    lax.fori_loop(half, nchunk, functools.partial(step, accumulate=True), 0, unroll=2)


def _ssd_call(xs, bm, cm, z, dt, xsc, bmc, cmc, dtc, dtb, alog, dskip, gssd):
    bsz, seq, _ = xs.shape
    ctx_len = xsc.shape[1]
    nck = (seq + ctx_len) // CHUNK
    per_b = lambda b: (b, 0, 0)
    const2 = lambda b: (0, 0)
    return pl.pallas_call(
        functools.partial(_ssd_kernel, seq=seq, ctx_len=ctx_len),
        grid=(bsz,),
        in_specs=[
            pl.BlockSpec((1, seq, SSD_PAD), per_b),
            pl.BlockSpec((1, seq // CHUNK, BC_WIDTH, CHUNK), lambda b: (b, 0, 0, 0)),
            pl.BlockSpec((1, seq, BC_WIDTH), per_b),
            pl.BlockSpec((1, seq, SSD_PAD), per_b),
            pl.BlockSpec((1, DT_ROWS, seq), per_b),
            pl.BlockSpec((1, ctx_len, SSD_PAD), per_b),
            pl.BlockSpec((1, ctx_len // CHUNK, BC_WIDTH, CHUNK), lambda b: (b, 0, 0, 0)),
            pl.BlockSpec((1, ctx_len, BC_WIDTH), per_b),
            pl.BlockSpec((1, DT_ROWS, ctx_len), per_b),
            pl.BlockSpec((DT_ROWS, 1), const2),
            pl.BlockSpec((DT_ROWS, 1), const2),
            pl.BlockSpec((1, SSD_PAD), const2),
            pl.BlockSpec((1, SSD_PAD), const2),
        ],
        out_specs=pl.BlockSpec((1, seq, SSD_PAD), per_b),
        out_shape=jax.ShapeDtypeStruct((bsz, seq, SSD_PAD), BF16),
        scratch_shapes=[
            pltpu.VMEM((2, SSD_GROUPS, D_STATE, GROUP_PAD), F32),
            pltpu.VMEM((seq, SSD_PAD), F32),
            pltpu.VMEM((2, nck, DT_DIR_ROWS, CHUNK), F32),
            pltpu.VMEM((2, nck, DT_DIR_ROWS, CHUNK), F32),
            pltpu.VMEM((2, nck, DT_DIR_ROWS, CHUNK), F32),
            pltpu.VMEM((2, nck, CHUNK, CHUNK), F32),
        ],
        compiler_params=pltpu.CompilerParams(
            dimension_semantics=("arbitrary",), vmem_limit_bytes=VMEM_LIMIT),
        name="ssd_scan",
    )(xs, bm, cm, z, dt, xsc, bmc, cmc, dtc, dtb, alog, dskip, gssd)


FFN_CHUNK = 256
MIX_SPLIT = 2


def _out_ffn_kernel(x_ref, yn_ref, uu_ref, cl_ref, sl_ref, mod_ref,
                    wof_ref, woy_ref, gpm_ref, gpf_ref, gpo_ref,
                    wg_ref, wu_ref, wd_ref, out_ref):
    b = pl.program_id(1)

    def mod(k):
        return mod_ref[pl.ds(b, 1), k * D_MODEL:(k + 1) * D_MODEL]

    half = x_ref.shape[1] // MIX_SPLIT
    x1_parts, h2_parts = [], []
    for r0 in range(0, x_ref.shape[1], half):
        rows = slice(r0, r0 + half)
        yf = (_dot(cl_ref[rows, :], uu_ref[0, :, 0:F_WIDTH])
              + _dot(sl_ref[rows, :], uu_ref[0, :, F_WIDTH:2 * F_WIDTH]))
        mix = _dot(yf.astype(BF16), wof_ref[...]) + _dot(yn_ref[0, rows, :], woy_ref[...])
        x1_h = x_ref[0, rows, :] + mod(2) * _rms(mix, gpm_ref[...])
        x1_parts.append(x1_h)
        h2_parts.append((_rms(x1_h, gpf_ref[...]) * (1.0 + mod(4)) + mod(3)).astype(BF16))
    x1 = jnp.concatenate(x1_parts, axis=0)
    h2 = jnp.concatenate(h2_parts, axis=0)
    ffn = None
    for c0 in range(0, wg_ref.shape[1], FFN_CHUNK):
        gate = _dot(h2, wg_ref[:, c0:c0 + FFN_CHUNK])
        up = _dot(h2, wu_ref[:, c0:c0 + FFN_CHUNK])
        act = (_silu(gate) * up).astype(BF16)
        part = _dot(act, wd_ref[c0:c0 + FFN_CHUNK, :])
        ffn = part if ffn is None else ffn + part
    out_ref[0] = x1 + mod(5) * _rms(ffn, gpo_ref[...])


def _out_ffn_call(x, yn, uu, cl, sl, mod, wof, woy, gpm, gpf, gpo, wg, wu, wd, *, tm):
    bsz, seq, _ = x.shape
    d_ff = wg.shape[1]
    nt = seq // tm
    const2 = lambda j, b: (0, 0)
    tile3 = lambda j, b: (b, j, 0)
    single = dict(pipeline_mode=pl.Buffered(1))
    return pl.pallas_call(
        _out_ffn_kernel,
        grid=(nt, bsz),
        in_specs=[
            pl.BlockSpec((1, tm, D_MODEL), tile3),
            pl.BlockSpec((1, tm, SSD_PAD), tile3),
            pl.BlockSpec((1, seq, 2 * F_WIDTH), lambda j, b: (b, 0, 0)),
            pl.BlockSpec((tm, seq), lambda j, b: (j, 0)),
            pl.BlockSpec((tm, seq), lambda j, b: (j, 0)),
            pl.BlockSpec((MOD_ROWS, N_MOD * D_MODEL), const2),
            pl.BlockSpec((F_WIDTH, D_MODEL), const2, **single),
            pl.BlockSpec((SSD_PAD, D_MODEL), const2, **single),
            pl.BlockSpec((1, D_MODEL), const2),
            pl.BlockSpec((1, D_MODEL), const2),
            pl.BlockSpec((1, D_MODEL), const2),
            pl.BlockSpec((D_MODEL, d_ff), const2, **single),
            pl.BlockSpec((D_MODEL, d_ff), const2, **single),
            pl.BlockSpec((d_ff, D_MODEL), const2, **single),
        ],
        out_specs=pl.BlockSpec((1, tm, D_MODEL), tile3),
        out_shape=jax.ShapeDtypeStruct((bsz, seq, D_MODEL), F32),
        compiler_params=pltpu.CompilerParams(
            dimension_semantics=("arbitrary", "arbitrary"),
            vmem_limit_bytes=VMEM_LIMIT),
        name="out_ffn",
    )(x, yn, uu, cl, sl, mod, wof, woy, gpm, gpf, gpo, wg, wu, wd)


@functools.lru_cache(maxsize=None)
def _dft_tables(seq):
    k = np.arange(seq, dtype=np.int64)
    ang = 2.0 * np.pi * ((k[:, None] * k[None, :]) % seq).astype(np.float64) / seq
    scale = 1.0 / math.sqrt(seq)
    cl = (np.cos(ang) * scale).astype(np.float32)
    sl = (np.sin(ang) * scale).astype(np.float32)
    c = np.arange(FGROUP_DIM, dtype=np.int64)
    angc = 2.0 * np.pi * ((c[:, None] * c[None, :]) % FGROUP_DIM).astype(np.float64) / FGROUP_DIM
    sc = 1.0 / math.sqrt(FGROUP_DIM)
    wc = np.zeros((F_WIDTH, 2 * F_WIDTH), np.float32)
    for g in range(N_FGROUPS):
        s = slice(g * FGROUP_DIM, (g + 1) * FGROUP_DIM)
        wc[s, g * FGROUP_DIM:(g + 1) * FGROUP_DIM] = np.cos(angc) * sc
        wc[s, F_WIDTH + g * FGROUP_DIM:F_WIDTH + (g + 1) * FGROUP_DIM] = -np.sin(angc) * sc
    return cl, sl, wc


def _pad_groups(a, axis):
    a = jnp.moveaxis(a, axis, -1)
    lead = a.shape[:-1]
    a = a.reshape(lead + (SSD_GROUPS, HEADS_PER_GROUP * HEAD_DIM))
    a = jnp.pad(a, [(0, 0)] * len(lead) + [(0, 0), (0, GROUP_PAD - HEADS_PER_GROUP * HEAD_DIM)])
    a = a.reshape(lead + (SSD_PAD,))
    return jnp.moveaxis(a, -1, axis)


def _pad_dirs(a):
    a = jnp.pad(a, ((0, 0), (0, DT_DIR_ROWS - SSD_HEADS)))
    return a.reshape(DT_ROWS, 1)


def kernel(x, c, ctx, c_ctx, w_ada, b_ada, g_pre_mix, g_post_mix, g_pre_ffn, g_post_ffn,
           w_in, conv_w, conv_b, dt_bias, a_log, d_skip, g_ssd, w_out, w_gate, w_up, w_down):
    bsz, seq, _ = x.shape
    ctx_len = ctx.shape[1]
    l = 0
    cl_np, sl_np, wc_np = _dft_tables(seq)
    cl = jnp.asarray(cl_np).astype(BF16)
    sl = jnp.asarray(sl_np).astype(BF16)
    wc = jnp.asarray(wc_np)

    cc = jnp.concatenate(
        [c, c_ctx[None, :], jnp.zeros((MOD_ROWS - bsz - 1, D_MODEL), F32)], axis=0)
    mod = _ada_call(cc, w_ada, b_ada[l][None, :], l)

    wi = w_in[l]
    w_cat = jnp.concatenate([
        wi[:, :F_WIDTH],
        _pad_groups(wi[:, F_WIDTH:XBC_OFF], 1),
        _pad_groups(wi[:, XBC_OFF:XBC_OFF + SSD_WIDTH], 1),
        wi[:, XBC_OFF + SSD_WIDTH:XBC_OFF + CONV_DIM],
    ], axis=1).astype(BF16)
    w_dt = wi[:, XBC_OFF + CONV_DIM:].T.reshape(2, SSD_HEADS, D_MODEL)
    w_dt = jnp.pad(w_dt, ((0, 0), (0, DT_DIR_ROWS - SSD_HEADS), (0, 0))).reshape(DT_ROWS, D_MODEL)
    w_dt = w_dt.astype(BF16)
    cw = conv_w[l].T
    cw = jnp.concatenate([_pad_groups(cw[:, :SSD_WIDTH], 1), cw[:, SSD_WIDTH:]], axis=1)
    cb = conv_b[l][None, :]
    cb = jnp.concatenate([_pad_groups(cb[:, :SSD_WIDTH], 1), cb[:, SSD_WIDTH:]], axis=1)
    g_pre = g_pre_mix[l][None, :]

    xs_c, bm_c, cm_c, dt_c = _inproj_call(
        ctx, mod, g_pre, w_cat, w_dt, cw, cb, None, ctx=True, tm=ctx_len, row_len=ctx_len)
    uu, z, xs, bm, cm, dt = _inproj_call(
        x, mod, g_pre, w_cat, w_dt, cw, cb, wc, ctx=False, tm=512, row_len=GRID_W)

    dskip = _pad_groups(jnp.repeat(d_skip[l], HEAD_DIM)[None, :], 1)
    gssd = _pad_groups(g_ssd[l][None, :], 1)
    yn = _ssd_call(xs, bm, cm, z, dt, xs_c, bm_c, cm_c, dt_c,
                   _pad_dirs(dt_bias[l]), _pad_dirs(a_log[l]), dskip, gssd)

    wo = w_out[l]
    wof = wo[:F_WIDTH].astype(BF16)
    woy = _pad_groups(wo[F_WIDTH:], 0).astype(BF16)
    return _out_ffn_call(
        x, yn, uu, cl, sl, mod, wof, woy,
        g_post_mix[l][None, :], g_pre_ffn[l][None, :], g_post_ffn[l][None, :],
        w_gate[l].astype(BF16), w_up[l].astype(BF16), w_down[l].astype(BF16), tm=512)
```
